```python
import jax, jax.numpy as jnp
from jax import lax
import numpy as np

D_MODEL = 2048
BATCH = 1
SEQ = 8192
DEPTH = 2
DEC_BATCH = 128
DEC_SEQ = 8
PAST_LEN = 16384
PAGE_SIZE = 128

N_A_LAYERS = DEPTH // 2
N_B_LAYERS = DEPTH - N_A_LAYERS
GDN_HEAD_K = 128
GDN_HEAD_V = 128
GDN_K_HEADS = D_MODEL // 128
GDN_V_HEADS = 2 * GDN_K_HEADS
GDN_QK_DIM = GDN_K_HEADS * GDN_HEAD_K
GDN_V_DIM = GDN_V_HEADS * GDN_HEAD_V
GDN_CONV_DIM = 2 * GDN_QK_DIM + GDN_V_DIM
GDN_IN_DIM = GDN_CONV_DIM + GDN_V_DIM + 2 * GDN_V_HEADS
GDN_CONV = 4
GDN_CHUNK = 64
MLA_HEADS = D_MODEL // 128
MLA_NOPE = 128
MLA_ROPE = 64
MLA_V = 128
MLA_KV_RANK = 512
MLA_QK = MLA_NOPE + MLA_ROPE
MLA_KEY_DIM = MLA_KV_RANK + MLA_ROPE
MLA_SCALE = MLA_QK ** -0.5
MLA_Q_BLOCK = 128
ROPE_THETA = 10000.0
MOE_GROUPS = 4
MOE_EXPERTS_PER_GROUP = 8
MOE_EXPERTS = MOE_GROUPS * MOE_EXPERTS_PER_GROUP
MOE_TOP_K = 2
MOE_FF = 512
NORM_EPS = 1e-6

kernel_name = 'yoco_gdn_mla_hmoe_step'

F32 = jnp.float32


def rmsnorm(x, g):
    xf = x.astype(F32)
    y = xf * lax.rsqrt(jnp.mean(xf * xf, axis=-1, keepdims=True) + NORM_EPS)
    return (y * g.astype(F32)).astype(x.dtype)


def l2norm(x):
    xf = x.astype(F32)
    return xf * lax.rsqrt(jnp.sum(xf * xf, axis=-1, keepdims=True) + NORM_EPS)


def rope(x, pos):
    half = x.shape[-1] // 2
    inv = ROPE_THETA ** (-jnp.arange(half, dtype=F32) / half)
    ang = pos.astype(F32)[:, None] * inv[None, :]
    shape = (pos.shape[0],) + (1,) * (x.ndim - 3) + (half,)
    cos = jnp.cos(ang).reshape(shape)
    sin = jnp.sin(ang).reshape(shape)
    xf = x.astype(F32)
    x1, x2 = xf[..., :half], xf[..., half:]
    return jnp.concatenate([x1 * cos - x2 * sin, x2 * cos + x1 * sin], axis=-1).astype(x.dtype)


def gated_delta_chunked(q, k, v, g, beta, s0, chunk):
    q, k, v, g, beta = (t.astype(F32) for t in (q, k, v, g, beta))
    b, l, h, dk = k.shape
    dv = v.shape[-1]
    n = l // chunk

    def to_chunks(t):
        return jnp.moveaxis(t.reshape((b, n, chunk) + t.shape[2:]), 3, 1)

    qc, kc, vc, bc = to_chunks(q), to_chunks(k), to_chunks(v), to_chunks(beta)
    gc = jnp.cumsum(to_chunks(g), axis=-1)
    causal = jnp.tril(jnp.ones((chunk, chunk), bool))
    strict = jnp.tril(jnp.ones((chunk, chunk), bool), -1)
    diff = gc[..., :, None] - gc[..., None, :]
    decay = jnp.where(causal, jnp.exp(jnp.where(causal, diff, 0.0)), 0.0)
    kk = jnp.einsum('bhncd,bhnsd->bhncs', kc, kc)
    lmat = jnp.where(strict, bc[..., :, None] * kk * decay, 0.0)
    eye = jnp.eye(chunk, dtype=F32)
    tmat = lax.linalg.triangular_solve(eye + lmat, jnp.broadcast_to(eye, lmat.shape),
                                       left_side=True, lower=True, unit_diagonal=True)
    u = jnp.einsum('bhncs,bhnse->bhnce', tmat, vc * bc[..., None])
    w = jnp.einsum('bhncs,bhnsd->bhncd', tmat, kc * (bc * jnp.exp(gc))[..., None])
    intra = jnp.where(causal, jnp.einsum('bhncd,bhnsd->bhncs', qc, kc) * decay, 0.0)

    def step(s, xs):
        q_i, k_i, u_i, w_i, g_i, a_i = xs
        v_new = u_i - jnp.einsum('bhcd,bhde->bhce', w_i, s)
        o = (jnp.einsum('bhcd,bhde->bhce', q_i * jnp.exp(g_i)[..., None], s)
             + jnp.einsum('bhcs,bhse->bhce', a_i, v_new))
        g_last = g_i[..., -1]
        s = (s * jnp.exp(g_last)[..., None, None]
             + jnp.einsum('bhcd,bhce->bhde', k_i * jnp.exp(g_last[..., None] - g_i)[..., None], v_new))
        return s, o

    xs = tuple(jnp.moveaxis(t, 2, 0) for t in (qc, kc, u, w, gc, intra))
    s_fin, o = lax.scan(step, s0.astype(F32), xs)
    o = jnp.moveaxis(jnp.moveaxis(o, 0, 2), 1, 3).reshape(b, l, h, dv)
    return o, s_fin


def gdn_mixer(xn, conv_buf, s0, w_in, conv_w, a_log, dt_bias, norm_g, w_out, chunk):
    b, l, _ = xn.shape
    proj = xn @ w_in
    qkv = proj[..., :GDN_CONV_DIM]
    z = proj[..., GDN_CONV_DIM:GDN_CONV_DIM + GDN_V_DIM]
    beta_in = proj[..., GDN_CONV_DIM + GDN_V_DIM:GDN_CONV_DIM + GDN_V_DIM + GDN_V_HEADS]
    a_in = proj[..., GDN_CONV_DIM + GDN_V_DIM + GDN_V_HEADS:]
    seq = jnp.concatenate([conv_buf.astype(qkv.dtype), qkv], axis=1)
    conv = seq[:, 0:l] * conv_w[0]
    for j in range(1, GDN_CONV):
        conv = conv + seq[:, j:j + l] * conv_w[j]
    new_buf = seq[:, l:]
    act = jax.nn.silu(conv)
    q = l2norm(act[..., :GDN_QK_DIM].reshape(b, l, GDN_K_HEADS, GDN_HEAD_K)) * (GDN_HEAD_K ** -0.5)
    k = l2norm(act[..., GDN_QK_DIM:2 * GDN_QK_DIM].reshape(b, l, GDN_K_HEADS, GDN_HEAD_K))
    v = act[..., 2 * GDN_QK_DIM:].reshape(b, l, GDN_V_HEADS, GDN_HEAD_V)
    rep = GDN_V_HEADS // GDN_K_HEADS
    q = jnp.repeat(q, rep, axis=2)
    k = jnp.repeat(k, rep, axis=2)
    beta = jax.nn.sigmoid(beta_in.astype(F32))
    g = -jnp.exp(a_log.astype(F32)) * jax.nn.softplus(a_in.astype(F32) + dt_bias.astype(F32))
    o, s_new = gated_delta_chunked(q, k, v, g, beta, s0, chunk)
    o = rmsnorm(o, norm_g) * jax.nn.silu(z.reshape(b, l, GDN_V_HEADS, GDN_HEAD_V).astype(F32))
    o = o.astype(xn.dtype).reshape(b, l, GDN_V_DIM)
    return o @ w_out, new_buf, s_new


def mla_shared_kv(h, pos, kv_norm, w_dkv, kv_latent_norm):
    c = rmsnorm(h, kv_norm) @ w_dkv
    ckv = rmsnorm(c[..., :MLA_KV_RANK], kv_latent_norm)
    kpe = rope(c[..., MLA_KV_RANK:], pos)
    return ckv, kpe


def mla_query(xn, pos, w_q, w_uk):
    b, l, _ = xn.shape
    q = (xn @ w_q).reshape(b, l, MLA_HEADS, MLA_QK)
    q_lat = jnp.einsum('blhn,chn->blhc', q[..., :MLA_NOPE], w_uk)
    q_pe = rope(q[..., MLA_NOPE:], pos)
    return jnp.concatenate([q_lat, q_pe], axis=-1) * MLA_SCALE


def mla_attend_prompt(q, ckv, kpe):
    b, l, h, ck = q.shape
    nb = l // MLA_Q_BLOCK
    keys = jnp.concatenate([ckv, kpe], axis=-1)
    qb = jnp.swapaxes(q.reshape(b, nb, MLA_Q_BLOCK, h, ck), 0, 1)
    kpos = jnp.arange(l)

    def block(args):
        i, qi = args
        s = jnp.einsum('bqhc,bkc->bhqk', qi, keys).astype(F32)
        qpos = i * MLA_Q_BLOCK + jnp.arange(MLA_Q_BLOCK)
        s = jnp.where(kpos[None, :] <= qpos[:, None], s, -jnp.inf)
        p = jax.nn.softmax(s, axis=-1).astype(ckv.dtype)
        return jnp.einsum('bhqk,bkc->bqhc', p, ckv)

    o = lax.map(block, (jnp.arange(nb), qb))
    return jnp.swapaxes(o, 0, 1).reshape(b, l, h, MLA_KV_RANK)


def mla_attend_paged(q, ckv_new, kpe_new, cache_ckv, cache_kpe, page_table):
    s_new = q.shape[1]
    past = page_table.shape[1] * PAGE_SIZE
    mask = jnp.concatenate([jnp.ones((s_new, past), bool), jnp.tril(jnp.ones((s_new, s_new), bool))], axis=1)

    def one(args):
        pt, qi, cn, kn = args
        ck = jnp.concatenate([cache_ckv[pt].reshape(past, MLA_KV_RANK).astype(cn.dtype), cn], axis=0)
        kp = jnp.concatenate([cache_kpe[pt].reshape(past, MLA_ROPE).astype(kn.dtype), kn], axis=0)
        keys = jnp.concatenate([ck, kp], axis=-1)
        s = jnp.einsum('qhc,kc->hqk', qi, keys).astype(F32)
        s = jnp.where(mask[None], s, -jnp.inf)
        p = jax.nn.softmax(s, axis=-1).astype(ck.dtype)
        return jnp.einsum('hqk,kc->qhc', p, ck)

    return lax.map(one, (page_table, q, ckv_new, kpe_new))


def mla_out(o_lat, w_uv, w_o):
    b, l = o_lat.shape[:2]
    o = jnp.einsum('blhc,chv->blhv', o_lat, w_uv).reshape(b, l, MLA_HEADS * MLA_V)
    return o @ w_o


def hier_moe(x, w_group, w_expert, w1, w3, w2):
    shp = x.shape
    t = x.reshape(-1, shp[-1])
    tf = t.astype(F32)
    pg = jax.nn.softmax(tf @ w_group.astype(F32), axis=-1)
    pg_top, g_idx = lax.top_k(pg, 1)
    le = (tf @ w_expert.astype(F32)).reshape(-1, MOE_GROUPS, MOE_EXPERTS_PER_GROUP)
    le_g = jnp.take_along_axis(le, g_idx[:, :, None], axis=1)[:, 0]
    pe_top, e_loc = lax.top_k(jax.nn.softmax(le_g, axis=-1), MOE_TOP_K)
    wts = pg_top * pe_top / jnp.sum(pe_top, axis=-1, keepdims=True)
    e_idx = g_idx * MOE_EXPERTS_PER_GROUP + e_loc
    comb = jnp.sum(jax.nn.one_hot(e_idx, MOE_EXPERTS, dtype=F32) * wts[..., None], axis=1).astype(t.dtype)
    y = jnp.zeros_like(t)
    for e in range(MOE_EXPERTS):
        hdn = jax.nn.silu(t @ w1[e]) * (t @ w3[e])
        y = y + comb[:, e:e + 1] * (hdn @ w2[e])
    return y.reshape(shp)


def setup_inputs(seed: int = 0) -> dict:
    key = jax.random.key(seed)
    ks = iter(jax.random.split(key, 40))

    def nrm(shape, scale):
        return jax.random.normal(next(ks), shape, F32) * scale

    def gain(shape):
        return 1.0 + nrm(shape, 0.02)

    n_pages = PAST_LEN // PAGE_SIZE
    n_pool = (DEC_BATCH * n_pages * 5) // 4
    perm = jax.random.permutation(next(ks), n_pool)[:DEC_BATCH * n_pages]
    page_table = perm.reshape(DEC_BATCH, n_pages).astype(jnp.int32)
    a_log = jnp.log(jax.random.uniform(next(ks), (N_A_LAYERS, GDN_V_HEADS), F32, 1.0, 16.0))
    return {
        'x_prompt': nrm((BATCH, SEQ, D_MODEL), 1.0),
        'x_sample': nrm((DEC_BATCH, DEC_SEQ, D_MODEL), 1.0),
        'state_ssm': nrm((N_A_LAYERS, DEC_BATCH, GDN_V_HEADS, GDN_HEAD_K, GDN_HEAD_V), 0.5),
        'state_conv': nrm((N_A_LAYERS, DEC_BATCH, GDN_CONV - 1, GDN_CONV_DIM), 1.0),
        'cache_ckv': nrm((n_pool, PAGE_SIZE, MLA_KV_RANK), 1.0),
        'cache_kpe': nrm((n_pool, PAGE_SIZE, MLA_ROPE), 1.0),
        'page_table': page_table,
        'norm_mix': gain((DEPTH, D_MODEL)),
        'norm_ffn': gain((DEPTH, D_MODEL)),
        'norm_final': gain((D_MODEL,)),
        'gdn_w_in': nrm((N_A_LAYERS, D_MODEL, GDN_IN_DIM), D_MODEL ** -0.5),
        'gdn_conv_w': nrm((N_A_LAYERS, GDN_CONV, GDN_CONV_DIM), GDN_CONV ** -0.5),
        'gdn_a_log': a_log,
        'gdn_dt_bias': nrm((N_A_LAYERS, GDN_V_HEADS), 0.5),
        'gdn_norm': gain((N_A_LAYERS, GDN_HEAD_V)),
        'gdn_w_out': nrm((N_A_LAYERS, GDN_V_DIM, D_MODEL), GDN_V_DIM ** -0.5),
        'kv_norm': gain((D_MODEL,)),
        'w_dkv': nrm((D_MODEL, MLA_KEY_DIM), D_MODEL ** -0.5),
        'kv_latent_norm': gain((MLA_KV_RANK,)),
        'w_uk': nrm((MLA_KV_RANK, MLA_HEADS, MLA_NOPE), MLA_KV_RANK ** -0.5),
        'w_uv': nrm((MLA_KV_RANK, MLA_HEADS, MLA_V), MLA_KV_RANK ** -0.5),
        'mla_w_q': nrm((N_B_LAYERS, D_MODEL, MLA_HEADS * MLA_QK), D_MODEL ** -0.5),
        'mla_w_o': nrm((N_B_LAYERS, MLA_HEADS * MLA_V, D_MODEL), (MLA_HEADS * MLA_V) ** -0.5),
        'moe_w_group': nrm((DEPTH, D_MODEL, MOE_GROUPS), D_MODEL ** -0.5),
        'moe_w_expert': nrm((DEPTH, D_MODEL, MOE_EXPERTS), D_MODEL ** -0.5),
        'moe_w1': nrm((DEPTH, MOE_EXPERTS, D_MODEL, MOE_FF), D_MODEL ** -0.5),
        'moe_w3': nrm((DEPTH, MOE_EXPERTS, D_MODEL, MOE_FF), D_MODEL ** -0.5),
        'moe_w2': nrm((DEPTH, MOE_EXPERTS, MOE_FF, D_MODEL), MOE_FF ** -0.5),
    }


def reference(x_prompt, x_sample, state_ssm, state_conv, cache_ckv, cache_kpe, page_table,
              norm_mix, norm_ffn, norm_final,
              gdn_w_in, gdn_conv_w, gdn_a_log, gdn_dt_bias, gdn_norm, gdn_w_out,
              kv_norm, w_dkv, kv_latent_norm, w_uk, w_uv,
              mla_w_q, mla_w_o,
              moe_w_group, moe_w_expert, moe_w1, moe_w3, moe_w2):
    past_len = page_table.shape[1] * PAGE_SIZE
    pos_p = jnp.arange(x_prompt.shape[1])
    pos_s = past_len + jnp.arange(x_sample.shape[1])
    hp, hs = x_prompt, x_sample
    ssm_p, conv_p, ssm_s, conv_s = [], [], [], []
    ckv_p = kpe_p = ckv_s = kpe_s = None
    for i in range(DEPTH):
        if i < N_A_LAYERS:
            wa = (gdn_w_in[i], gdn_conv_w[i], gdn_a_log[i], gdn_dt_bias[i], gdn_norm[i], gdn_w_out[i])
            zero_buf = jnp.zeros((hp.shape[0], GDN_CONV - 1, GDN_CONV_DIM), hp.dtype)
            zero_s = jnp.zeros((hp.shape[0], GDN_V_HEADS, GDN_HEAD_K, GDN_HEAD_V), F32)
            yp, bp, sp = gdn_mixer(rmsnorm(hp, norm_mix[i]), zero_buf, zero_s, *wa, GDN_CHUNK)
            ys, bs, ss = gdn_mixer(rmsnorm(hs, norm_mix[i]), state_conv[i], state_ssm[i], *wa, hs.shape[1])
            ssm_p.append(sp)
            conv_p.append(bp)
            ssm_s.append(ss)
            conv_s.append(bs)
        else:
            if i == N_A_LAYERS:
                ckv_p, kpe_p = mla_shared_kv(hp, pos_p, kv_norm, w_dkv, kv_latent_norm)
                ckv_s, kpe_s = mla_shared_kv(hs, pos_s, kv_norm, w_dkv, kv_latent_norm)
            j = i - N_A_LAYERS
            qp = mla_query(rmsnorm(hp, norm_mix[i]), pos_p, mla_w_q[j], w_uk)
            qs = mla_query(rmsnorm(hs, norm_mix[i]), pos_s, mla_w_q[j], w_uk)
            yp = mla_out(mla_attend_prompt(qp, ckv_p, kpe_p), w_uv, mla_w_o[j])
            ys = mla_out(mla_attend_paged(qs, ckv_s, kpe_s, cache_ckv, cache_kpe, page_table), w_uv, mla_w_o[j])
        hp = hp + yp
        hs = hs + ys
        moe_w = (moe_w_group[i], moe_w_expert[i], moe_w1[i], moe_w3[i], moe_w2[i])
        hp = hp + hier_moe(rmsnorm(hp, norm_ffn[i]), *moe_w)
        hs = hs + hier_moe(rmsnorm(hs, norm_ffn[i]), *moe_w)
    y_prompt = rmsnorm(hp, norm_final)
    y_sample = rmsnorm(hs, norm_final)
    ssm_prompt = jnp.stack(ssm_p)
    conv_prompt = jnp.stack(conv_p)
    ssm_sample = jnp.stack(ssm_s)
    conv_sample = jnp.stack(conv_s)
    return (y_prompt, y_sample, ssm_prompt, conv_prompt, ckv_p, kpe_p, ssm_sample, conv_sample, ckv_s, kpe_s)
```

```python
import functools
import math

import numpy as np
import jax
import jax.numpy as jnp
from jax import lax
from jax.experimental import pallas as pl
from jax.experimental.pallas import tpu as pltpu

F32 = jnp.float32
BF16 = jnp.bfloat16

NORM_EPS = 1e-6
ROPE_THETA = 10000.0
GDN_CHUNK = 64
MOE_TOP_K = 2
SUBLANES = 8
LANES = 128
VMEM_LIMIT_BYTES = 56 * 1024 * 1024
GDN_ROWS = 128
MOE_TILE = 256
NEG_BIG = -1e30


def _params(*sem):
    return pltpu.CompilerParams(dimension_semantics=sem, vmem_limit_bytes=VMEM_LIMIT_BYTES)


def _pick(n, pref):
    t = min(pref, n)
    while n % t:
        t -= SUBLANES
    return t


def _dot(a, b):
    return jnp.dot(a.astype(BF16), b.astype(BF16), preferred_element_type=F32)


def _dot_nt(a, b):
    return lax.dot_general(a.astype(BF16), b.astype(BF16), (((1,), (1,)), ((), ())),
                           preferred_element_type=F32)


def _split(a):
    hi = a.astype(BF16)
    lo = (a - hi.astype(F32)).astype(BF16)
    return hi, lo


def _dot3(a, b):
    ah, al = _split(a)
    bh, bl = _split(b)
    d = lambda x, y: jnp.dot(x, y, preferred_element_type=F32)
    return d(ah, bh) + (d(ah, bl) + d(al, bh))


def _sigmoid(x):
    return 1.0 / (1.0 + jnp.exp(-x))


def _silu(x):
    return x * _sigmoid(x)


def _rmsnorm_kernel(x_ref, g_ref, o_ref):
    x = x_ref[...]
    y = x * lax.rsqrt(jnp.mean(x * x, axis=-1, keepdims=True) + NORM_EPS) * g_ref[...]
    o_ref[...] = y.astype(o_ref.dtype)


def rmsnorm(x, g, out_dtype, row_start=0, rows=None):
    t, d = x.shape
    rows = t if rows is None else rows
    tm = _pick(math.gcd(rows, row_start) if row_start else rows, 512)
    off = row_start // tm
    return pl.pallas_call(
        _rmsnorm_kernel,
        grid=(rows // tm,),
        in_specs=[pl.BlockSpec((tm, d), lambda i: (i + off, 0)),
                  pl.BlockSpec((1, d), lambda i: (0, 0))],
        out_specs=pl.BlockSpec((tm, d), lambda i: (i, 0)),
        out_shape=jax.ShapeDtypeStruct((rows, d), out_dtype),
        compiler_params=_params("parallel"),
        name="rmsnorm",
    )(x, g.reshape(1, d))


def _matmul_kernel(*refs, has_res):
    if has_res:
        x_ref, w_ref, r_ref, o_ref, wb_ref = refs
    else:
        x_ref, w_ref, o_ref, wb_ref = refs

    @pl.when(pl.program_id(1) == 0)
    def _():
        wb_ref[...] = w_ref[...].astype(BF16)

    acc = jnp.dot(x_ref[...], wb_ref[...], preferred_element_type=F32)
    if has_res:
        acc = acc + r_ref[...]
    o_ref[...] = acc.astype(o_ref.dtype)


def matmul(x, w, out_dtype, n_cols=None, res=None, tm_pref=1024, tn_pref=1024):
    m, k = x.shape
    n = w.shape[1] if n_cols is None else n_cols
    tm = _pick(m, tm_pref)
    tn = _pick(n, tn_pref)
    in_specs = [pl.BlockSpec((tm, k), lambda j, i: (i, 0)),
                pl.BlockSpec((k, tn), lambda j, i: (0, j))]
    args = [x, w]
    if res is not None:
        in_specs.append(pl.BlockSpec((tm, tn), lambda j, i: (i, j)))
        args.append(res)
    return pl.pallas_call(
        functools.partial(_matmul_kernel, has_res=res is not None),
        grid=(n // tn, m // tm),
        in_specs=in_specs,
        out_specs=pl.BlockSpec((tm, tn), lambda j, i: (i, j)),
        out_shape=jax.ShapeDtypeStruct((m, n), out_dtype),
        scratch_shapes=[pltpu.VMEM((k, tn), BF16)],
        compiler_params=_params("parallel", "arbitrary"),
        name="matmul",
    )(*args)


def _gates_kernel(x_ref, w_ref, a_ref, dt_ref, o_ref, *, n_heads):
    ba = _dot(x_ref[...], w_ref[...])
    lane = lax.broadcasted_iota(jnp.int32, ba.shape, 1)
    beta = _sigmoid(ba)
    xs = ba + dt_ref[...]
    softplus = jnp.maximum(xs, 0.0) + jnp.log(1.0 + jnp.exp(-jnp.abs(xs)))
    g = -jnp.exp(a_ref[...]) * softplus
    o_ref[...] = jnp.where(lane < n_heads, beta, g)


def gdn_gates(xn, w_ba, a_log, dt_bias):
    t, d = xn.shape
    hv = a_log.shape[0]
    tm = _pick(t, 512)
    zeros = jnp.zeros((hv,), F32)
    a_pad = jnp.concatenate([zeros, a_log.astype(F32)]).reshape(1, 2 * hv)
    dt_pad = jnp.concatenate([zeros, dt_bias.astype(F32)]).reshape(1, 2 * hv)
    return pl.pallas_call(
        functools.partial(_gates_kernel, n_heads=hv),
        grid=(t // tm,),
        in_specs=[pl.BlockSpec((tm, d), lambda i: (i, 0)),
                  pl.BlockSpec((d, 2 * hv), lambda i: (0, 0)),
                  pl.BlockSpec((1, 2 * hv), lambda i: (0, 0)),
                  pl.BlockSpec((1, 2 * hv), lambda i: (0, 0))],
        out_specs=pl.BlockSpec((tm, 2 * hv), lambda i: (i, 0)),
        out_shape=jax.ShapeDtypeStruct((t, 2 * hv), F32),
        compiler_params=_params("parallel"),
        name="gdn_gates",
    )(xn, w_ba, a_pad, dt_pad)


def _conv_kernel(x_ref, prev_ref, st_ref, w_ref, o_ref, *, n_prompt_blocks, n_q_blocks, n_qk_blocks,
                 n_taps, head_dim):
    i = pl.program_id(0)
    j = pl.program_id(1)
    x3 = x_ref[...]
    prev = jnp.where(i == 0, 0.0, prev_ref[...])
    hist_prompt = jnp.concatenate([prev, x3[:-1]], axis=0)
    hist = jnp.where(i >= n_prompt_blocks, st_ref[...], hist_prompt)
    row = lax.broadcasted_iota(jnp.int32, x3.shape, 1)
    tap = lambda s: w_ref[n_taps - 1 - s:n_taps - s, :][None]
    acc = x3 * tap(0)
    for s in range(1, n_taps):
        shifted = jnp.where(row >= s, pltpu.roll(x3, s, 1), pltpu.roll(hist, s, 1))
        acc = acc + shifted * tap(s)
    act = _silu(acc).reshape(x3.shape[0] * SUBLANES, x3.shape[2])
    scale = jnp.where(j < n_q_blocks, head_dim ** -0.5, 1.0)
    is_qk = j < n_qk_blocks
    for c in range(act.shape[1] // head_dim):
        seg = act[:, c * head_dim:(c + 1) * head_dim]
        nrm = seg * (lax.rsqrt(jnp.sum(seg * seg, axis=-1, keepdims=True) + NORM_EPS) * scale)
        o_ref[:, c * head_dim:(c + 1) * head_dim] = jnp.where(is_qk, nrm, seg)


def gdn_conv(proj, state_hist, conv_w, tp, qk_dim, conv_dim, head_dim):
    t = proj.shape[0]
    n_taps = conv_w.shape[0]
    tm = _pick(math.gcd(tp, t - tp), 256)
    tc = _pick(qk_dim, 1024)
    tiles = tm // SUBLANES
    proj3 = proj.reshape(t // SUBLANES, SUBLANES, proj.shape[1])
    npb = tp // tm
    return pl.pallas_call(
        functools.partial(_conv_kernel, n_prompt_blocks=npb, n_q_blocks=qk_dim // tc,
                          n_qk_blocks=2 * qk_dim // tc, n_taps=n_taps, head_dim=head_dim),
        grid=(t // tm, conv_dim // tc),
        in_specs=[pl.BlockSpec((tiles, SUBLANES, tc), lambda i, j: (i, 0, j)),
                  pl.BlockSpec((1, SUBLANES, tc), lambda i, j: (jnp.maximum(i * tiles - 1, 0), 0, j)),
                  pl.BlockSpec((tiles, SUBLANES, tc), lambda i, j: (jnp.maximum(i - npb, 0), 0, j)),
                  pl.BlockSpec((n_taps, tc), lambda i, j: (0, j))],
        out_specs=pl.BlockSpec((tm, tc), lambda i, j: (i, j)),
        out_shape=jax.ShapeDtypeStruct((t, conv_dim), F32),
        compiler_params=_params("parallel", "parallel"),
        name="gdn_conv",
    )(proj3, proj3, state_hist, conv_w)


def _gdn_block(q, k, v, z, g_row, b_row, norm_g, states, chunk):
    r = GDN_ROWS
    n_groups = r // chunk
    ri = lax.broadcasted_iota(jnp.int32, (r, r), 0)
    ci = lax.broadcasted_iota(jnp.int32, (r, r), 1)
    shift = int(math.log2(chunk))
    rg = ri >> shift
    same = rg == (ci >> shift)
    causal = same & (ri >= ci)
    strict = same & (ri > ci)
    eye = ri == ci
    last = ci == (rg << shift) + (chunk - 1)

    g_b = jnp.broadcast_to(g_row, (r, r))
    gc_col = jnp.sum(jnp.where(causal, g_b, 0.0), axis=1, keepdims=True)
    gc_row = jnp.sum(jnp.where(eye, jnp.broadcast_to(gc_col, (r, r)), 0.0), axis=0, keepdims=True)
    gc_rb = jnp.broadcast_to(gc_row, (r, r))
    b_col = jnp.sum(jnp.where(eye, jnp.broadcast_to(b_row, (r, r)), 0.0), axis=1, keepdims=True)
    glast_col = jnp.sum(jnp.where(last, gc_rb, 0.0), axis=1, keepdims=True)
    decay = jnp.where(causal, jnp.exp(jnp.where(causal, gc_col - gc_rb, 0.0)), 0.0)

    kk = _dot_nt(k, k)
    lmat = jnp.where(strict, b_col * kk * decay, 0.0)
    tmat = jnp.where(eye, 1.0, 0.0) - lmat
    pw = lmat
    for _ in range(shift - 1):
        pw = _dot3(pw, pw)
        tmat = tmat + _dot3(tmat, pw)

    eg_col = jnp.exp(gc_col)
    u = _dot(tmat, v * b_col)
    w = _dot(tmat, k * (b_col * eg_col))
    qg = q * eg_col
    kd = k * jnp.exp(glast_col - gc_col)
    amat = jnp.where(causal, _dot_nt(q, k) * decay, 0.0)

    vn_parts, oi_parts = [], []
    for gi in range(n_groups):
        sl = slice(gi * chunk, (gi + 1) * chunk)
        s_g = states[gi]
        vn_parts.append(u[sl] - _dot(w[sl], s_g))
        oi_parts.append(_dot(qg[sl], s_g))
    vn = jnp.concatenate(vn_parts, axis=0)
    o = jnp.concatenate(oi_parts, axis=0) + _dot(amat, vn)

    kd_t = kd.T
    new_states = []
    for gi in range(n_groups):
        in_group = (ci >> shift) == gi
        upd = _dot(jnp.where(in_group, kd_t, 0.0), vn)
        e_last = jnp.exp(glast_col[gi * chunk:gi * chunk + 1, :])
        new_states.append(states[gi] * e_last + upd)

    on = o * lax.rsqrt(jnp.mean(o * o, axis=-1, keepdims=True) + NORM_EPS) * norm_g
    return on * _silu(z), new_states


def _gdn_prompt_kernel(q_ref, k_ref, v_ref, z_ref, g_ref, b_ref, ng_ref, o_ref, sfin_ref, s_scr, *,
                       heads_per_step, chunk, dv):
    c = pl.program_id(1)
    rep = GDN_ROWS // chunk

    @pl.when(c == 0)
    def _():
        s_scr[...] = jnp.zeros_like(s_scr)

    for hh in range(heads_per_step):
        qh = q_ref[:, hh * LANES:(hh + 1) * LANES]
        kh = k_ref[:, hh * LANES:(hh + 1) * LANES]
        stack = lambda ref: jnp.concatenate(
            [ref[:, (hh * rep + e) * dv:(hh * rep + e + 1) * dv] for e in range(rep)], axis=0)
        o, new_states = _gdn_block(
            jnp.concatenate([qh] * rep, axis=0), jnp.concatenate([kh] * rep, axis=0),
            stack(v_ref), stack(z_ref), g_ref[hh], b_ref[hh], ng_ref[...],
            [s_scr[hh * rep + e] for e in range(rep)], chunk)
        for e in range(rep):
            s_scr[hh * rep + e] = new_states[e]
            o_ref[:, (hh * rep + e) * dv:(hh * rep + e + 1) * dv] = o[e * chunk:(e + 1) * chunk].astype(o_ref.dtype)

    @pl.when(c == pl.num_programs(1) - 1)
    def _():
        sfin_ref[...] = s_scr[...]


def gdn_prompt(qkv, proj, gates_t, norm_g, tp, n_kh, n_vh, dk, dv, chunk):
    rep = n_vh // n_kh
    assert rep * chunk == GDN_ROWS and dk == LANES and dv == LANES
    hps = _pick(n_kh, 4) if n_kh % 4 == 0 else 1
    n_chunks = tp // chunk
    qk_blocks = n_kh // hps
    conv_dim = 2 * n_kh * dk + n_vh * dv

    def arrange(gt):
        a = gt[:, :tp].reshape(qk_blocks, hps, rep, n_chunks, chunk)
        return a.transpose(0, 3, 1, 2, 4).reshape(qk_blocks, n_chunks, hps, 1, GDN_ROWS)

    beta_a = arrange(gates_t[:n_vh])
    g_a = arrange(gates_t[n_vh:])
    vw = hps * rep * dv
    v_off = 2 * n_kh * dk // vw
    z_off = conv_dim // vw
    gate_spec = pl.BlockSpec((None, None, hps, 1, GDN_ROWS), lambda h, c: (h, c, 0, 0, 0))
    return pl.pallas_call(
        functools.partial(_gdn_prompt_kernel, heads_per_step=hps, chunk=chunk, dv=dv),
        grid=(qk_blocks, n_chunks),
        in_specs=[pl.BlockSpec((chunk, hps * dk), lambda h, c: (c, h)),
                  pl.BlockSpec((chunk, hps * dk), lambda h, c: (c, qk_blocks + h)),
                  pl.BlockSpec((chunk, vw), lambda h, c: (c, v_off + h)),
                  pl.BlockSpec((chunk, vw), lambda h, c: (c, z_off + h)),
                  gate_spec, gate_spec,
                  pl.BlockSpec((1, dv), lambda h, c: (0, 0))],
        out_specs=[pl.BlockSpec((chunk, vw), lambda h, c: (c, h)),
                   pl.BlockSpec((hps * rep, dk, dv), lambda h, c: (h, 0, 0))],
        out_shape=[jax.ShapeDtypeStruct((tp, n_vh * dv), BF16),
                   jax.ShapeDtypeStruct((n_vh, dk, dv), F32)],
        scratch_shapes=[pltpu.VMEM((hps * rep, dk, dv), F32)],
        compiler_params=_params("parallel", "arbitrary"),
        name="gdn_prompt",
    )(qkv, qkv, qkv, proj, g_a, beta_a, norm_g.reshape(1, dv))


def _gdn_sample_kernel(q_ref, k_ref, v_ref, z_ref, g_ref, b_ref, ng_ref, s_ref, o_ref, snew_ref, *, chunk):
    n_groups = GDN_ROWS // chunk
    o, new_states = _gdn_block(q_ref[...], k_ref[...], v_ref[...], z_ref[...], g_ref[...], b_ref[...],
                               ng_ref[...], [s_ref[gi] for gi in range(n_groups)], chunk)
    o_ref[...] = o.astype(o_ref.dtype)
    for gi in range(n_groups):
        snew_ref[gi] = new_states[gi]


def gdn_sample(qkv, proj, gates_t, norm_g, state, tp, n_kh, n_vh, dk, dv, chunk):
    rep = n_vh // n_kh
    t = qkv.shape[0]
    ts = t - tp
    assert dk == LANES and dv == LANES and ts % GDN_ROWS == 0 and tp % GDN_ROWS == 0
    n_blocks = ts // GDN_ROWS
    spb = GDN_ROWS // chunk
    row_off = tp // GDN_ROWS
    conv_dim = 2 * n_kh * dk + n_vh * dv

    def arrange(gt):
        return gt[:, tp:].reshape(n_vh, n_blocks, 1, GDN_ROWS)

    beta_a = arrange(gates_t[:n_vh])
    g_a = arrange(gates_t[n_vh:])
    gate_spec = pl.BlockSpec((None, None, 1, GDN_ROWS), lambda b, h: (h, b, 0, 0))
    state_spec = pl.BlockSpec((spb, None, dk, dv), lambda b, h: (b, h, 0, 0))
    return pl.pallas_call(
        functools.partial(_gdn_sample_kernel, chunk=chunk),
        grid=(n_blocks, n_vh),
        in_specs=[pl.BlockSpec((GDN_ROWS, dk), lambda b, h: (row_off + b, h // rep)),
                  pl.BlockSpec((GDN_ROWS, dk), lambda b, h: (row_off + b, n_kh + h // rep)),
                  pl.BlockSpec((GDN_ROWS, dv), lambda b, h: (row_off + b, 2 * n_kh * dk // dv + h)),
                  pl.BlockSpec((GDN_ROWS, dv), lambda b, h: (row_off + b, conv_dim // dv + h)),
                  gate_spec, gate_spec,
                  pl.BlockSpec((1, dv), lambda b, h: (0, 0)),
                  state_spec],
        out_specs=[pl.BlockSpec((GDN_ROWS, dv), lambda b, h: (b, h)),
                   state_spec],
        out_shape=[jax.ShapeDtypeStruct((ts, n_vh * dv), BF16),
                   jax.ShapeDtypeStruct(state.shape, F32)],
        compiler_params=_params("parallel", "parallel"),
        name="gdn_sample",
    )(qkv, qkv, qkv, proj, g_a, beta_a, norm_g.reshape(1, dv), state)


def _router_kernel(x_ref, g_ref, w_ref, xn_ref, r_ref, *, n_groups, per_group):
    x = x_ref[...]
    xn = x * lax.rsqrt(jnp.mean(x * x, axis=-1, keepdims=True) + NORM_EPS) * g_ref[...]
    xn_ref[...] = xn.astype(xn_ref.dtype)
    logits = _dot3(xn, w_ref[...])
    lane = lax.broadcasted_iota(jnp.int32, logits.shape, 1).astype(F32)
    n_exp = n_groups * per_group
    is_grp = lane < n_groups
    m = jnp.max(jnp.where(is_grp, logits, NEG_BIG), axis=-1, keepdims=True)
    e = jnp.where(is_grp, jnp.exp(jnp.where(is_grp, logits - m, 0.0)), 0.0)
    pg = e / jnp.sum(e, axis=-1, keepdims=True)
    pg_top = jnp.max(pg, axis=-1, keepdims=True)
    g_idx = jnp.min(jnp.where(is_grp & (pg == pg_top), lane, LANES), axis=-1, keepdims=True)
    elane = lane - n_groups
    in_grp = (elane >= g_idx * per_group) & (elane < (g_idx + 1.0) * per_group)
    m2 = jnp.max(jnp.where(in_grp, logits, NEG_BIG), axis=-1, keepdims=True)
    e2 = jnp.where(in_grp, jnp.exp(jnp.where(in_grp, logits - m2, 0.0)), 0.0)
    pe = e2 / jnp.sum(e2, axis=-1, keepdims=True)
    p1 = jnp.max(jnp.where(in_grp, pe, -1.0), axis=-1, keepdims=True)
    i1 = jnp.min(jnp.where(in_grp & (pe == p1), elane, LANES), axis=-1, keepdims=True)
    rest = in_grp & (elane != i1)
    p2 = jnp.max(jnp.where(rest, pe, -1.0), axis=-1, keepdims=True)
    i2 = jnp.min(jnp.where(rest & (pe == p2), elane, LANES), axis=-1, keepdims=True)
    denom = p1 + p2
    w1 = pg_top * p1 / denom
    w2 = pg_top * p2 / denom
    r_ref[...] = jnp.where(lane == 0, i1,
                           jnp.where(lane == 1, i2, jnp.where(lane == 2, w1, jnp.where(lane == 3, w2, 0.0))))


def moe_router(h, g, w_group, w_expert):
    t, d = h.shape
    n_groups = w_group.shape[1]
    n_exp = w_expert.shape[1]
    tm = _pick(t, 256)
    w_r = jnp.concatenate([w_group.astype(F32), w_expert.astype(F32),
                           jnp.zeros((d, LANES - n_groups - n_exp), F32)], axis=1)
    return pl.pallas_call(
        functools.partial(_router_kernel, n_groups=n_groups, per_group=n_exp // n_groups),
        grid=(t // tm,),
        in_specs=[pl.BlockSpec((tm, d), lambda i: (i, 0)),
                  pl.BlockSpec((1, d), lambda i: (0, 0)),
                  pl.BlockSpec((d, LANES), lambda i: (0, 0))],
        out_specs=[pl.BlockSpec((tm, d), lambda i: (i, 0)),
                   pl.BlockSpec((tm, LANES), lambda i: (i, 0))],
        out_shape=[jax.ShapeDtypeStruct((t, d), BF16),
                   jax.ShapeDtypeStruct((t, LANES), F32)],
        compiler_params=_params("parallel"),
        name="moe_router",
    )(h, g.reshape(1, d), w_r)


def _ffn_kernel(te_ref, nu_ref, x_ref, w1_ref, w3_ref, w2_ref, o_ref, w1b, w3b, w2b):
    i = pl.program_id(0)

    @pl.when(i < nu_ref[0])
    def _():
        prev = te_ref[jnp.maximum(i - 1, 0)]

        @pl.when((i == 0) | (te_ref[i] != prev))
        def _():
            w1b[...] = w1_ref[...].astype(BF16)
            w3b[...] = w3_ref[...].astype(BF16)
            w2b[...] = w2_ref[...].astype(BF16)

        x = x_ref[...]
        h1 = jnp.dot(x, w1b[...], preferred_element_type=F32)
        h3 = jnp.dot(x, w3b[...], preferred_element_type=F32)
        hdn = (_silu(h1) * h3).astype(BF16)
        o_ref[...] = jnp.dot(hdn, w2b[...], preferred_element_type=F32).astype(o_ref.dtype)


def moe_ffn(x_sorted, tile_expert, n_used, w1, w3, w2):
    r, d = x_sorted.shape
    ff = w1.shape[2]
    n_tiles = r // MOE_TILE
    grid_spec = pltpu.PrefetchScalarGridSpec(
        num_scalar_prefetch=2,
        grid=(n_tiles,),
        in_specs=[pl.BlockSpec((MOE_TILE, d), lambda i, te, nu: (i, 0)),
                  pl.BlockSpec((None, d, ff), lambda i, te, nu: (te[i], 0, 0)),
                  pl.BlockSpec((None, d, ff), lambda i, te, nu: (te[i], 0, 0)),
                  pl.BlockSpec((None, ff, d), lambda i, te, nu: (te[i], 0, 0))],
        out_specs=pl.BlockSpec((MOE_TILE, d), lambda i, te, nu: (i, 0)),
        scratch_shapes=[pltpu.VMEM((d, ff), BF16), pltpu.VMEM((d, ff), BF16), pltpu.VMEM((ff, d), BF16)],
    )
    return pl.pallas_call(
        _ffn_kernel,
        grid_spec=grid_spec,
        out_shape=jax.ShapeDtypeStruct((r, d), F32),
        compiler_params=_params("arbitrary"),
        name="moe_ffn",
    )(tile_expert, n_used, x_sorted, w1, w3, w2)


def _combine_kernel(h_ref, a_ref, b_ref, r_ref, o_ref):
    r = r_ref[...]
    o_ref[...] = h_ref[...] + r[:, 2:3] * a_ref[...] + r[:, 3:4] * b_ref[...]


def moe_combine(h, rows_a, rows_b, route):
    t, d = h.shape
    tm = _pick(t, 512)
    spec = pl.BlockSpec((tm, d), lambda i: (i, 0))
    return pl.pallas_call(
        _combine_kernel,
        grid=(t // tm,),
        in_specs=[spec, spec, spec, pl.BlockSpec((tm, LANES), lambda i: (i, 0))],
        out_specs=spec,
        out_shape=jax.ShapeDtypeStruct((t, d), F32),
        compiler_params=_params("parallel"),
        name="moe_combine",
    )(h, rows_a, rows_b, route)


def hier_moe_layer(h, g, w_group, w_expert, w1, w3, w2):
    t, d = h.shape
    n_exp = w1.shape[0]
    xn, route = moe_router(h, g, w_group, w_expert)
    e_flat = jnp.concatenate([route[:, 0], route[:, 1]]).astype(jnp.int32)
    onehot = (e_flat[:, None] == jnp.arange(n_exp, dtype=jnp.int32)[None, :]).astype(jnp.int32)
    counts = jnp.sum(onehot, axis=0)
    rank = jnp.sum((jnp.cumsum(onehot, axis=0) - onehot) * onehot, axis=1)
    padded = ((counts + MOE_TILE - 1) // MOE_TILE) * MOE_TILE
    ends = jnp.cumsum(padded)
    dest = (ends - padded)[e_flat] + rank
    n_rows = MOE_TOP_K * t + n_exp * MOE_TILE
    n_rows = ((n_rows + MOE_TILE - 1) // MOE_TILE) * MOE_TILE
    token = jnp.concatenate([jnp.arange(t, dtype=jnp.int32)] * MOE_TOP_K)
    row_token = jnp.zeros((n_rows,), jnp.int32).at[dest].set(token)
    tile_start = jnp.arange(n_rows // MOE_TILE, dtype=jnp.int32) * MOE_TILE
    tile_expert = jnp.minimum(jnp.searchsorted(ends, tile_start, side="right"), n_exp - 1).astype(jnp.int32)
    n_used = (ends[-1] // MOE_TILE).astype(jnp.int32).reshape(1)
    x_sorted = jnp.take(xn, row_token, axis=0)
    y_sorted = moe_ffn(x_sorted, tile_expert, n_used, w1, w3, w2)
    rows_a = jnp.take(y_sorted, dest[:t], axis=0)
    rows_b = jnp.take(y_sorted, dest[t:], axis=0)
    return moe_combine(h, rows_a, rows_b, route)


def _rope_tables(positions, half):
    inv = ROPE_THETA ** (-np.arange(half, dtype=np.float64) / half)
    ang = positions.astype(np.float64)[:, None] * inv[None, :]
    cos, sin = np.cos(ang), np.sin(ang)
    return (jnp.asarray(np.concatenate([cos, cos], axis=1), F32),
            jnp.asarray(np.concatenate([-sin, sin], axis=1), F32))


def _rope(x, cos2, sin2):
    half = x.shape[-1] // 2
    return x * cos2 + jnp.concatenate([x[:, half:], x[:, :half]], axis=-1) * sin2


def _kv_kernel(x_ref, w_ref, g_ref, cos_ref, sin_ref, ckv_ref, kpe_ref, wb_ref, *, rank):
    @pl.when(pl.program_id(0) == 0)
    def _():
        wb_ref[...] = w_ref[...].astype(BF16)

    c = jnp.dot(x_ref[...], wb_ref[...], preferred_element_type=F32)
    lat = c[:, :rank]
    ckv_ref[...] = lat * lax.rsqrt(jnp.mean(lat * lat, axis=-1, keepdims=True) + NORM_EPS) * g_ref[...]
    kpe_ref[...] = _rope(c[:, rank:], cos_ref[...], sin_ref[...])


def mla_shared_kv(xkv, w_dkv, latent_g, cos2, sin2, rank):
    t, d = xkv.shape
    rope_dim = w_dkv.shape[1] - rank
    tm = _pick(t, 512)
    return pl.pallas_call(
        functools.partial(_kv_kernel, rank=rank),
        grid=(t // tm,),
        in_specs=[pl.BlockSpec((tm, d), lambda i: (i, 0)),
                  pl.BlockSpec((d, rank + rope_dim), lambda i: (0, 0)),
                  pl.BlockSpec((1, rank), lambda i: (0, 0)),
                  pl.BlockSpec((tm, rope_dim), lambda i: (i, 0)),
                  pl.BlockSpec((tm, rope_dim), lambda i: (i, 0))],
        out_specs=[pl.BlockSpec((tm, rank), lambda i: (i, 0)),
                   pl.BlockSpec((tm, rope_dim), lambda i: (i, 0))],
        out_shape=[jax.ShapeDtypeStruct((t, rank), F32),
                   jax.ShapeDtypeStruct((t, rope_dim), F32)],
        scratch_shapes=[pltpu.VMEM((d, rank + rope_dim), BF16)],
        compiler_params=_params("arbitrary"),
        name="mla_shared_kv",
    )(xkv, w_dkv, latent_g.reshape(1, rank), cos2, sin2)


def _q_kernel(x_ref, w_ref, cos_ref, sin_ref, o_ref, *, nope, scale):
    q = _dot(x_ref[...], w_ref[...])
    o_ref[:, :nope] = (q[:, :nope] * scale).astype(o_ref.dtype)
    o_ref[:, nope:] = (_rope(q[:, nope:], cos_ref[...], sin_ref[...]) * scale).astype(o_ref.dtype)


def mla_query(xn, w_q_heads, cos2, sin2, nope, scale):
    t, d = xn.shape
    n_heads, _, qk = w_q_heads.shape
    rope_dim = qk - nope
    tm = _pick(t, 1024)
    return pl.pallas_call(
        functools.partial(_q_kernel, nope=nope, scale=scale),
        grid=(t // tm, n_heads),
        in_specs=[pl.BlockSpec((tm, d), lambda i, h: (i, 0)),
                  pl.BlockSpec((None, d, qk), lambda i, h: (h, 0, 0)),
                  pl.BlockSpec((tm, rope_dim), lambda i, h: (i, 0)),
                  pl.BlockSpec((tm, rope_dim), lambda i, h: (i, 0))],
        out_specs=pl.BlockSpec((None, tm, qk), lambda i, h: (h, i, 0)),
        out_shape=jax.ShapeDtypeStruct((n_heads, t, qk), BF16),
        compiler_params=_params("parallel", "arbitrary"),
        name="mla_query",
    )(xn, w_q_heads, cos2, sin2)


def _kv_expand_kernel(ckv_ref, kpe_ref, wk_ref, wv_ref, k_ref, v_ref, *, nope):
    ckv = ckv_ref[...]
    k_ref[:, :nope] = _dot(ckv, wk_ref[...]).astype(k_ref.dtype)
    k_ref[:, nope:] = kpe_ref[...].astype(k_ref.dtype)
    v_ref[...] = _dot(ckv, wv_ref[...]).astype(v_ref.dtype)


def mla_expand_kv(ckv, kpe, w_uk_heads, w_uv_heads, tp):
    rank = ckv.shape[1]
    rope_dim = kpe.shape[1]
    n_heads, _, nope = w_uk_heads.shape
    dv = w_uv_heads.shape[2]
    tm = _pick(tp, 1024)
    return pl.pallas_call(
        functools.partial(_kv_expand_kernel, nope=nope),
        grid=(tp // tm, n_heads),
        in_specs=[pl.BlockSpec((tm, rank), lambda i, h: (i, 0)),
                  pl.BlockSpec((tm, rope_dim), lambda i, h: (i, 0)),
                  pl.BlockSpec((None, rank, nope), lambda i, h: (h, 0, 0)),
                  pl.BlockSpec((None, rank, dv), lambda i, h: (h, 0, 0))],
        out_specs=[pl.BlockSpec((None, tm, nope + rope_dim), lambda i, h: (h, i, 0)),
                   pl.BlockSpec((None, tm, dv), lambda i, h: (h, i, 0))],
        out_shape=[jax.ShapeDtypeStruct((n_heads, tp, nope + rope_dim), BF16),
                   jax.ShapeDtypeStruct((n_heads, tp, dv), BF16)],
        compiler_params=_params("parallel", "arbitrary"),
        name="mla_expand_kv",
    )(ckv, kpe, w_uk_heads, w_uv_heads)


def _flash_kernel(q_ref, k_ref, v_ref, o_ref, m_scr, l_scr, acc_scr, *, tq, tk):
    qi = pl.program_id(1)
    kj = pl.program_id(2)

    @pl.when(kj == 0)
    def _():
        m_scr[...] = jnp.full_like(m_scr, NEG_BIG)
        l_scr[...] = jnp.zeros_like(l_scr)
        acc_scr[...] = jnp.zeros_like(acc_scr)

    @pl.when(kj * tk <= qi * tq + (tq - 1))
    def _():
        s = lax.dot_general(q_ref[...], k_ref[...], (((1,), (1,)), ((), ())), preferred_element_type=F32)
        qpos = qi * tq + lax.broadcasted_iota(jnp.int32, s.shape, 0)
        kpos = kj * tk + lax.broadcasted_iota(jnp.int32, s.shape, 1)
        s = jnp.where(kpos <= qpos, s, NEG_BIG)
        m_prev = m_scr[...]
        m_new = jnp.maximum(m_prev, jnp.max(s, axis=-1, keepdims=True))
        alpha = jnp.exp(m_prev - m_new)
        p = jnp.exp(s - m_new)
        l_scr[...] = alpha * l_scr[...] + jnp.sum(p, axis=-1, keepdims=True)
        acc_scr[...] = alpha * acc_scr[...] + jnp.dot(p.astype(BF16), v_ref[...], preferred_element_type=F32)
        m_scr[...] = m_new

    @pl.when(kj == pl.num_programs(2) - 1)
    def _():
        o_ref[...] = (acc_scr[...] / l_scr[...]).astype(o_ref.dtype)


def mla_prompt_attention(q_heads, k_heads, v_heads, tp):
    n_heads, _, qk = q_heads.shape
    dv = v_heads.shape[2]
    tq = _pick(tp, 512)
    tk = tq
    last_k = lambda i: (i * tq + tq - 1) // tk
    return pl.pallas_call(
        functools.partial(_flash_kernel, tq=tq, tk=tk),
        grid=(n_heads, tp // tq, tp // tk),
        in_specs=[pl.BlockSpec((None, tq, qk), lambda h, i, j: (h, i, 0)),
                  pl.BlockSpec((None, tk, qk), lambda h, i, j: (h, jnp.minimum(j, last_k(i)), 0)),
                  pl.BlockSpec((None, tk, dv), lambda h, i, j: (h, jnp.minimum(j, last_k(i)), 0))],
        out_specs=pl.BlockSpec((tq, dv), lambda h, i, j: (i, h)),
        out_shape=jax.ShapeDtypeStruct((tp, n_heads * dv), BF16),
        scratch_shapes=[pltpu.VMEM((tq, 1), F32), pltpu.VMEM((tq, 1), F32), pltpu.VMEM((tq, dv), F32)],
        compiler_params=_params("parallel", "parallel", "arbitrary"),
        name="mla_prompt_attention",
    )(q_heads, k_heads, v_heads)


def _absorb_kernel(q_ref, wk_ref, o_ref, *, nope):
    q = q_ref[...]
    rank = wk_ref.shape[0]
    o_ref[:, :rank] = _dot_nt(q[:, :nope], wk_ref[...]).astype(o_ref.dtype)
    o_ref[:, rank:] = q[:, nope:]


def mla_absorb_query(q_heads, w_uk_heads, tp):
    n_heads, t, qk = q_heads.shape
    _, rank, nope = w_uk_heads.shape
    ts = t - tp
    tm = _pick(math.gcd(ts, tp), 1024)
    off = tp // tm
    return pl.pallas_call(
        functools.partial(_absorb_kernel, nope=nope),
        grid=(n_heads, ts // tm),
        in_specs=[pl.BlockSpec((None, tm, qk), lambda h, i: (h, off + i, 0)),
                  pl.BlockSpec((None, rank, nope), lambda h, i: (h, 0, 0))],
        out_specs=pl.BlockSpec((None, tm, rank + qk - nope), lambda h, i: (h, i, 0)),
        out_shape=jax.ShapeDtypeStruct((n_heads, ts, rank + qk - nope), BF16),
        compiler_params=_params("parallel", "parallel"),
        name="mla_absorb_query",
    )(q_heads, w_uk_heads)


def _paged_kernel(pt_ref, q_ref, cn_ref, kn_ref, *rest, pages_per_step, rank, dec_seq):
    ckv_refs = rest[:pages_per_step]
    kpe_refs = rest[pages_per_step:2 * pages_per_step]
    o_ref, m_scr, l_scr, acc_scr = rest[2 * pages_per_step:]
    j = pl.program_id(1)
    n_heads = q_ref.shape[0]
    q = q_ref[...].reshape(n_heads * dec_seq, q_ref.shape[2])
    q_lat, q_pe = q[:, :rank], q[:, rank:]

    def update(s, values):
        m_prev = m_scr[...]
        m_new = jnp.maximum(m_prev, jnp.max(s, axis=-1, keepdims=True))
        alpha = jnp.exp(m_prev - m_new)
        p = jnp.exp(s - m_new)
        l_scr[...] = alpha * l_scr[...] + jnp.sum(p, axis=-1, keepdims=True)
        acc_scr[...] = alpha * acc_scr[...] + jnp.dot(p.astype(BF16), values, preferred_element_type=F32)
        m_scr[...] = m_new

    @pl.when(j == 0)
    def _():
        pad = lambda a: jnp.concatenate([a, jnp.zeros((LANES - dec_seq, a.shape[1]), a.dtype)], axis=0)
        cn = pad(cn_ref[...]).astype(BF16)
        s = _dot_nt(q_lat, cn) + _dot_nt(q_pe, pad(kn_ref[...]))
        qtok = lax.broadcasted_iota(jnp.int32, s.shape, 0) & (dec_seq - 1)
        ktok = lax.broadcasted_iota(jnp.int32, s.shape, 1)
        s = jnp.where(ktok <= qtok, s, NEG_BIG)
        m0 = jnp.max(s, axis=-1, keepdims=True)
        p = jnp.exp(s - m0)
        m_scr[...] = m0
        l_scr[...] = jnp.sum(p, axis=-1, keepdims=True)
        acc_scr[...] = jnp.dot(p.astype(BF16), cn, preferred_element_type=F32)

    for pg in range(pages_per_step):
        ck = ckv_refs[pg][...].astype(BF16)
        s = _dot_nt(q_lat, ck) + _dot_nt(q_pe, kpe_refs[pg][...])
        update(s, ck)

    @pl.when(j == pl.num_programs(1) - 1)
    def _():
        o = acc_scr[...] / l_scr[...]
        o_ref[...] = o.reshape(o_ref.shape).astype(o_ref.dtype)


def mla_paged_attention(q_abs, ckv_new, kpe_new, cache_ckv, cache_kpe, page_table, dec_seq):
    n_heads, ts, key_dim = q_abs.shape
    n_seq, n_pages = page_table.shape
    _, page, rank = cache_ckv.shape
    rope_dim = cache_kpe.shape[2]
    pps = _pick(n_pages, 16) if n_pages % 8 == 0 else n_pages
    while n_pages % pps:
        pps -= 1
    rows = n_heads * dec_seq

    def page_map(pg):
        return lambda s, j, pt: (pt[s, j * pps + pg], 0, 0)

    in_specs = [pl.BlockSpec((n_heads, dec_seq, key_dim), lambda s, j, pt: (0, s, 0)),
                pl.BlockSpec((dec_seq, rank), lambda s, j, pt: (s, 0)),
                pl.BlockSpec((dec_seq, rope_dim), lambda s, j, pt: (s, 0))]
    in_specs += [pl.BlockSpec((None, page, rank), page_map(pg)) for pg in range(pps)]
    in_specs += [pl.BlockSpec((None, page, rope_dim), page_map(pg)) for pg in range(pps)]
    grid_spec = pltpu.PrefetchScalarGridSpec(
        num_scalar_prefetch=1,
        grid=(n_seq, n_pages // pps),
        in_specs=in_specs,
        out_specs=pl.BlockSpec((n_heads, dec_seq, rank), lambda s, j, pt: (0, s, 0)),
        scratch_shapes=[pltpu.VMEM((rows, 1), F32), pltpu.VMEM((rows, 1), F32), pltpu.VMEM((rows, rank), F32)],
    )
    return pl.pallas_call(
        functools.partial(_paged_kernel, pages_per_step=pps, rank=rank, dec_seq=dec_seq),
        grid_spec=grid_spec,
        out_shape=jax.ShapeDtypeStruct((n_heads, ts, rank), BF16),
        compiler_params=_params("parallel", "arbitrary"),
        name="mla_paged_attention",
    )(page_table, q_abs, ckv_new, kpe_new, *([cache_ckv] * pps), *([cache_kpe] * pps))


def _unabsorb_kernel(o_ref, wv_ref, out_ref):
    out_ref[...] = _dot(o_ref[...], wv_ref[...]).astype(out_ref.dtype)


def mla_unabsorb(o_lat, w_uv_heads):
    n_heads, ts, rank = o_lat.shape
    dv = w_uv_heads.shape[2]
    tm = _pick(ts, 1024)
    return pl.pallas_call(
        _unabsorb_kernel,
        grid=(n_heads, ts // tm),
        in_specs=[pl.BlockSpec((None, tm, rank), lambda h, i: (h, i, 0)),
                  pl.BlockSpec((None, rank, dv), lambda h, i: (h, 0, 0))],
        out_specs=pl.BlockSpec((tm, dv), lambda h, i: (i, h)),
        out_shape=jax.ShapeDtypeStruct((ts, n_heads * dv), BF16),
        compiler_params=_params("parallel", "parallel"),
        name="mla_unabsorb",
    )(o_lat, w_uv_heads)


def kernel(x_prompt, x_sample, state_ssm, state_conv, cache_ckv, cache_kpe, page_table, norm_mix, norm_ffn,
           norm_final, gdn_w_in, gdn_conv_w, gdn_a_log, gdn_dt_bias, gdn_norm, gdn_w_out, kv_norm, w_dkv,
           kv_latent_norm, w_uk, w_uv, mla_w_q, mla_w_o, moe_w_group, moe_w_expert, moe_w1, moe_w3, moe_w2):
    batch, seq, d = x_prompt.shape
    n_seq, dec_seq, _ = x_sample.shape
    assert batch == 1 and dec_seq == SUBLANES
    tp, ts = seq, n_seq * dec_seq
    depth = norm_mix.shape[0]
    n_a = state_ssm.shape[0]
    n_vh, dk, dv = state_ssm.shape[2:]
    conv_dim = state_conv.shape[3]
    qk_dim = (conv_dim - n_vh * dv) // 2
    n_kh = qk_dim // dk
    n_taps = gdn_conv_w.shape[1]
    rank = kv_latent_norm.shape[0]
    rope_dim = w_dkv.shape[1] - rank
    n_heads, nope = w_uk.shape[1:]
    past_len = page_table.shape[1] * cache_ckv.shape[1]
    scale = float((nope + rope_dim) ** -0.5)

    h = jnp.concatenate([x_prompt.reshape(tp, d), x_sample.reshape(ts, d)], axis=0)
    positions = np.concatenate([np.arange(tp), past_len + np.arange(dec_seq)[None, :].repeat(n_seq, 0).reshape(-1)])
    cos2, sin2 = _rope_tables(positions, rope_dim // 2)

    ssm_p, conv_p, ssm_s, conv_s = [], [], [], []
    ckv = kpe = None
    for i in range(depth):
        if i < n_a:
            xn = rmsnorm(h, norm_mix[i], BF16)
            w_in = gdn_w_in[i]
            main_cols = conv_dim + n_vh * dv
            proj = matmul(xn, w_in, F32, n_cols=main_cols)
            gates = gdn_gates(xn, w_in[:, main_cols:], gdn_a_log[i], gdn_dt_bias[i])
            hist = jnp.pad(state_conv[i], ((0, 0), (SUBLANES - (n_taps - 1), 0), (0, 0)))
            qkv = gdn_conv(proj, hist, gdn_conv_w[i], tp, qk_dim, conv_dim, dk)
            gates_t = gates.T
            o_p, s_p = gdn_prompt(qkv, proj, gates_t, gdn_norm[i], tp, n_kh, n_vh, dk, dv, GDN_CHUNK)
            o_s, s_s = gdn_sample(qkv, proj, gates_t, gdn_norm[i], state_ssm[i], tp, n_kh, n_vh, dk, dv, dec_seq)
            h = matmul(jnp.concatenate([o_p, o_s], axis=0), gdn_w_out[i], F32, res=h, tn_pref=512)
            ssm_p.append(s_p[None])
            ssm_s.append(s_s)
            conv_p.append(proj[tp - (n_taps - 1):tp, :conv_dim][None])
            conv_s.append(proj[tp:, :conv_dim].reshape(n_seq, dec_seq, conv_dim)[:, dec_seq - (n_taps - 1):])
        else:
            if i == n_a:
                ckv, kpe = mla_shared_kv(rmsnorm(h, kv_norm, BF16), w_dkv, kv_latent_norm, cos2, sin2, rank)
                w_uk_heads = jnp.transpose(w_uk, (1, 0, 2))
                w_uv_heads = jnp.transpose(w_uv, (1, 0, 2))
                k_heads, v_heads = mla_expand_kv(ckv, kpe, w_uk_heads, w_uv_heads, tp)
            j = i - n_a
            xn = rmsnorm(h, norm_mix[i], BF16)
            w_q_heads = jnp.transpose(mla_w_q[j].reshape(d, n_heads, nope + rope_dim), (1, 0, 2))
            q_heads = mla_query(xn, w_q_heads, cos2, sin2, nope, scale)
            o_p = mla_prompt_attention(q_heads, k_heads, v_heads, tp)
            q_abs = mla_absorb_query(q_heads, w_uk_heads, tp)
            o_lat = mla_paged_attention(q_abs, ckv[tp:], kpe[tp:], cache_ckv, cache_kpe, page_table, dec_seq)
            o_s = mla_unabsorb(o_lat, w_uv_heads)
            h = matmul(jnp.concatenate([o_p, o_s], axis=0), mla_w_o[j], F32, res=h)
        h = hier_moe_layer(h, norm_ffn[i], moe_w_group[i], moe_w_expert[i], moe_w1[i], moe_w3[i], moe_w2[i])

    y_prompt = rmsnorm(h, norm_final, F32, 0, tp).reshape(batch, seq, d)
    y_sample = rmsnorm(h, norm_final, F32, tp, ts).reshape(n_seq, dec_seq, d)
    return (y_prompt, y_sample,
            jnp.stack(ssm_p), jnp.stack(conv_p),
            ckv[:tp].reshape(batch, seq, rank), kpe[:tp].reshape(batch, seq, rope_dim),
            jnp.stack(ssm_s), jnp.stack(conv_s),
            ckv[tp:].reshape(n_seq, dec_seq, rank), kpe[tp:].reshape(n_seq, dec_seq, rope_dim))
```

```python
import functools
import math

import numpy as np
import jax
import jax.numpy as jnp
from jax import lax
from jax.experimental import pallas as pl
from jax.experimental.pallas import tpu as pltpu

F32 = jnp.float32
BF16 = jnp.bfloat16

NORM_EPS = 1e-6
ROPE_THETA = 10000.0
GDN_CHUNK = 64
MOE_TOP_K = 2
SUBLANES = 8
LANES = 128
VMEM_LIMIT_BYTES = 56 * 1024 * 1024
GDN_ROWS = 128
GDN_PROMPT_HEADS = 8
GDN_SAMPLE_HEADS = 4
MOE_TILE = 256
ATTN_TILE = 1024
PAGES_PER_STEP = 32
PAGED_SUB_BLOCKS = 4
NEG_BIG = -1e30
LOG2E = math.log2(math.e)


def _params(*sem):
    return pltpu.CompilerParams(dimension_semantics=sem, vmem_limit_bytes=VMEM_LIMIT_BYTES)


def _pick(n, pref):
    t = min(pref, n)
    while n % t:
        t -= SUBLANES
    return t


def _divisor(n, pref):
    t = min(pref, n)
    while n % t:
        t -= 1
    return t


def _dot(a, b):
    return jnp.dot(a.astype(BF16), b.astype(BF16), preferred_element_type=F32)


def _dot_nt(a, b):
    return lax.dot_general(a.astype(BF16), b.astype(BF16), (((1,), (1,)), ((), ())),
                           preferred_element_type=F32)


def _split(a):
    hi = a.astype(BF16)
    lo = (a - hi.astype(F32)).astype(BF16)
    return hi, lo


def _dot3(a, b):
    ah, al = _split(a)
    bh, bl = _split(b)
    d = lambda x, y: jnp.dot(x, y, preferred_element_type=F32)
    return d(ah, bh) + (d(ah, bl) + d(al, bh))


def _sigmoid(x):
    return 1.0 / (1.0 + jnp.exp(-x))


def _silu(x):
    return x * _sigmoid(x)


def _rmsnorm_kernel(x_ref, g_ref, o_ref):
    x = x_ref[...]
    y = x * lax.rsqrt(jnp.mean(x * x, axis=-1, keepdims=True) + NORM_EPS) * g_ref[...]
    o_ref[...] = y.astype(o_ref.dtype)


def rmsnorm(x, g, out_dtype, row_start=0, rows=None):
    t, d = x.shape
    rows = t if rows is None else rows
    tm = _pick(math.gcd(rows, row_start) if row_start else rows, 512)
    off = row_start // tm
    return pl.pallas_call(
        _rmsnorm_kernel,
        grid=(rows // tm,),
        in_specs=[pl.BlockSpec((tm, d), lambda i: (i + off, 0)),
                  pl.BlockSpec((1, d), lambda i: (0, 0))],
        out_specs=pl.BlockSpec((tm, d), lambda i: (i, 0)),
        out_shape=jax.ShapeDtypeStruct((rows, d), out_dtype),
        compiler_params=_params("parallel"),
        name="rmsnorm",
    )(x, g.reshape(1, d))


def _matmul_kernel(*refs, has_res):
    if has_res:
        x_ref, w_ref, r_ref, o_ref, wb_ref = refs
    else:
        x_ref, w_ref, o_ref, wb_ref = refs

    @pl.when(pl.program_id(1) == 0)
    def _():
        wb_ref[...] = w_ref[...].astype(BF16)

    acc = jnp.dot(x_ref[...], wb_ref[...], preferred_element_type=F32)
    if has_res:
        acc = acc + r_ref[...]
    o_ref[...] = acc.astype(o_ref.dtype)


def matmul(x, w, out_dtype, n_cols=None, res=None, tm_pref=1024, tn_pref=1024):
    m, k = x.shape
    n = w.shape[1] if n_cols is None else n_cols
    tm = _pick(m, tm_pref)
    tn = _pick(n, tn_pref)
    in_specs = [pl.BlockSpec((tm, k), lambda j, i: (i, 0)),
                pl.BlockSpec((k, tn), lambda j, i: (0, j))]
    args = [x, w]
    if res is not None:
        in_specs.append(pl.BlockSpec((tm, tn), lambda j, i: (i, j)))
        args.append(res)
    return pl.pallas_call(
        functools.partial(_matmul_kernel, has_res=res is not None),
        grid=(n // tn, m // tm),
        in_specs=in_specs,
        out_specs=pl.BlockSpec((tm, tn), lambda j, i: (i, j)),
        out_shape=jax.ShapeDtypeStruct((m, n), out_dtype),
        scratch_shapes=[pltpu.VMEM((k, tn), BF16)],
        compiler_params=_params("parallel", "arbitrary"),
        name="matmul",
    )(*args)


def _gates_kernel(x_ref, w_ref, a_ref, dt_ref, o_ref, *, n_heads):
    ba = _dot(x_ref[...], w_ref[...])
    lane = lax.broadcasted_iota(jnp.int32, ba.shape, 1)
    beta = _sigmoid(ba)
    xs = ba + dt_ref[...]
    softplus = jnp.maximum(xs, 0.0) + jnp.log(1.0 + jnp.exp(-jnp.abs(xs)))
    g = -jnp.exp(a_ref[...]) * softplus
    o_ref[...] = jnp.where(lane < n_heads, beta, g)


def gdn_gates(xn, w_ba, a_log, dt_bias):
    t, d = xn.shape
    hv = a_log.shape[0]
    tm = _pick(t, 512)
    zeros = jnp.zeros((hv,), F32)
    a_pad = jnp.concatenate([zeros, a_log.astype(F32)]).reshape(1, 2 * hv)
    dt_pad = jnp.concatenate([zeros, dt_bias.astype(F32)]).reshape(1, 2 * hv)
    return pl.pallas_call(
        functools.partial(_gates_kernel, n_heads=hv),
        grid=(t // tm,),
        in_specs=[pl.BlockSpec((tm, d), lambda i: (i, 0)),
                  pl.BlockSpec((d, 2 * hv), lambda i: (0, 0)),
                  pl.BlockSpec((1, 2 * hv), lambda i: (0, 0)),
                  pl.BlockSpec((1, 2 * hv), lambda i: (0, 0))],
        out_specs=pl.BlockSpec((tm, 2 * hv), lambda i: (i, 0)),
        out_shape=jax.ShapeDtypeStruct((t, 2 * hv), F32),
        compiler_params=_params("parallel"),
        name="gdn_gates",
    )(xn, w_ba, a_pad, dt_pad)


def _conv_kernel(x_ref, prev_ref, st_ref, w_ref, o_ref, *, n_prompt_blocks, n_q_blocks, n_qk_blocks,
                 n_taps, head_dim):
    i = pl.program_id(0)
    j = pl.program_id(1)
    x3 = x_ref[...]
    prev = jnp.where(i == 0, 0.0, prev_ref[...])
    hist_prompt = jnp.concatenate([prev, x3[:-1]], axis=0)
    hist = jnp.where(i >= n_prompt_blocks, st_ref[...], hist_prompt)
    row = lax.broadcasted_iota(jnp.int32, x3.shape, 1)
    tap = lambda s: w_ref[n_taps - 1 - s:n_taps - s, :][None]
    acc = x3 * tap(0)
    for s in range(1, n_taps):
        shifted = jnp.where(row >= s, pltpu.roll(x3, s, 1), pltpu.roll(hist, s, 1))
        acc = acc + shifted * tap(s)
    act = _silu(acc).reshape(x3.shape[0] * SUBLANES, x3.shape[2])
    scale = jnp.where(j < n_q_blocks, head_dim ** -0.5, 1.0)
    is_qk = j < n_qk_blocks
    for c in range(act.shape[1] // head_dim):
        seg = act[:, c * head_dim:(c + 1) * head_dim]
        nrm = seg * (lax.rsqrt(jnp.sum(seg * seg, axis=-1, keepdims=True) + NORM_EPS) * scale)
        o_ref[:, c * head_dim:(c + 1) * head_dim] = jnp.where(is_qk, nrm, seg)


def gdn_conv(proj, state_hist, conv_w, tp, qk_dim, conv_dim, head_dim):
    t = proj.shape[0]
    n_taps = conv_w.shape[0]
    tm = _pick(math.gcd(tp, t - tp), 256)
    tc = _pick(qk_dim, 1024)
    tiles = tm // SUBLANES
    proj3 = proj.reshape(t // SUBLANES, SUBLANES, proj.shape[1])
    npb = tp // tm
    return pl.pallas_call(
        functools.partial(_conv_kernel, n_prompt_blocks=npb, n_q_blocks=qk_dim // tc,
                          n_qk_blocks=2 * qk_dim // tc, n_taps=n_taps, head_dim=head_dim),
        grid=(t // tm, conv_dim // tc),
        in_specs=[pl.BlockSpec((tiles, SUBLANES, tc), lambda i, j: (i, 0, j)),
                  pl.BlockSpec((1, SUBLANES, tc), lambda i, j: (jnp.maximum(i * tiles - 1, 0), 0, j)),
                  pl.BlockSpec((tiles, SUBLANES, tc), lambda i, j: (jnp.maximum(i - npb, 0), 0, j)),
                  pl.BlockSpec((n_taps, tc), lambda i, j: (0, j))],
        out_specs=pl.BlockSpec((tm, tc), lambda i, j: (i, j)),
        out_shape=jax.ShapeDtypeStruct((t, conv_dim), F32),
        compiler_params=_params("parallel", "parallel"),
        name="gdn_conv",
    )(proj3, proj3, state_hist, conv_w)


def _gdn_blocks(qs, ks, qk_of, vs, zs, g_rows, b_rows, norm_g, states, chunk):
    r = GDN_ROWS
    n_groups = r // chunk
    units = range(len(vs))
    ri = lax.broadcasted_iota(jnp.int32, (r, r), 0)
    ci = lax.broadcasted_iota(jnp.int32, (r, r), 1)
    shift = int(math.log2(chunk))
    rg = ri >> shift
    cg = ci >> shift
    same = rg == cg
    causal = same & (ri >= ci)
    strict = same & (ri > ci)
    eye = ri == ci
    last = ci == (rg << shift) + (chunk - 1)
    eye_f = jnp.where(eye, 1.0, 0.0)
    bcast = lambda a: jnp.broadcast_to(a, (r, r))

    gc_col = [jnp.sum(jnp.where(causal, bcast(g_rows[u]), 0.0), axis=1, keepdims=True) for u in units]
    gc_rb = [bcast(jnp.sum(jnp.where(eye, bcast(gc_col[u]), 0.0), axis=0, keepdims=True)) for u in units]
    b_col = [jnp.sum(jnp.where(eye, bcast(b_rows[u]), 0.0), axis=1, keepdims=True) for u in units]
    glast_col = [jnp.sum(jnp.where(last, gc_rb[u], 0.0), axis=1, keepdims=True) for u in units]
    decay = [jnp.where(causal, jnp.exp(jnp.where(causal, gc_col[u] - gc_rb[u], 0.0)), 0.0) for u in units]

    kk = [_dot_nt(k, k) for k in ks]
    qk = [_dot_nt(q, k) for q, k in zip(qs, ks)]
    lmat = [jnp.where(strict, b_col[u] * kk[qk_of[u]] * decay[u], 0.0) for u in units]
    tmat = [eye_f - lmat[u] for u in units]
    pw = lmat
    for _ in range(shift - 1):
        pw = [_dot(pw[u], pw[u]) for u in units]
        tmat = [tmat[u] + _dot(tmat[u], pw[u]) for u in units]

    eg_col = [jnp.exp(gc_col[u]) for u in units]
    uu = [_dot(tmat[u], vs[u] * b_col[u]) for u in units]
    ww = [_dot(tmat[u], ks[qk_of[u]] * (b_col[u] * eg_col[u])) for u in units]
    qg = [qs[qk_of[u]] * eg_col[u] for u in units]
    amat = [jnp.where(causal, qk[qk_of[u]] * decay[u], 0.0) for u in units]

    ws_qs = [[_dot(jnp.concatenate([ww[u][gi * chunk:(gi + 1) * chunk], qg[u][gi * chunk:(gi + 1) * chunk]], axis=0),
                   states[u][gi]) for gi in range(n_groups)] for u in units]
    vn = [uu[u] - jnp.concatenate([ws_qs[u][gi][:chunk] for gi in range(n_groups)], axis=0) for u in units]
    o = [jnp.concatenate([ws_qs[u][gi][chunk:] for gi in range(n_groups)], axis=0) + _dot(amat[u], vn[u])
         for u in units]

    kd_t = [(ks[qk_of[u]] * jnp.exp(glast_col[u] - gc_col[u])).T for u in units]
    new_states = [[states[u][gi] * jnp.exp(glast_col[u][gi * chunk:gi * chunk + 1, :])
                   + _dot(jnp.where(cg == gi, kd_t[u], 0.0), vn[u]) for gi in range(n_groups)] for u in units]

    outs = [o[u] * lax.rsqrt(jnp.mean(o[u] * o[u], axis=-1, keepdims=True) + NORM_EPS) * norm_g * _silu(zs[u])
            for u in units]
    return outs, new_states


def _gdn_prompt_kernel(q_ref, k_ref, v_ref, z_ref, g_ref, b_ref, ng_ref, o_ref, sfin_ref, s_scr, *,
                       heads_per_step, chunk, dv):
    c = pl.program_id(1)
    rep = GDN_ROWS // chunk

    @pl.when(c == 0)
    def _():
        s_scr[...] = jnp.zeros_like(s_scr)

    heads = range(heads_per_step)
    tile = lambda ref, hh: jnp.concatenate([ref[:, hh * LANES:(hh + 1) * LANES]] * rep, axis=0)
    stack = lambda ref, hh: jnp.concatenate(
        [ref[:, (hh * rep + e) * dv:(hh * rep + e + 1) * dv] for e in range(rep)], axis=0)
    outs, new_states = _gdn_blocks(
        [tile(q_ref, hh) for hh in heads], [tile(k_ref, hh) for hh in heads], list(heads),
        [stack(v_ref, hh) for hh in heads], [stack(z_ref, hh) for hh in heads],
        [g_ref[hh] for hh in heads], [b_ref[hh] for hh in heads], ng_ref[...],
        [[s_scr[hh * rep + e] for e in range(rep)] for hh in heads], chunk)
    for hh in heads:
        for e in range(rep):
            s_scr[hh * rep + e] = new_states[hh][e]
            o_ref[:, (hh * rep + e) * dv:(hh * rep + e + 1) * dv] = (
                outs[hh][e * chunk:(e + 1) * chunk].astype(o_ref.dtype))

    @pl.when(c == pl.num_programs(1) - 1)
    def _():
        sfin_ref[...] = s_scr[...]


def gdn_prompt(qkv, proj, gates_t, norm_g, tp, n_kh, n_vh, dk, dv, chunk):
    rep = n_vh // n_kh
    assert rep * chunk == GDN_ROWS and dk == LANES and dv == LANES
    hps = _divisor(n_kh, GDN_PROMPT_HEADS)
    n_chunks = tp // chunk
    qk_blocks = n_kh // hps
    conv_dim = 2 * n_kh * dk + n_vh * dv

    def arrange(gt):
        a = gt[:, :tp].reshape(qk_blocks, hps, rep, n_chunks, chunk)
        return a.transpose(0, 3, 1, 2, 4).reshape(qk_blocks, n_chunks, hps, 1, GDN_ROWS)

    beta_a = arrange(gates_t[:n_vh])
    g_a = arrange(gates_t[n_vh:])
    vw = hps * rep * dv
    v_off = 2 * n_kh * dk // vw
    z_off = conv_dim // vw
    gate_spec = pl.BlockSpec((None, None, hps, 1, GDN_ROWS), lambda h, c: (h, c, 0, 0, 0))
    return pl.pallas_call(
        functools.partial(_gdn_prompt_kernel, heads_per_step=hps, chunk=chunk, dv=dv),
        grid=(qk_blocks, n_chunks),
        in_specs=[pl.BlockSpec((chunk, hps * dk), lambda h, c: (c, h)),
                  pl.BlockSpec((chunk, hps * dk), lambda h, c: (c, qk_blocks + h)),
                  pl.BlockSpec((chunk, vw), lambda h, c: (c, v_off + h)),
                  pl.BlockSpec((chunk, vw), lambda h, c: (c, z_off + h)),
                  gate_spec, gate_spec,
                  pl.BlockSpec((1, dv), lambda h, c: (0, 0))],
        out_specs=[pl.BlockSpec((chunk, vw), lambda h, c: (c, h)),
                   pl.BlockSpec((hps * rep, dk, dv), lambda h, c: (h, 0, 0))],
        out_shape=[jax.ShapeDtypeStruct((tp, n_vh * dv), BF16),
                   jax.ShapeDtypeStruct((n_vh, dk, dv), F32)],
        scratch_shapes=[pltpu.VMEM((hps * rep, dk, dv), F32)],
        compiler_params=_params("parallel", "arbitrary"),
        name="gdn_prompt",
    )(qkv, qkv, qkv, proj, g_a, beta_a, norm_g.reshape(1, dv))


def _gdn_sample_kernel(q_ref, k_ref, v_ref, z_ref, g_ref, b_ref, ng_ref, s_ref, o_ref, snew_ref, *,
                       heads_per_step, rep, chunk, dv):
    n_groups = GDN_ROWS // chunk
    n_qk = heads_per_step // rep
    heads = range(heads_per_step)
    cols = lambda ref, n, w: [ref[:, i * w:(i + 1) * w] for i in range(n)]
    outs, new_states = _gdn_blocks(
        cols(q_ref, n_qk, LANES), cols(k_ref, n_qk, LANES), [hh // rep for hh in heads],
        cols(v_ref, heads_per_step, dv), cols(z_ref, heads_per_step, dv),
        [g_ref[hh] for hh in heads], [b_ref[hh] for hh in heads], ng_ref[...],
        [[s_ref[gi, hh] for gi in range(n_groups)] for hh in heads], chunk)
    for hh in heads:
        o_ref[:, hh * dv:(hh + 1) * dv] = outs[hh].astype(o_ref.dtype)
        for gi in range(n_groups):
            snew_ref[gi, hh] = new_states[hh][gi]


def gdn_sample(qkv, proj, gates_t, norm_g, state, tp, n_kh, n_vh, dk, dv, chunk):
    rep = n_vh // n_kh
    t = qkv.shape[0]
    ts = t - tp
    assert dk == LANES and dv == LANES and ts % GDN_ROWS == 0 and tp % GDN_ROWS == 0
    hps = rep * _divisor(n_kh, GDN_SAMPLE_HEADS // rep)
    n_blocks = ts // GDN_ROWS
    spb = GDN_ROWS // chunk
    row_off = tp // GDN_ROWS
    conv_dim = 2 * n_kh * dk + n_vh * dv
    n_qk = hps // rep

    def arrange(gt):
        return gt[:, tp:].reshape(n_vh // hps, hps, n_blocks, 1, GDN_ROWS).transpose(0, 2, 1, 3, 4)

    beta_a = arrange(gates_t[:n_vh])
    g_a = arrange(gates_t[n_vh:])
    gate_spec = pl.BlockSpec((None, None, hps, 1, GDN_ROWS), lambda b, h: (h, b, 0, 0, 0))
    state_spec = pl.BlockSpec((spb, hps, dk, dv), lambda b, h: (b, h, 0, 0))
    return pl.pallas_call(
        functools.partial(_gdn_sample_kernel, heads_per_step=hps, rep=rep, chunk=chunk, dv=dv),
        grid=(n_blocks, n_vh // hps),
        in_specs=[pl.BlockSpec((GDN_ROWS, n_qk * dk), lambda b, h: (row_off + b, h)),
                  pl.BlockSpec((GDN_ROWS, n_qk * dk), lambda b, h: (row_off + b, n_kh // n_qk + h)),
                  pl.BlockSpec((GDN_ROWS, hps * dv), lambda b, h: (row_off + b, 2 * n_kh * dk // (hps * dv) + h)),
                  pl.BlockSpec((GDN_ROWS, hps * dv), lambda b, h: (row_off + b, conv_dim // (hps * dv) + h)),
                  gate_spec, gate_spec,
                  pl.BlockSpec((1, dv), lambda b, h: (0, 0)),
                  state_spec],
        out_specs=[pl.BlockSpec((GDN_ROWS, hps * dv), lambda b, h: (b, h)),
                   state_spec],
        out_shape=[jax.ShapeDtypeStruct((ts, n_vh * dv), BF16),
                   jax.ShapeDtypeStruct(state.shape, F32)],
        compiler_params=_params("parallel", "parallel"),
        name="gdn_sample",
    )(qkv, qkv, qkv, proj, g_a, beta_a, norm_g.reshape(1, dv), state)


def _router_kernel(x_ref, g_ref, w_ref, xn_ref, r_ref, *, n_groups, per_group):
    x = x_ref[...]
    xn = x * lax.rsqrt(jnp.mean(x * x, axis=-1, keepdims=True) + NORM_EPS) * g_ref[...]
    xn_ref[...] = xn.astype(xn_ref.dtype)
    logits = _dot3(xn, w_ref[...])
    lane = lax.broadcasted_iota(jnp.int32, logits.shape, 1).astype(F32)
    is_grp = lane < n_groups
    m = jnp.max(jnp.where(is_grp, logits, NEG_BIG), axis=-1, keepdims=True)
    e = jnp.where(is_grp, jnp.exp(jnp.where(is_grp, logits - m, 0.0)), 0.0)
    pg = e / jnp.sum(e, axis=-1, keepdims=True)
    pg_top = jnp.max(pg, axis=-1, keepdims=True)
    g_idx = jnp.min(jnp.where(is_grp & (pg == pg_top), lane, LANES), axis=-1, keepdims=True)
    elane = lane - n_groups
    in_grp = (elane >= g_idx * per_group) & (elane < (g_idx + 1.0) * per_group)
    m2 = jnp.max(jnp.where(in_grp, logits, NEG_BIG), axis=-1, keepdims=True)
    e2 = jnp.where(in_grp, jnp.exp(jnp.where(in_grp, logits - m2, 0.0)), 0.0)
    pe = e2 / jnp.sum(e2, axis=-1, keepdims=True)
    p1 = jnp.max(jnp.where(in_grp, pe, -1.0), axis=-1, keepdims=True)
    i1 = jnp.min(jnp.where(in_grp & (pe == p1), elane, LANES), axis=-1, keepdims=True)
    rest = in_grp & (elane != i1)
    p2 = jnp.max(jnp.where(rest, pe, -1.0), axis=-1, keepdims=True)
    i2 = jnp.min(jnp.where(rest & (pe == p2), elane, LANES), axis=-1, keepdims=True)
    denom = p1 + p2
    w1 = pg_top * p1 / denom
    w2 = pg_top * p2 / denom
    r_ref[...] = jnp.where(lane == 0, i1,
                           jnp.where(lane == 1, i2, jnp.where(lane == 2, w1, jnp.where(lane == 3, w2, 0.0))))


def moe_router(h, g, w_group, w_expert):
    t, d = h.shape
    n_groups = w_group.shape[1]
    n_exp = w_expert.shape[1]
    tm = _pick(t, 256)
    w_r = jnp.concatenate([w_group.astype(F32), w_expert.astype(F32),
                           jnp.zeros((d, LANES - n_groups - n_exp), F32)], axis=1)
    return pl.pallas_call(
        functools.partial(_router_kernel, n_groups=n_groups, per_group=n_exp // n_groups),
        grid=(t // tm,),
        in_specs=[pl.BlockSpec((tm, d), lambda i: (i, 0)),
                  pl.BlockSpec((1, d), lambda i: (0, 0)),
                  pl.BlockSpec((d, LANES), lambda i: (0, 0))],
        out_specs=[pl.BlockSpec((tm, d), lambda i: (i, 0)),
                   pl.BlockSpec((tm, LANES), lambda i: (i, 0))],
        out_shape=[jax.ShapeDtypeStruct((t, d), BF16),
                   jax.ShapeDtypeStruct((t, LANES), F32)],
        compiler_params=_params("parallel"),
        name="moe_router",
    )(h, g.reshape(1, d), w_r)


def _ffn_kernel(te_ref, nu_ref, x_ref, w1_ref, w3_ref, w2_ref, o_ref, w1b, w3b, w2b):
    i = pl.program_id(0)

    @pl.when(i < nu_ref[0])
    def _():
        prev = te_ref[jnp.maximum(i - 1, 0)]

        @pl.when((i == 0) | (te_ref[i] != prev))
        def _():
            w1b[...] = w1_ref[...].astype(BF16)
            w3b[...] = w3_ref[...].astype(BF16)
            w2b[...] = w2_ref[...].astype(BF16)

        x = x_ref[...]
        h1 = jnp.dot(x, w1b[...], preferred_element_type=F32)
        h3 = jnp.dot(x, w3b[...], preferred_element_type=F32)
        hdn = (_silu(h1) * h3).astype(BF16)
        o_ref[...] = jnp.dot(hdn, w2b[...], preferred_element_type=F32).astype(o_ref.dtype)


def moe_ffn(x_sorted, tile_expert, n_used, w1, w3, w2, layer):
    r, d = x_sorted.shape
    ff = w1.shape[3]
    n_tiles = r // MOE_TILE
    grid_spec = pltpu.PrefetchScalarGridSpec(
        num_scalar_prefetch=2,
        grid=(n_tiles,),
        in_specs=[pl.BlockSpec((MOE_TILE, d), lambda i, te, nu: (i, 0)),
                  pl.BlockSpec((None, None, d, ff), lambda i, te, nu: (layer, te[i], 0, 0)),
                  pl.BlockSpec((None, None, d, ff), lambda i, te, nu: (layer, te[i], 0, 0)),
                  pl.BlockSpec((None, None, ff, d), lambda i, te, nu: (layer, te[i], 0, 0))],
        out_specs=pl.BlockSpec((MOE_TILE, d), lambda i, te, nu: (i, 0)),
        scratch_shapes=[pltpu.VMEM((d, ff), BF16), pltpu.VMEM((d, ff), BF16), pltpu.VMEM((ff, d), BF16)],
    )
    return pl.pallas_call(
        _ffn_kernel,
        grid_spec=grid_spec,
        out_shape=jax.ShapeDtypeStruct((r, d), F32),
        compiler_params=_params("arbitrary"),
        name="moe_ffn",
    )(tile_expert, n_used, x_sorted, w1, w3, w2)


def _combine_kernel(h_ref, a_ref, b_ref, r_ref, o_ref):
    r = r_ref[...]
    o_ref[...] = h_ref[...] + r[:, 2:3] * a_ref[...] + r[:, 3:4] * b_ref[...]


def moe_combine(h, rows_a, rows_b, route):
    t, d = h.shape
    tm = _pick(t, 512)
    spec = pl.BlockSpec((tm, d), lambda i: (i, 0))
    return pl.pallas_call(
        _combine_kernel,
        grid=(t // tm,),
        in_specs=[spec, spec, spec, pl.BlockSpec((tm, LANES), lambda i: (i, 0))],
        out_specs=spec,
        out_shape=jax.ShapeDtypeStruct((t, d), F32),
        compiler_params=_params("parallel"),
        name="moe_combine",
    )(h, rows_a, rows_b, route)


def hier_moe_layer(h, g, w_group, w_expert, w1, w3, w2, layer):
    t, d = h.shape
    n_exp = w1.shape[1]
    xn, route = moe_router(h, g, w_group, w_expert)
    e_flat = jnp.concatenate([route[:, 0], route[:, 1]]).astype(jnp.int32)
    onehot = (e_flat[:, None] == jnp.arange(n_exp, dtype=jnp.int32)[None, :]).astype(jnp.int32)
    counts = jnp.sum(onehot, axis=0)
    rank = jnp.sum((jnp.cumsum(onehot, axis=0) - onehot) * onehot, axis=1)
    padded = ((counts + MOE_TILE - 1) // MOE_TILE) * MOE_TILE
    ends = jnp.cumsum(padded)
    dest = jnp.sum(onehot * (ends - padded)[None, :], axis=1) + rank
    n_rows = MOE_TOP_K * t + n_exp * MOE_TILE
    n_rows = ((n_rows + MOE_TILE - 1) // MOE_TILE) * MOE_TILE
    token = jnp.concatenate([jnp.arange(t, dtype=jnp.int32)] * MOE_TOP_K)
    row_token = jnp.zeros((n_rows,), jnp.int32).at[dest].set(token)
    tile_start = jnp.arange(n_rows // MOE_TILE, dtype=jnp.int32) * MOE_TILE
    tile_expert = jnp.minimum(jnp.sum((ends[None, :] <= tile_start[:, None]).astype(jnp.int32), axis=1), n_exp - 1)
    n_used = (ends[-1] // MOE_TILE).astype(jnp.int32).reshape(1)
    x_sorted = jnp.take(xn, row_token, axis=0, mode="clip")
    y_sorted = moe_ffn(x_sorted, tile_expert, n_used, w1, w3, w2, layer)
    rows_a = jnp.take(y_sorted, dest[:t], axis=0, mode="clip")
    rows_b = jnp.take(y_sorted, dest[t:], axis=0, mode="clip")
    return moe_combine(h, rows_a, rows_b, route)


def _rope_tables(positions, half):
    inv = ROPE_THETA ** (-np.arange(half, dtype=np.float64) / half)
    ang = positions.astype(np.float64)[:, None] * inv[None, :]
    cos, sin = np.cos(ang), np.sin(ang)
    return (jnp.asarray(np.concatenate([cos, cos], axis=1), F32),
            jnp.asarray(np.concatenate([-sin, sin], axis=1), F32))


def _rope(x, cos2, sin2):
    half = x.shape[-1] // 2
    return x * cos2 + jnp.concatenate([x[:, half:], x[:, :half]], axis=-1) * sin2


def _kv_kernel(x_ref, w_ref, g_ref, cos_ref, sin_ref, ckv_ref, kpe_ref, wb_ref, *, rank):
    @pl.when(pl.program_id(0) == 0)
    def _():
        wb_ref[...] = w_ref[...].astype(BF16)

    c = jnp.dot(x_ref[...], wb_ref[...], preferred_element_type=F32)
    lat = c[:, :rank]
    ckv_ref[...] = lat * lax.rsqrt(jnp.mean(lat * lat, axis=-1, keepdims=True) + NORM_EPS) * g_ref[...]
    kpe_ref[...] = _rope(c[:, rank:], cos_ref[...], sin_ref[...])


def mla_shared_kv(xkv, w_dkv, latent_g, cos2, sin2, rank):
    t, d = xkv.shape
    rope_dim = w_dkv.shape[1] - rank
    tm = _pick(t, 512)
    return pl.pallas_call(
        functools.partial(_kv_kernel, rank=rank),
        grid=(t // tm,),
        in_specs=[pl.BlockSpec((tm, d), lambda i: (i, 0)),
                  pl.BlockSpec((d, rank + rope_dim), lambda i: (0, 0)),
                  pl.BlockSpec((1, rank), lambda i: (0, 0)),
                  pl.BlockSpec((tm, rope_dim), lambda i: (i, 0)),
                  pl.BlockSpec((tm, rope_dim), lambda i: (i, 0))],
        out_specs=[pl.BlockSpec((tm, rank), lambda i: (i, 0)),
                   pl.BlockSpec((tm, rope_dim), lambda i: (i, 0))],
        out_shape=[jax.ShapeDtypeStruct((t, rank), F32),
                   jax.ShapeDtypeStruct((t, rope_dim), F32)],
        scratch_shapes=[pltpu.VMEM((d, rank + rope_dim), BF16)],
        compiler_params=_params("arbitrary"),
        name="mla_shared_kv",
    )(xkv, w_dkv, latent_g.reshape(1, rank), cos2, sin2)


def _q_kernel(x_ref, w_ref, cos_ref, sin_ref, o_ref, *, nope, scale, n_prompt_blocks):
    sc = jnp.where(pl.program_id(0) < n_prompt_blocks, scale * LOG2E, scale)
    q = _dot(x_ref[...], w_ref[...])
    o_ref[:, :nope] = (q[:, :nope] * sc).astype(o_ref.dtype)
    o_ref[:, nope:] = (_rope(q[:, nope:], cos_ref[...], sin_ref[...]) * sc).astype(o_ref.dtype)


def mla_query(xn, w_q_heads, cos2, sin2, nope, scale, tp):
    t, d = xn.shape
    n_heads, _, qk = w_q_heads.shape
    rope_dim = qk - nope
    tm = _pick(math.gcd(tp, t - tp), 1024)
    return pl.pallas_call(
        functools.partial(_q_kernel, nope=nope, scale=scale, n_prompt_blocks=tp // tm),
        grid=(t // tm, n_heads),
        in_specs=[pl.BlockSpec((tm, d), lambda i, h: (i, 0)),
                  pl.BlockSpec((None, d, qk), lambda i, h: (h, 0, 0)),
                  pl.BlockSpec((tm, rope_dim), lambda i, h: (i, 0)),
                  pl.BlockSpec((tm, rope_dim), lambda i, h: (i, 0))],
        out_specs=pl.BlockSpec((None, tm, qk), lambda i, h: (h, i, 0)),
        out_shape=jax.ShapeDtypeStruct((n_heads, t, qk), BF16),
        compiler_params=_params("parallel", "arbitrary"),
        name="mla_query",
    )(xn, w_q_heads, cos2, sin2)


def _kv_expand_kernel(ckv_ref, kpe_ref, wk_ref, wv_ref, k_ref, v_ref, *, nope):
    ckv = ckv_ref[...]
    k_ref[:, :nope] = _dot(ckv, wk_ref[...]).astype(k_ref.dtype)
    k_ref[:, nope:] = kpe_ref[...].astype(k_ref.dtype)
    v_ref[...] = _dot(ckv, wv_ref[...]).astype(v_ref.dtype)


def mla_expand_kv(ckv, kpe, w_uk_heads, w_uv_heads, tp):
    rank = ckv.shape[1]
    rope_dim = kpe.shape[1]
    n_heads, _, nope = w_uk_heads.shape
    dv = w_uv_heads.shape[2]
    tm = _pick(tp, 1024)
    return pl.pallas_call(
        functools.partial(_kv_expand_kernel, nope=nope),
        grid=(tp // tm, n_heads),
        in_specs=[pl.BlockSpec((tm, rank), lambda i, h: (i, 0)),
                  pl.BlockSpec((tm, rope_dim), lambda i, h: (i, 0)),
                  pl.BlockSpec((None, rank, nope), lambda i, h: (h, 0, 0)),
                  pl.BlockSpec((None, rank, dv), lambda i, h: (h, 0, 0))],
        out_specs=[pl.BlockSpec((None, tm, nope + rope_dim), lambda i, h: (h, i, 0)),
                   pl.BlockSpec((None, tm, dv), lambda i, h: (h, i, 0))],
        out_shape=[jax.ShapeDtypeStruct((n_heads, tp, nope + rope_dim), BF16),
                   jax.ShapeDtypeStruct((n_heads, tp, dv), BF16)],
        compiler_params=_params("parallel", "arbitrary"),
        name="mla_expand_kv",
    )(ckv, kpe, w_uk_heads, w_uv_heads)


def _flash_kernel(qi_ref, kj_ref, q_ref, k_ref, v_ref, o_ref, m_scr, l_scr, acc_scr, *, tile):
    step = pl.program_id(1)
    qi = qi_ref[step]
    kj = kj_ref[step]

    @pl.when(kj == 0)
    def _():
        m_scr[...] = jnp.full_like(m_scr, NEG_BIG)
        l_scr[...] = jnp.zeros_like(l_scr)
        acc_scr[...] = jnp.zeros_like(acc_scr)

    s = lax.dot_general(q_ref[...], k_ref[...], (((1,), (1,)), ((), ())), preferred_element_type=F32)

    def update(s):
        m_prev = m_scr[...]
        m_new = jnp.maximum(m_prev, jnp.max(s, axis=-1, keepdims=True))
        alpha = jnp.exp2(m_prev - m_new)
        p = jnp.exp2(s - m_new)
        l_scr[...] = alpha * l_scr[...] + jnp.sum(p, axis=-1, keepdims=True)
        acc_scr[...] = alpha * acc_scr[...] + jnp.dot(p.astype(BF16), v_ref[...], preferred_element_type=F32)
        m_scr[...] = m_new

    @pl.when(kj < qi)
    def _():
        update(s)

    @pl.when(kj == qi)
    def _():
        row = lax.broadcasted_iota(jnp.int32, s.shape, 0)
        col = lax.broadcasted_iota(jnp.int32, s.shape, 1)
        update(jnp.where(col <= row, s, NEG_BIG))
        o_ref[...] = (acc_scr[...] / l_scr[...]).astype(o_ref.dtype)


def mla_prompt_attention(q_heads, k_heads, v_heads, tp):
    n_heads, _, qk = q_heads.shape
    dv = v_heads.shape[2]
    tile = _pick(tp, ATTN_TILE)
    n_blocks = tp // tile
    pairs = [(i, j) for i in range(n_blocks) for j in range(i + 1)]
    qi_list = jnp.asarray([p[0] for p in pairs], jnp.int32)
    kj_list = jnp.asarray([p[1] for p in pairs], jnp.int32)
    grid_spec = pltpu.PrefetchScalarGridSpec(
        num_scalar_prefetch=2,
        grid=(n_heads, len(pairs)),
        in_specs=[pl.BlockSpec((None, tile, qk), lambda h, s, qi, kj: (h, qi[s], 0)),
                  pl.BlockSpec((None, tile, qk), lambda h, s, qi, kj: (h, kj[s], 0)),
                  pl.BlockSpec((None, tile, dv), lambda h, s, qi, kj: (h, kj[s], 0))],
        out_specs=pl.BlockSpec((tile, dv), lambda h, s, qi, kj: (qi[s], h)),
        scratch_shapes=[pltpu.VMEM((tile, 1), F32), pltpu.VMEM((tile, 1), F32), pltpu.VMEM((tile, dv), F32)],
    )
    return pl.pallas_call(
        functools.partial(_flash_kernel, tile=tile),
        grid_spec=grid_spec,
        out_shape=jax.ShapeDtypeStruct((tp, n_heads * dv), BF16),
        compiler_params=_params("parallel", "arbitrary"),
        name="mla_prompt_attention",
    )(qi_list, kj_list, q_heads, k_heads, v_heads)


def _absorb_kernel(q_ref, wk_ref, o_ref, *, nope):
    q = q_ref[...]
    rank = wk_ref.shape[0]
    o_ref[:, :rank] = _dot_nt(q[:, :nope], wk_ref[...]).astype(o_ref.dtype)
    o_ref[:, rank:] = q[:, nope:]


def mla_absorb_query(q_heads, w_uk_heads, tp):
    n_heads, t, qk = q_heads.shape
    _, rank, nope = w_uk_heads.shape
    ts = t - tp
    tm = _pick(math.gcd(ts, tp), 1024)
    off = tp // tm
    return pl.pallas_call(
        functools.partial(_absorb_kernel, nope=nope),
        grid=(n_heads, ts // tm),
        in_specs=[pl.BlockSpec((None, tm, qk), lambda h, i: (h, off + i, 0)),
                  pl.BlockSpec((None, rank, nope), lambda h, i: (h, 0, 0))],
        out_specs=pl.BlockSpec((None, tm, rank + qk - nope), lambda h, i: (h, i, 0)),
        out_shape=jax.ShapeDtypeStruct((n_heads, ts, rank + qk - nope), BF16),
        compiler_params=_params("parallel", "parallel"),
        name="mla_absorb_query",
    )(q_heads, w_uk_heads)


def _paged_kernel(pt_ref, q_ref, cn_ref, kn_ref, *rest, pages_per_step, rank, dec_seq, page, n_sub):
    ckv_refs = rest[:pages_per_step]
    kpe_refs = rest[pages_per_step:2 * pages_per_step]
    o_ref, m_scr, l_scr, acc_scr, ck_scr, kp_scr = rest[2 * pages_per_step:]
    j = pl.program_id(1)
    n_heads = q_ref.shape[0]
    q = q_ref[...].reshape(n_heads * dec_seq, q_ref.shape[2])
    q_lat, q_pe = q[:, :rank], q[:, rank:]

    @pl.when(j == 0)
    def _():
        pad = lambda a: jnp.concatenate([a, jnp.zeros((LANES - dec_seq, a.shape[1]), a.dtype)], axis=0)
        cn = pad(cn_ref[...]).astype(BF16)
        s = _dot_nt(q_lat, cn) + _dot_nt(q_pe, pad(kn_ref[...]))
        qtok = lax.broadcasted_iota(jnp.int32, s.shape, 0) & (dec_seq - 1)
        ktok = lax.broadcasted_iota(jnp.int32, s.shape, 1)
        s = jnp.where(ktok <= qtok, s, NEG_BIG)
        m0 = jnp.max(s, axis=-1, keepdims=True)
        p = jnp.exp(s - m0)
        m_scr[...] = m0
        l_scr[...] = jnp.sum(p, axis=-1, keepdims=True)
        acc_scr[...] = jnp.dot(p.astype(BF16), cn, preferred_element_type=F32)

    for pg in range(pages_per_step):
        ck_scr[pg * page:(pg + 1) * page, :] = ckv_refs[pg][...].astype(BF16)
        kp_scr[:, pg * page:(pg + 1) * page] = kpe_refs[pg][...].astype(BF16)

    sub = pages_per_step * page // n_sub
    cks = [ck_scr[i * sub:(i + 1) * sub, :] for i in range(n_sub)]
    ss = [lax.dot_general(q_lat, cks[i], (((1,), (1,)), ((), ())), preferred_element_type=F32)
          + jnp.dot(q_pe, kp_scr[:, i * sub:(i + 1) * sub], preferred_element_type=F32) for i in range(n_sub)]
    ms = [jnp.max(s, axis=-1, keepdims=True) for s in ss]
    ps = [jnp.exp(s - m) for s, m in zip(ss, ms)]
    ls = [jnp.sum(p, axis=-1, keepdims=True) for p in ps]
    accs = [jnp.dot(p.astype(BF16), ck, preferred_element_type=F32) for p, ck in zip(ps, cks)]
    m_prev = m_scr[...]
    m_new = functools.reduce(jnp.maximum, ms, m_prev)
    alpha = jnp.exp(m_prev - m_new)
    l_new = alpha * l_scr[...]
    acc_new = alpha * acc_scr[...]
    for m, l, acc in zip(ms, ls, accs):
        coef = jnp.exp(m - m_new)
        l_new = l_new + coef * l
        acc_new = acc_new + coef * acc
    m_scr[...] = m_new
    l_scr[...] = l_new
    acc_scr[...] = acc_new

    @pl.when(j == pl.num_programs(1) - 1)
    def _():
        o_ref[...] = (acc_new / l_new).reshape(o_ref.shape).astype(o_ref.dtype)


def mla_paged_attention(q_abs, ckv_new, kpe_new, cache_ckv, cache_kpe_t, page_table, dec_seq):
    n_heads, ts, key_dim = q_abs.shape
    n_seq, n_pages = page_table.shape
    _, page, rank = cache_ckv.shape
    rope_dim = cache_kpe_t.shape[1]
    pps = _divisor(n_pages, PAGES_PER_STEP)
    rows = n_heads * dec_seq

    def page_map(pg):
        return lambda s, j, pt: (pt[s, j * pps + pg], 0, 0)

    in_specs = [pl.BlockSpec((n_heads, dec_seq, key_dim), lambda s, j, pt: (0, s, 0)),
                pl.BlockSpec((dec_seq, rank), lambda s, j, pt: (s, 0)),
                pl.BlockSpec((dec_seq, rope_dim), lambda s, j, pt: (s, 0))]
    in_specs += [pl.BlockSpec((None, page, rank), page_map(pg)) for pg in range(pps)]
    in_specs += [pl.BlockSpec((None, rope_dim, page), page_map(pg)) for pg in range(pps)]
    grid_spec = pltpu.PrefetchScalarGridSpec(
        num_scalar_prefetch=1,
        grid=(n_seq, n_pages // pps),
        in_specs=in_specs,
        out_specs=pl.BlockSpec((n_heads, dec_seq, rank), lambda s, j, pt: (0, s, 0)),
        scratch_shapes=[pltpu.VMEM((rows, 1), F32), pltpu.VMEM((rows, 1), F32), pltpu.VMEM((rows, rank), F32),
                        pltpu.VMEM((pps * page, rank), BF16), pltpu.VMEM((rope_dim, pps * page), BF16)],
    )
    return pl.pallas_call(
        functools.partial(_paged_kernel, pages_per_step=pps, rank=rank, dec_seq=dec_seq, page=page,
                          n_sub=_divisor(pps, PAGED_SUB_BLOCKS)),
        grid_spec=grid_spec,
        out_shape=jax.ShapeDtypeStruct((n_heads, ts, rank), BF16),
        compiler_params=_params("parallel", "arbitrary"),
        name="mla_paged_attention",
    )(page_table, q_abs, ckv_new, kpe_new, *([cache_ckv] * pps), *([cache_kpe_t] * pps))


def _unabsorb_kernel(o_ref, wv_ref, out_ref):
    out_ref[...] = _dot(o_ref[...], wv_ref[...]).astype(out_ref.dtype)


def mla_unabsorb(o_lat, w_uv_heads):
    n_heads, ts, rank = o_lat.shape
    dv = w_uv_heads.shape[2]
    tm = _pick(ts, 1024)
    return pl.pallas_call(
        _unabsorb_kernel,
        grid=(n_heads, ts // tm),
        in_specs=[pl.BlockSpec((None, tm, rank), lambda h, i: (h, i, 0)),
                  pl.BlockSpec((None, rank, dv), lambda h, i: (h, 0, 0))],
        out_specs=pl.BlockSpec((tm, dv), lambda h, i: (i, h)),
        out_shape=jax.ShapeDtypeStruct((ts, n_heads * dv), BF16),
        compiler_params=_params("parallel", "parallel"),
        name="mla_unabsorb",
    )(o_lat, w_uv_heads)


def kernel(x_prompt, x_sample, state_ssm, state_conv, cache_ckv, cache_kpe, page_table, norm_mix, norm_ffn,
           norm_final, gdn_w_in, gdn_conv_w, gdn_a_log, gdn_dt_bias, gdn_norm, gdn_w_out, kv_norm, w_dkv,
           kv_latent_norm, w_uk, w_uv, mla_w_q, mla_w_o, moe_w_group, moe_w_expert, moe_w1, moe_w3, moe_w2):
    batch, seq, d = x_prompt.shape
    n_seq, dec_seq, _ = x_sample.shape
    assert batch == 1 and dec_seq == SUBLANES
    tp, ts = seq, n_seq * dec_seq
    depth = norm_mix.shape[0]
    n_a = state_ssm.shape[0]
    n_vh, dk, dv = state_ssm.shape[2:]
    conv_dim = state_conv.shape[3]
    qk_dim = (conv_dim - n_vh * dv) // 2
    n_kh = qk_dim // dk
    n_taps = gdn_conv_w.shape[1]
    rank = kv_latent_norm.shape[0]
    rope_dim = w_dkv.shape[1] - rank
    n_heads, nope = w_uk.shape[1:]
    past_len = page_table.shape[1] * cache_ckv.shape[1]
    scale = float((nope + rope_dim) ** -0.5)

    h = jnp.concatenate([x_prompt.reshape(tp, d), x_sample.reshape(ts, d)], axis=0)
    positions = np.concatenate([np.arange(tp), past_len + np.arange(dec_seq)[None, :].repeat(n_seq, 0).reshape(-1)])
    cos2, sin2 = _rope_tables(positions, rope_dim // 2)

    ssm_p, conv_p, ssm_s, conv_s = [], [], [], []
    ckv = kpe = None
    for i in range(depth):
        if i < n_a:
            xn = rmsnorm(h, norm_mix[i], BF16)
            w_in = gdn_w_in[i]
            main_cols = conv_dim + n_vh * dv
            proj = matmul(xn, w_in, F32, n_cols=main_cols)
            gates = gdn_gates(xn, w_in[:, main_cols:], gdn_a_log[i], gdn_dt_bias[i])
            hist = jnp.pad(state_conv[i], ((0, 0), (SUBLANES - (n_taps - 1), 0), (0, 0)))
            qkv = gdn_conv(proj, hist, gdn_conv_w[i], tp, qk_dim, conv_dim, dk)
            gates_t = gates.T
            o_p, s_p = gdn_prompt(qkv, proj, gates_t, gdn_norm[i], tp, n_kh, n_vh, dk, dv, GDN_CHUNK)
            o_s, s_s = gdn_sample(qkv, proj, gates_t, gdn_norm[i], state_ssm[i], tp, n_kh, n_vh, dk, dv, dec_seq)
            h = matmul(jnp.concatenate([o_p, o_s], axis=0), gdn_w_out[i], F32, res=h, tn_pref=512)
            ssm_p.append(s_p[None])
            ssm_s.append(s_s)
            conv_p.append(proj[tp - (n_taps - 1):tp, :conv_dim][None])
            conv_s.append(proj[tp:, :conv_dim].reshape(n_seq, dec_seq, conv_dim)[:, dec_seq - (n_taps - 1):])
        else:
            if i == n_a:
                ckv, kpe = mla_shared_kv(rmsnorm(h, kv_norm, BF16), w_dkv, kv_latent_norm, cos2, sin2, rank)
                w_uk_heads = jnp.transpose(w_uk, (1, 0, 2))
                w_uv_heads = jnp.transpose(w_uv, (1, 0, 2))
                k_heads, v_heads = mla_expand_kv(ckv, kpe, w_uk_heads, w_uv_heads, tp)
                cache_kpe_t = jnp.swapaxes(cache_kpe, 1, 2)
            j = i - n_a
            xn = rmsnorm(h, norm_mix[i], BF16)
            w_q_heads = jnp.transpose(mla_w_q[j].reshape(d, n_heads, nope + rope_dim), (1, 0, 2))
            q_heads = mla_query(xn, w_q_heads, cos2, sin2, nope, scale, tp)
            o_p = mla_prompt_attention(q_heads, k_heads, v_heads, tp)
            q_abs = mla_absorb_query(q_heads, w_uk_heads, tp)
            o_lat = mla_paged_attention(q_abs, ckv[tp:], kpe[tp:], cache_ckv, cache_kpe_t, page_table, dec_seq)
            o_s = mla_unabsorb(o_lat, w_uv_heads)
            h = matmul(jnp.concatenate([o_p, o_s], axis=0), mla_w_o[j], F32, res=h)
        h = hier_moe_layer(h, norm_ffn[i], moe_w_group[i], moe_w_expert[i], moe_w1, moe_w3, moe_w2, i)

    y_prompt = rmsnorm(h, norm_final, F32, 0, tp).reshape(batch, seq, d)
    y_sample = rmsnorm(h, norm_final, F32, tp, ts).reshape(n_seq, dec_seq, d)
    return (y_prompt, y_sample,
            jnp.stack(ssm_p), jnp.stack(conv_p),
            ckv[:tp].reshape(batch, seq, rank), kpe[:tp].reshape(batch, seq, rope_dim),
            jnp.stack(ssm_s), jnp.stack(conv_s),
            ckv[tp:].reshape(n_seq, dec_seq, rank), kpe[tp:].reshape(n_seq, dec_seq, rope_dim))
```

```python
import functools
import math

import numpy as np
import jax
import jax.numpy as jnp
from jax import lax
from jax.experimental import pallas as pl
from jax.experimental.pallas import tpu as pltpu

F32 = jnp.float32
BF16 = jnp.bfloat16

NORM_EPS = 1e-6
ROPE_THETA = 10000.0
GDN_CHUNK = 64
MOE_TOP_K = 2
SUBLANES = 8
LANES = 128
VMEM_LIMIT_BYTES = 56 * 1024 * 1024
GDN_ROWS = 128
GDN_PROMPT_HEADS = 8
GDN_SAMPLE_HEADS = 4
MOE_TILE = 256
ATTN_TILE = 1024
PAGES_PER_STEP = 32
PAGED_SUB_BLOCKS = 4
NEG_BIG = -1e30
LOG2E = math.log2(math.e)


def _params(*sem):
    return pltpu.CompilerParams(dimension_semantics=sem, vmem_limit_bytes=VMEM_LIMIT_BYTES)


def _pick(n, pref):
    t = min(pref, n)
    while n % t:
        t -= SUBLANES
    return t


def _divisor(n, pref):
    t = min(pref, n)
    while n % t:
        t -= 1
    return t


def _dot(a, b):
    return jnp.dot(a.astype(BF16), b.astype(BF16), preferred_element_type=F32)


def _dot_nt(a, b):
    return lax.dot_general(a.astype(BF16), b.astype(BF16), (((1,), (1,)), ((), ())),
                           preferred_element_type=F32)


def _split(a):
    hi = a.astype(BF16)
    lo = (a - hi.astype(F32)).astype(BF16)
    return hi, lo


def _dot3(a, b):
    ah, al = _split(a)
    bh, bl = _split(b)
    d = lambda x, y: jnp.dot(x, y, preferred_element_type=F32)
    return d(ah, bh) + (d(ah, bl) + d(al, bh))


def _sigmoid(x):
    return 1.0 / (1.0 + jnp.exp(-x))


def _silu(x):
    return x * _sigmoid(x)


def _rmsnorm_kernel(x_ref, g_ref, o_ref):
    x = x_ref[...]
    y = x * lax.rsqrt(jnp.mean(x * x, axis=-1, keepdims=True) + NORM_EPS) * g_ref[...]
    o_ref[...] = y.astype(o_ref.dtype)


def rmsnorm(x, g, out_dtype, row_start=0, rows=None):
    t, d = x.shape
    rows = t if rows is None else rows
    tm = _pick(math.gcd(rows, row_start) if row_start else rows, 512)
    off = row_start // tm
    return pl.pallas_call(
        _rmsnorm_kernel,
        grid=(rows // tm,),
        in_specs=[pl.BlockSpec((tm, d), lambda i: (i + off, 0)),
                  pl.BlockSpec((1, d), lambda i: (0, 0))],
        out_specs=pl.BlockSpec((tm, d), lambda i: (i, 0)),
        out_shape=jax.ShapeDtypeStruct((rows, d), out_dtype),
        compiler_params=_params("parallel"),
        name="rmsnorm",
    )(x, g.reshape(1, d))


def _matmul_kernel(*refs, has_res):
    if has_res:
        x_ref, w_ref, r_ref, o_ref, wb_ref = refs
    else:
        x_ref, w_ref, o_ref, wb_ref = refs

    @pl.when(pl.program_id(1) == 0)
    def _():
        wb_ref[...] = w_ref[...].astype(BF16)

    acc = jnp.dot(x_ref[...], wb_ref[...], preferred_element_type=F32)
    if has_res:
        acc = acc + r_ref[...]
    o_ref[...] = acc.astype(o_ref.dtype)


def matmul(x, w, out_dtype, n_cols=None, res=None, tm_pref=1024, tn_pref=1024):
    m, k = x.shape
    n = w.shape[1] if n_cols is None else n_cols
    tm = _pick(m, tm_pref)
    tn = _pick(n, tn_pref)
    in_specs = [pl.BlockSpec((tm, k), lambda j, i: (i, 0)),
                pl.BlockSpec((k, tn), lambda j, i: (0, j))]
    args = [x, w]
    if res is not None:
        in_specs.append(pl.BlockSpec((tm, tn), lambda j, i: (i, j)))
        args.append(res)
    return pl.pallas_call(
        functools.partial(_matmul_kernel, has_res=res is not None),
        grid=(n // tn, m // tm),
        in_specs=in_specs,
        out_specs=pl.BlockSpec((tm, tn), lambda j, i: (i, j)),
        out_shape=jax.ShapeDtypeStruct((m, n), out_dtype),
        scratch_shapes=[pltpu.VMEM((k, tn), BF16)],
        compiler_params=_params("parallel", "arbitrary"),
        name="matmul",
    )(*args)


def _gates_kernel(x_ref, w_ref, a_ref, dt_ref, o_ref, *, n_heads):
    ba = _dot(x_ref[...], w_ref[...])
    lane = lax.broadcasted_iota(jnp.int32, ba.shape, 1)
    beta = _sigmoid(ba)
    xs = ba + dt_ref[...]
    softplus = jnp.maximum(xs, 0.0) + jnp.log(1.0 + jnp.exp(-jnp.abs(xs)))
    g = -jnp.exp(a_ref[...]) * softplus
    o_ref[...] = jnp.where(lane < n_heads, beta, g)


def gdn_gates(xn, w_ba, a_log, dt_bias):
    t, d = xn.shape
    hv = a_log.shape[0]
    tm = _pick(t, 512)
    zeros = jnp.zeros((hv,), F32)
    a_pad = jnp.concatenate([zeros, a_log.astype(F32)]).reshape(1, 2 * hv)
    dt_pad = jnp.concatenate([zeros, dt_bias.astype(F32)]).reshape(1, 2 * hv)
    return pl.pallas_call(
        functools.partial(_gates_kernel, n_heads=hv),
        grid=(t // tm,),
        in_specs=[pl.BlockSpec((tm, d), lambda i: (i, 0)),
                  pl.BlockSpec((d, 2 * hv), lambda i: (0, 0)),
                  pl.BlockSpec((1, 2 * hv), lambda i: (0, 0)),
                  pl.BlockSpec((1, 2 * hv), lambda i: (0, 0))],
        out_specs=pl.BlockSpec((tm, 2 * hv), lambda i: (i, 0)),
        out_shape=jax.ShapeDtypeStruct((t, 2 * hv), F32),
        compiler_params=_params("parallel"),
        name="gdn_gates",
    )(xn, w_ba, a_pad, dt_pad)


def _conv_kernel(x_ref, prev_ref, st_ref, w_ref, o_ref, *, n_prompt_blocks, n_q_blocks, n_qk_blocks,
                 n_taps, head_dim):
    i = pl.program_id(0)
    j = pl.program_id(1)
    x3 = x_ref[...]
    prev = jnp.where(i == 0, 0.0, prev_ref[...])
    hist_prompt = jnp.concatenate([prev, x3[:-1]], axis=0)
    hist = jnp.where(i >= n_prompt_blocks, st_ref[...], hist_prompt)
    row = lax.broadcasted_iota(jnp.int32, x3.shape, 1)
    tap = lambda s: w_ref[n_taps - 1 - s:n_taps - s, :][None]
    acc = x3 * tap(0)
    for s in range(1, n_taps):
        shifted = jnp.where(row >= s, pltpu.roll(x3, s, 1), pltpu.roll(hist, s, 1))
        acc = acc + shifted * tap(s)
    act = _silu(acc).reshape(x3.shape[0] * SUBLANES, x3.shape[2])
    scale = jnp.where(j < n_q_blocks, head_dim ** -0.5, 1.0)
    is_qk = j < n_qk_blocks
    for c in range(act.shape[1] // head_dim):
        seg = act[:, c * head_dim:(c + 1) * head_dim]
        nrm = seg * (lax.rsqrt(jnp.sum(seg * seg, axis=-1, keepdims=True) + NORM_EPS) * scale)
        o_ref[:, c * head_dim:(c + 1) * head_dim] = jnp.where(is_qk, nrm, seg)


def gdn_conv(proj, state_hist, conv_w, tp, qk_dim, conv_dim, head_dim):
    t = proj.shape[0]
    n_taps = conv_w.shape[0]
    tm = _pick(math.gcd(tp, t - tp), 512)
    tc = _pick(qk_dim, 1024)
    tiles = tm // SUBLANES
    proj3 = proj.reshape(t // SUBLANES, SUBLANES, proj.shape[1])
    npb = tp // tm
    return pl.pallas_call(
        functools.partial(_conv_kernel, n_prompt_blocks=npb, n_q_blocks=qk_dim // tc,
                          n_qk_blocks=2 * qk_dim // tc, n_taps=n_taps, head_dim=head_dim),
        grid=(t // tm, conv_dim // tc),
        in_specs=[pl.BlockSpec((tiles, SUBLANES, tc), lambda i, j: (i, 0, j)),
                  pl.BlockSpec((1, SUBLANES, tc), lambda i, j: (jnp.maximum(i * tiles - 1, 0), 0, j)),
                  pl.BlockSpec((tiles, SUBLANES, tc), lambda i, j: (jnp.maximum(i - npb, 0), 0, j)),
                  pl.BlockSpec((n_taps, tc), lambda i, j: (0, j))],
        out_specs=pl.BlockSpec((tm, tc), lambda i, j: (i, j)),
        out_shape=jax.ShapeDtypeStruct((t, conv_dim), F32),
        compiler_params=_params("parallel", "parallel"),
        name="gdn_conv",
    )(proj3, proj3, state_hist, conv_w)


def _gdn_blocks(qs, ks, qk_of, vs, zs, g_rows, b_rows, norm_g, states, chunk):
    r = GDN_ROWS
    n_groups = r // chunk
    units = range(len(vs))
    ri = lax.broadcasted_iota(jnp.int32, (r, r), 0)
    ci = lax.broadcasted_iota(jnp.int32, (r, r), 1)
    shift = int(math.log2(chunk))
    rg = ri >> shift
    cg = ci >> shift
    same = rg == cg
    causal = same & (ri >= ci)
    strict = same & (ri > ci)
    eye = ri == ci
    last = ci == (rg << shift) + (chunk - 1)
    eye_f = jnp.where(eye, 1.0, 0.0)
    bcast = lambda a: jnp.broadcast_to(a, (r, r))

    gc_col = [jnp.sum(jnp.where(causal, bcast(g_rows[u]), 0.0), axis=1, keepdims=True) for u in units]
    gc_rb = [bcast(jnp.sum(jnp.where(eye, bcast(gc_col[u]), 0.0), axis=0, keepdims=True)) for u in units]
    b_col = [jnp.sum(jnp.where(eye, bcast(b_rows[u]), 0.0), axis=1, keepdims=True) for u in units]
    glast_col = [jnp.sum(jnp.where(last, gc_rb[u], 0.0), axis=1, keepdims=True) for u in units]
    decay = [jnp.where(causal, jnp.exp(jnp.where(causal, gc_col[u] - gc_rb[u], 0.0)), 0.0) for u in units]

    kk = [_dot_nt(k, k) for k in ks]
    qk = [_dot_nt(q, k) for q, k in zip(qs, ks)]
    lmat = [jnp.where(strict, b_col[u] * kk[qk_of[u]] * decay[u], 0.0) for u in units]
    tmat = [eye_f - lmat[u] for u in units]
    pw = lmat
    for _ in range(shift - 1):
        pw = [_dot(pw[u], pw[u]) for u in units]
        tmat = [tmat[u] + _dot(tmat[u], pw[u]) for u in units]

    eg_col = [jnp.exp(gc_col[u]) for u in units]
    uu = [_dot(tmat[u], vs[u] * b_col[u]) for u in units]
    ww = [_dot(tmat[u], ks[qk_of[u]] * (b_col[u] * eg_col[u])) for u in units]
    qg = [qs[qk_of[u]] * eg_col[u] for u in units]
    amat = [jnp.where(causal, qk[qk_of[u]] * decay[u], 0.0) for u in units]

    ws_qs = [[_dot(jnp.concatenate([ww[u][gi * chunk:(gi + 1) * chunk], qg[u][gi * chunk:(gi + 1) * chunk]], axis=0),
                   states[u][gi]) for gi in range(n_groups)] for u in units]
    vn = [uu[u] - jnp.concatenate([ws_qs[u][gi][:chunk] for gi in range(n_groups)], axis=0) for u in units]
    o = [jnp.concatenate([ws_qs[u][gi][chunk:] for gi in range(n_groups)], axis=0) + _dot(amat[u], vn[u])
         for u in units]

    kd_t = [(ks[qk_of[u]] * jnp.exp(glast_col[u] - gc_col[u])).T for u in units]
    new_states = [[states[u][gi] * jnp.exp(glast_col[u][gi * chunk:gi * chunk + 1, :])
                   + _dot(jnp.where(cg == gi, kd_t[u], 0.0), vn[u]) for gi in range(n_groups)] for u in units]

    outs = [o[u] * lax.rsqrt(jnp.mean(o[u] * o[u], axis=-1, keepdims=True) + NORM_EPS) * norm_g * _silu(zs[u])
            for u in units]
    return outs, new_states


def _gdn_prompt_kernel(q_ref, k_ref, v_ref, z_ref, g_ref, b_ref, ng_ref, o_ref, sfin_ref, s_scr, *,
                       heads_per_step, chunk, dv):
    c = pl.program_id(1)
    rep = GDN_ROWS // chunk

    @pl.when(c == 0)
    def _():
        s_scr[...] = jnp.zeros_like(s_scr)

    heads = range(heads_per_step)
    tile = lambda ref, hh: jnp.concatenate([ref[:, hh * LANES:(hh + 1) * LANES]] * rep, axis=0)
    stack = lambda ref, hh: jnp.concatenate(
        [ref[:, (hh * rep + e) * dv:(hh * rep + e + 1) * dv] for e in range(rep)], axis=0)
    outs, new_states = _gdn_blocks(
        [tile(q_ref, hh) for hh in heads], [tile(k_ref, hh) for hh in heads], list(heads),
        [stack(v_ref, hh) for hh in heads], [stack(z_ref, hh) for hh in heads],
        [g_ref[hh] for hh in heads], [b_ref[hh] for hh in heads], ng_ref[...],
        [[s_scr[hh * rep + e] for e in range(rep)] for hh in heads], chunk)
    for hh in heads:
        for e in range(rep):
            s_scr[hh * rep + e] = new_states[hh][e]
            o_ref[:, (hh * rep + e) * dv:(hh * rep + e + 1) * dv] = (
                outs[hh][e * chunk:(e + 1) * chunk].astype(o_ref.dtype))

    @pl.when(c == pl.num_programs(1) - 1)
    def _():
        sfin_ref[...] = s_scr[...]


def gdn_prompt(qkv, proj, gates_t, norm_g, tp, n_kh, n_vh, dk, dv, chunk):
    rep = n_vh // n_kh
    assert rep * chunk == GDN_ROWS and dk == LANES and dv == LANES
    hps = _divisor(n_kh, GDN_PROMPT_HEADS)
    n_chunks = tp // chunk
    qk_blocks = n_kh // hps
    conv_dim = 2 * n_kh * dk + n_vh * dv

    def arrange(gt):
        a = gt[:, :tp].reshape(qk_blocks, hps, rep, n_chunks, chunk)
        return a.transpose(0, 3, 1, 2, 4).reshape(qk_blocks, n_chunks, hps, 1, GDN_ROWS)

    beta_a = arrange(gates_t[:n_vh])
    g_a = arrange(gates_t[n_vh:])
    vw = hps * rep * dv
    v_off = 2 * n_kh * dk // vw
    z_off = conv_dim // vw
    gate_spec = pl.BlockSpec((None, None, hps, 1, GDN_ROWS), lambda h, c: (h, c, 0, 0, 0))
    return pl.pallas_call(
        functools.partial(_gdn_prompt_kernel, heads_per_step=hps, chunk=chunk, dv=dv),
        grid=(qk_blocks, n_chunks),
        in_specs=[pl.BlockSpec((chunk, hps * dk), lambda h, c: (c, h)),
                  pl.BlockSpec((chunk, hps * dk), lambda h, c: (c, qk_blocks + h)),
                  pl.BlockSpec((chunk, vw), lambda h, c: (c, v_off + h)),
                  pl.BlockSpec((chunk, vw), lambda h, c: (c, z_off + h)),
                  gate_spec, gate_spec,
                  pl.BlockSpec((1, dv), lambda h, c: (0, 0))],
        out_specs=[pl.BlockSpec((chunk, vw), lambda h, c: (c, h)),
                   pl.BlockSpec((hps * rep, dk, dv), lambda h, c: (h, 0, 0))],
        out_shape=[jax.ShapeDtypeStruct((tp, n_vh * dv), BF16),
                   jax.ShapeDtypeStruct((n_vh, dk, dv), F32)],
        scratch_shapes=[pltpu.VMEM((hps * rep, dk, dv), F32)],
        compiler_params=_params("parallel", "arbitrary"),
        name="gdn_prompt",
    )(qkv, qkv, qkv, proj, g_a, beta_a, norm_g.reshape(1, dv))


def _gdn_sample_kernel(q_ref, k_ref, v_ref, z_ref, g_ref, b_ref, ng_ref, s_ref, o_ref, snew_ref, *,
                       heads_per_step, rep, chunk, dv):
    n_groups = GDN_ROWS // chunk
    n_qk = heads_per_step // rep
    heads = range(heads_per_step)
    cols = lambda ref, n, w: [ref[:, i * w:(i + 1) * w] for i in range(n)]
    outs, new_states = _gdn_blocks(
        cols(q_ref, n_qk, LANES), cols(k_ref, n_qk, LANES), [hh // rep for hh in heads],
        cols(v_ref, heads_per_step, dv), cols(z_ref, heads_per_step, dv),
        [g_ref[hh] for hh in heads], [b_ref[hh] for hh in heads], ng_ref[...],
        [[s_ref[gi, hh] for gi in range(n_groups)] for hh in heads], chunk)
    for hh in heads:
        o_ref[:, hh * dv:(hh + 1) * dv] = outs[hh].astype(o_ref.dtype)
        for gi in range(n_groups):
            snew_ref[gi, hh] = new_states[hh][gi]


def gdn_sample(qkv, proj, gates_t, norm_g, state, tp, n_kh, n_vh, dk, dv, chunk):
    rep = n_vh // n_kh
    t = qkv.shape[0]
    ts = t - tp
    assert dk == LANES and dv == LANES and ts % GDN_ROWS == 0 and tp % GDN_ROWS == 0
    hps = rep * _divisor(n_kh, GDN_SAMPLE_HEADS // rep)
    n_blocks = ts // GDN_ROWS
    spb = GDN_ROWS // chunk
    row_off = tp // GDN_ROWS
    conv_dim = 2 * n_kh * dk + n_vh * dv
    n_qk = hps // rep

    def arrange(gt):
        return gt[:, tp:].reshape(n_vh // hps, hps, n_blocks, 1, GDN_ROWS).transpose(0, 2, 1, 3, 4)

    beta_a = arrange(gates_t[:n_vh])
    g_a = arrange(gates_t[n_vh:])
    gate_spec = pl.BlockSpec((None, None, hps, 1, GDN_ROWS), lambda b, h: (h, b, 0, 0, 0))
    state_spec = pl.BlockSpec((spb, hps, dk, dv), lambda b, h: (b, h, 0, 0))
    return pl.pallas_call(
        functools.partial(_gdn_sample_kernel, heads_per_step=hps, rep=rep, chunk=chunk, dv=dv),
        grid=(n_blocks, n_vh // hps),
        in_specs=[pl.BlockSpec((GDN_ROWS, n_qk * dk), lambda b, h: (row_off + b, h)),
                  pl.BlockSpec((GDN_ROWS, n_qk * dk), lambda b, h: (row_off + b, n_kh // n_qk + h)),
                  pl.BlockSpec((GDN_ROWS, hps * dv), lambda b, h: (row_off + b, 2 * n_kh * dk // (hps * dv) + h)),
                  pl.BlockSpec((GDN_ROWS, hps * dv), lambda b, h: (row_off + b, conv_dim // (hps * dv) + h)),
                  gate_spec, gate_spec,
                  pl.BlockSpec((1, dv), lambda b, h: (0, 0)),
                  state_spec],
        out_specs=[pl.BlockSpec((GDN_ROWS, hps * dv), lambda b, h: (b, h)),
                   state_spec],
        out_shape=[jax.ShapeDtypeStruct((ts, n_vh * dv), BF16),
                   jax.ShapeDtypeStruct(state.shape, F32)],
        compiler_params=_params("parallel", "parallel"),
        name="gdn_sample",
    )(qkv, qkv, qkv, proj, g_a, beta_a, norm_g.reshape(1, dv), state)


def _router_kernel(x_ref, g_ref, w_ref, xn_ref, r_ref, *, n_groups, per_group):
    x = x_ref[...]
    xn = x * lax.rsqrt(jnp.mean(x * x, axis=-1, keepdims=True) + NORM_EPS) * g_ref[...]
    xn_ref[...] = xn.astype(xn_ref.dtype)
    logits = _dot3(xn, w_ref[...])
    lane = lax.broadcasted_iota(jnp.int32, logits.shape, 1).astype(F32)
    is_grp = lane < n_groups
    m = jnp.max(jnp.where(is_grp, logits, NEG_BIG), axis=-1, keepdims=True)
    e = jnp.where(is_grp, jnp.exp(jnp.where(is_grp, logits - m, 0.0)), 0.0)
    pg = e / jnp.sum(e, axis=-1, keepdims=True)
    pg_top = jnp.max(pg, axis=-1, keepdims=True)
    g_idx = jnp.min(jnp.where(is_grp & (pg == pg_top), lane, LANES), axis=-1, keepdims=True)
    elane = lane - n_groups
    in_grp = (elane >= g_idx * per_group) & (elane < (g_idx + 1.0) * per_group)
    m2 = jnp.max(jnp.where(in_grp, logits, NEG_BIG), axis=-1, keepdims=True)
    e2 = jnp.where(in_grp, jnp.exp(jnp.where(in_grp, logits - m2, 0.0)), 0.0)
    pe = e2 / jnp.sum(e2, axis=-1, keepdims=True)
    p1 = jnp.max(jnp.where(in_grp, pe, -1.0), axis=-1, keepdims=True)
    i1 = jnp.min(jnp.where(in_grp & (pe == p1), elane, LANES), axis=-1, keepdims=True)
    rest = in_grp & (elane != i1)
    p2 = jnp.max(jnp.where(rest, pe, -1.0), axis=-1, keepdims=True)
    i2 = jnp.min(jnp.where(rest & (pe == p2), elane, LANES), axis=-1, keepdims=True)
    denom = p1 + p2
    w1 = pg_top * p1 / denom
    w2 = pg_top * p2 / denom
    r_ref[...] = jnp.where(lane == 0, i1,
                           jnp.where(lane == 1, i2, jnp.where(lane == 2, w1, jnp.where(lane == 3, w2, 0.0))))


def moe_router(h, g, w_group, w_expert):
    t, d = h.shape
    n_groups = w_group.shape[1]
    n_exp = w_expert.shape[1]
    tm = _pick(t, 256)
    w_r = jnp.concatenate([w_group.astype(F32), w_expert.astype(F32),
                           jnp.zeros((d, LANES - n_groups - n_exp), F32)], axis=1)
    return pl.pallas_call(
        functools.partial(_router_kernel, n_groups=n_groups, per_group=n_exp // n_groups),
        grid=(t // tm,),
        in_specs=[pl.BlockSpec((tm, d), lambda i: (i, 0)),
                  pl.BlockSpec((1, d), lambda i: (0, 0)),
                  pl.BlockSpec((d, LANES), lambda i: (0, 0))],
        out_specs=[pl.BlockSpec((tm, d), lambda i: (i, 0)),
                   pl.BlockSpec((tm, LANES), lambda i: (i, 0))],
        out_shape=[jax.ShapeDtypeStruct((t, d), F32),
                   jax.ShapeDtypeStruct((t, LANES), F32)],
        compiler_params=_params("parallel"),
        name="moe_router",
    )(h, g.reshape(1, d), w_r)


def _row_gather_start(idx_ref, base, n_rows, src_hbm, dst, sem):
    def body(r, carry):
        pltpu.make_async_copy(src_hbm.at[pl.ds(idx_ref[base + r], 1), :], dst.at[pl.ds(r, 1), :], sem).start()
        return carry
    lax.fori_loop(0, n_rows, body, 0, unroll=8)


def _row_gather_wait(n_rows, src_hbm, dst, sem):
    pltpu.make_async_copy(src_hbm.at[pl.ds(0, n_rows), :], dst, sem).wait()


def _ffn_kernel(te_ref, nu_ref, rt_ref, x_hbm, w1_ref, w3_ref, w2_ref, o_ref, xbuf, sem, w1b, w3b, w2b):
    i = pl.program_id(0)
    n_used = nu_ref[0]

    @pl.when(i >= n_used)
    def _():
        o_ref[...] = jnp.zeros_like(o_ref)

    @pl.when(i < n_used)
    def _():
        slot = i % 2

        @pl.when(i == 0)
        def _():
            _row_gather_start(rt_ref, 0, MOE_TILE, x_hbm, xbuf.at[0], sem.at[0])

        @pl.when(i + 1 < n_used)
        def _():
            _row_gather_start(rt_ref, (i + 1) * MOE_TILE, MOE_TILE, x_hbm, xbuf.at[1 - slot], sem.at[1 - slot])

        prev = te_ref[jnp.maximum(i - 1, 0)]

        @pl.when((i == 0) | (te_ref[i] != prev))
        def _():
            w1b[...] = w1_ref[...].astype(BF16)
            w3b[...] = w3_ref[...].astype(BF16)
            w2b[...] = w2_ref[...].astype(BF16)

        _row_gather_wait(MOE_TILE, x_hbm, xbuf.at[slot], sem.at[slot])
        x = xbuf[slot].astype(BF16)
        h1 = jnp.dot(x, w1b[...], preferred_element_type=F32)
        h3 = jnp.dot(x, w3b[...], preferred_element_type=F32)
        hdn = (_silu(h1) * h3).astype(BF16)
        o_ref[...] = jnp.dot(hdn, w2b[...], preferred_element_type=F32).astype(o_ref.dtype)


def moe_ffn(xn, row_token, tile_expert, n_used, w1, w3, w2, layer):
    d = xn.shape[1]
    r = row_token.shape[0]
    ff = w1.shape[3]
    n_tiles = r // MOE_TILE
    grid_spec = pltpu.PrefetchScalarGridSpec(
        num_scalar_prefetch=3,
        grid=(n_tiles,),
        in_specs=[pl.BlockSpec(memory_space=pl.ANY),
                  pl.BlockSpec((None, None, d, ff), lambda i, te, nu, rt: (layer, te[i], 0, 0)),
                  pl.BlockSpec((None, None, d, ff), lambda i, te, nu, rt: (layer, te[i], 0, 0)),
                  pl.BlockSpec((None, None, ff, d), lambda i, te, nu, rt: (layer, te[i], 0, 0))],
        out_specs=pl.BlockSpec((MOE_TILE, d), lambda i, te, nu, rt: (i, 0)),
        scratch_shapes=[pltpu.VMEM((2, MOE_TILE, d), F32), pltpu.SemaphoreType.DMA((2,)),
                        pltpu.VMEM((d, ff), BF16), pltpu.VMEM((d, ff), BF16), pltpu.VMEM((ff, d), BF16)],
    )
    return pl.pallas_call(
        _ffn_kernel,
        grid_spec=grid_spec,
        out_shape=jax.ShapeDtypeStruct((r, d), F32),
        compiler_params=_params("arbitrary"),
        name="moe_ffn",
    )(tile_expert, n_used, row_token, xn, w1, w3, w2)


def _combine_kernel(dest_ref, h_ref, r_ref, y_hbm, o_ref, ybuf, sem, *, tm, n_tokens):
    i = pl.program_id(0)
    slot = i % 2

    def start(tile, to_slot):
        for k in range(MOE_TOP_K):
            _row_gather_start(dest_ref, k * n_tokens + tile * tm, tm, y_hbm, ybuf.at[to_slot, k], sem.at[to_slot])

    @pl.when(i == 0)
    def _():
        start(0, 0)

    @pl.when(i + 1 < pl.num_programs(0))
    def _():
        start(i + 1, 1 - slot)

    for k in range(MOE_TOP_K):
        _row_gather_wait(tm, y_hbm, ybuf.at[slot, k], sem.at[slot])
    r = r_ref[...]
    o_ref[...] = h_ref[...] + r[:, 2:3] * ybuf[slot, 0] + r[:, 3:4] * ybuf[slot, 1]


def moe_combine(h, y_sorted, dest, route):
    t, d = h.shape
    tm = _pick(t, 256)
    spec = pl.BlockSpec((tm, d), lambda i, dest: (i, 0))
    grid_spec = pltpu.PrefetchScalarGridSpec(
        num_scalar_prefetch=1,
        grid=(t // tm,),
        in_specs=[spec, pl.BlockSpec((tm, LANES), lambda i, dest: (i, 0)), pl.BlockSpec(memory_space=pl.ANY)],
        out_specs=spec,
        scratch_shapes=[pltpu.VMEM((2, MOE_TOP_K, tm, d), F32), pltpu.SemaphoreType.DMA((2,))],
    )
    return pl.pallas_call(
        functools.partial(_combine_kernel, tm=tm, n_tokens=t),
        grid_spec=grid_spec,
        out_shape=jax.ShapeDtypeStruct((t, d), F32),
        compiler_params=_params("arbitrary"),
        name="moe_combine",
    )(dest, h, route, y_sorted)


def hier_moe_layer(h, g, w_group, w_expert, w1, w3, w2, layer):
    t, d = h.shape
    n_exp = w1.shape[1]
    xn, route = moe_router(h, g, w_group, w_expert)
    e_flat = jnp.concatenate([route[:, 0], route[:, 1]]).astype(jnp.int32)
    onehot = (e_flat[:, None] == jnp.arange(n_exp, dtype=jnp.int32)[None, :]).astype(jnp.int32)
    counts = jnp.sum(onehot, axis=0)
    rank = jnp.sum((jnp.cumsum(onehot, axis=0) - onehot) * onehot, axis=1)
    padded = ((counts + MOE_TILE - 1) // MOE_TILE) * MOE_TILE
    ends = jnp.cumsum(padded)
    dest = jnp.sum(onehot * (ends - padded)[None, :], axis=1) + rank
    n_rows = MOE_TOP_K * t + n_exp * MOE_TILE
    n_rows = ((n_rows + MOE_TILE - 1) // MOE_TILE) * MOE_TILE
    token = jnp.concatenate([jnp.arange(t, dtype=jnp.int32)] * MOE_TOP_K)
    row_token = jnp.zeros((n_rows,), jnp.int32).at[dest].set(token)
    tile_start = jnp.arange(n_rows // MOE_TILE, dtype=jnp.int32) * MOE_TILE
    tile_expert = jnp.minimum(jnp.sum((ends[None, :] <= tile_start[:, None]).astype(jnp.int32), axis=1), n_exp - 1)
    n_used = (ends[-1] // MOE_TILE).astype(jnp.int32).reshape(1)
    y_sorted = moe_ffn(xn, row_token, tile_expert, n_used, w1, w3, w2, layer)
    return moe_combine(h, y_sorted, dest, route)


def _rope_tables(positions, half):
    inv = ROPE_THETA ** (-np.arange(half, dtype=np.float64) / half)
    ang = positions.astype(np.float64)[:, None] * inv[None, :]
    cos, sin = np.cos(ang), np.sin(ang)
    return (jnp.asarray(np.concatenate([cos, cos], axis=1), F32),
            jnp.asarray(np.concatenate([-sin, sin], axis=1), F32))


def _rope(x, cos2, sin2):
    half = x.shape[-1] // 2
    return x * cos2 + jnp.concatenate([x[:, half:], x[:, :half]], axis=-1) * sin2


def _kv_kernel(x_ref, w_ref, g_ref, cos_ref, sin_ref, ckv_ref, kpe_ref, wb_ref, *, rank):
    @pl.when(pl.program_id(0) == 0)
    def _():
        wb_ref[...] = w_ref[...].astype(BF16)

    c = jnp.dot(x_ref[...], wb_ref[...], preferred_element_type=F32)
    lat = c[:, :rank]
    ckv_ref[...] = lat * lax.rsqrt(jnp.mean(lat * lat, axis=-1, keepdims=True) + NORM_EPS) * g_ref[...]
    kpe_ref[...] = _rope(c[:, rank:], cos_ref[...], sin_ref[...])


def mla_shared_kv(xkv, w_dkv, latent_g, cos2, sin2, rank):
    t, d = xkv.shape
    rope_dim = w_dkv.shape[1] - rank
    tm = _pick(t, 512)
    return pl.pallas_call(
        functools.partial(_kv_kernel, rank=rank),
        grid=(t // tm,),
        in_specs=[pl.BlockSpec((tm, d), lambda i: (i, 0)),
                  pl.BlockSpec((d, rank + rope_dim), lambda i: (0, 0)),
                  pl.BlockSpec((1, rank), lambda i: (0, 0)),
                  pl.BlockSpec((tm, rope_dim), lambda i: (i, 0)),
                  pl.BlockSpec((tm, rope_dim), lambda i: (i, 0))],
        out_specs=[pl.BlockSpec((tm, rank), lambda i: (i, 0)),
                   pl.BlockSpec((tm, rope_dim), lambda i: (i, 0))],
        out_shape=[jax.ShapeDtypeStruct((t, rank), F32),
                   jax.ShapeDtypeStruct((t, rope_dim), F32)],
        scratch_shapes=[pltpu.VMEM((d, rank + rope_dim), BF16)],
        compiler_params=_params("arbitrary"),
        name="mla_shared_kv",
    )(xkv, w_dkv, latent_g.reshape(1, rank), cos2, sin2)


def _q_kernel(x_ref, w_ref, cos_ref, sin_ref, o_ref, *, nope, scale, n_prompt_blocks):
    sc = jnp.where(pl.program_id(0) < n_prompt_blocks, scale * LOG2E, scale)
    q = _dot(x_ref[...], w_ref[...])
    o_ref[:, :nope] = (q[:, :nope] * sc).astype(o_ref.dtype)
    o_ref[:, nope:] = (_rope(q[:, nope:], cos_ref[...], sin_ref[...]) * sc).astype(o_ref.dtype)


def mla_query(xn, w_q_heads, cos2, sin2, nope, scale, tp):
    t, d = xn.shape
    n_heads, _, qk = w_q_heads.shape
    rope_dim = qk - nope
    tm = _pick(math.gcd(tp, t - tp), 1024)
    return pl.pallas_call(
        functools.partial(_q_kernel, nope=nope, scale=scale, n_prompt_blocks=tp // tm),
        grid=(t // tm, n_heads),
        in_specs=[pl.BlockSpec((tm, d), lambda i, h: (i, 0)),
                  pl.BlockSpec((None, d, qk), lambda i, h: (h, 0, 0)),
                  pl.BlockSpec((tm, rope_dim), lambda i, h: (i, 0)),
                  pl.BlockSpec((tm, rope_dim), lambda i, h: (i, 0))],
        out_specs=pl.BlockSpec((None, tm, qk), lambda i, h: (h, i, 0)),
        out_shape=jax.ShapeDtypeStruct((n_heads, t, qk), BF16),
        compiler_params=_params("parallel", "arbitrary"),
        name="mla_query",
    )(xn, w_q_heads, cos2, sin2)


def _kv_expand_kernel(ckv_ref, kpe_ref, wk_ref, wv_ref, k_ref, v_ref, *, nope):
    ckv = ckv_ref[...]
    k_ref[:, :nope] = _dot(ckv, wk_ref[...]).astype(k_ref.dtype)
    k_ref[:, nope:] = kpe_ref[...].astype(k_ref.dtype)
    v_ref[...] = _dot(ckv, wv_ref[...]).astype(v_ref.dtype)


def mla_expand_kv(ckv, kpe, w_uk_heads, w_uv_heads, tp):
    rank = ckv.shape[1]
    rope_dim = kpe.shape[1]
    n_heads, _, nope = w_uk_heads.shape
    dv = w_uv_heads.shape[2]
    tm = _pick(tp, 1024)
    return pl.pallas_call(
        functools.partial(_kv_expand_kernel, nope=nope),
        grid=(tp // tm, n_heads),
        in_specs=[pl.BlockSpec((tm, rank), lambda i, h: (i, 0)),
                  pl.BlockSpec((tm, rope_dim), lambda i, h: (i, 0)),
                  pl.BlockSpec((None, rank, nope), lambda i, h: (h, 0, 0)),
                  pl.BlockSpec((None, rank, dv), lambda i, h: (h, 0, 0))],
        out_specs=[pl.BlockSpec((None, tm, nope + rope_dim), lambda i, h: (h, i, 0)),
                   pl.BlockSpec((None, tm, dv), lambda i, h: (h, i, 0))],
        out_shape=[jax.ShapeDtypeStruct((n_heads, tp, nope + rope_dim), BF16),
                   jax.ShapeDtypeStruct((n_heads, tp, dv), BF16)],
        compiler_params=_params("parallel", "arbitrary"),
        name="mla_expand_kv",
    )(ckv, kpe, w_uk_heads, w_uv_heads)


def _flash_kernel(qi_ref, kj_ref, q_ref, k_ref, v_ref, o_ref, m_scr, l_scr, acc_scr, *, tile):
    step = pl.program_id(1)
    qi = qi_ref[step]
    kj = kj_ref[step]

    @pl.when(kj == 0)
    def _():
        m_scr[...] = jnp.full_like(m_scr, NEG_BIG)
        l_scr[...] = jnp.zeros_like(l_scr)
        acc_scr[...] = jnp.zeros_like(acc_scr)

    s = lax.dot_general(q_ref[...], k_ref[...], (((1,), (1,)), ((), ())), preferred_element_type=F32)

    def update(s):
        m_prev = m_scr[...]
        m_new = jnp.maximum(m_prev, jnp.max(s, axis=-1, keepdims=True))
        alpha = jnp.exp2(m_prev - m_new)
        p = jnp.exp2(s - m_new)
        l_scr[...] = alpha * l_scr[...] + jnp.sum(p, axis=-1, keepdims=True)
        acc_scr[...] = alpha * acc_scr[...] + jnp.dot(p.astype(BF16), v_ref[...], preferred_element_type=F32)
        m_scr[...] = m_new

    @pl.when(kj < qi)
    def _():
        update(s)

    @pl.when(kj == qi)
    def _():
        row = lax.broadcasted_iota(jnp.int32, s.shape, 0)
        col = lax.broadcasted_iota(jnp.int32, s.shape, 1)
        update(jnp.where(col <= row, s, NEG_BIG))
        o_ref[...] = (acc_scr[...] / l_scr[...]).astype(o_ref.dtype)


def mla_prompt_attention(q_heads, k_heads, v_heads, tp):
    n_heads, _, qk = q_heads.shape
    dv = v_heads.shape[2]
    tile = _pick(tp, ATTN_TILE)
    n_blocks = tp // tile
    pairs = [(i, j) for i in range(n_blocks) for j in range(i + 1)]
    qi_list = jnp.asarray([p[0] for p in pairs], jnp.int32)
    kj_list = jnp.asarray([p[1] for p in pairs], jnp.int32)
    grid_spec = pltpu.PrefetchScalarGridSpec(
        num_scalar_prefetch=2,
        grid=(n_heads, len(pairs)),
        in_specs=[pl.BlockSpec((None, tile, qk), lambda h, s, qi, kj: (h, qi[s], 0)),
                  pl.BlockSpec((None, tile, qk), lambda h, s, qi, kj: (h, kj[s], 0)),
                  pl.BlockSpec((None, tile, dv), lambda h, s, qi, kj: (h, kj[s], 0))],
        out_specs=pl.BlockSpec((tile, dv), lambda h, s, qi, kj: (qi[s], h)),
        scratch_shapes=[pltpu.VMEM((tile, 1), F32), pltpu.VMEM((tile, 1), F32), pltpu.VMEM((tile, dv), F32)],
    )
    return pl.pallas_call(
        functools.partial(_flash_kernel, tile=tile),
        grid_spec=grid_spec,
        out_shape=jax.ShapeDtypeStruct((tp, n_heads * dv), BF16),
        compiler_params=_params("parallel", "arbitrary"),
        name="mla_prompt_attention",
    )(qi_list, kj_list, q_heads, k_heads, v_heads)


def _absorb_kernel(q_ref, wk_ref, o_ref, *, nope):
    q = q_ref[...]
    rank = wk_ref.shape[0]
    o_ref[:, :rank] = _dot_nt(q[:, :nope], wk_ref[...]).astype(o_ref.dtype)
    o_ref[:, rank:] = q[:, nope:]


def mla_absorb_query(q_heads, w_uk_heads, tp):
    n_heads, t, qk = q_heads.shape
    _, rank, nope = w_uk_heads.shape
    ts = t - tp
    tm = _pick(math.gcd(ts, tp), 1024)
    off = tp // tm
    return pl.pallas_call(
        functools.partial(_absorb_kernel, nope=nope),
        grid=(n_heads, ts // tm),
        in_specs=[pl.BlockSpec((None, tm, qk), lambda h, i: (h, off + i, 0)),
                  pl.BlockSpec((None, rank, nope), lambda h, i: (h, 0, 0))],
        out_specs=pl.BlockSpec((None, tm, rank + qk - nope), lambda h, i: (h, i, 0)),
        out_shape=jax.ShapeDtypeStruct((n_heads, ts, rank + qk - nope), BF16),
        compiler_params=_params("parallel", "parallel"),
        name="mla_absorb_query",
    )(q_heads, w_uk_heads)


def _paged_kernel(pt_ref, q_ref, cn_ref, kn_ref, *rest, pages_per_step, rank, dec_seq, page, n_sub):
    ckv_refs = rest[:pages_per_step]
    kpe_refs = rest[pages_per_step:2 * pages_per_step]
    o_ref, m_scr, l_scr, acc_scr, ck_scr, kp_scr = rest[2 * pages_per_step:]
    j = pl.program_id(1)
    n_heads = q_ref.shape[0]
    q = q_ref[...].reshape(n_heads * dec_seq, q_ref.shape[2])
    q_lat, q_pe = q[:, :rank], q[:, rank:]

    @pl.when(j == 0)
    def _():
        pad = lambda a: jnp.concatenate([a, jnp.zeros((LANES - dec_seq, a.shape[1]), a.dtype)], axis=0)
        cn = pad(cn_ref[...]).astype(BF16)
        s = _dot_nt(q_lat, cn) + _dot_nt(q_pe, pad(kn_ref[...]))
        qtok = lax.broadcasted_iota(jnp.int32, s.shape, 0) & (dec_seq - 1)
        ktok = lax.broadcasted_iota(jnp.int32, s.shape, 1)
        s = jnp.where(ktok <= qtok, s, NEG_BIG)
        m0 = jnp.max(s, axis=-1, keepdims=True)
        p = jnp.exp(s - m0)
        m_scr[...] = m0
        l_scr[...] = jnp.sum(p, axis=-1, keepdims=True)
        acc_scr[...] = jnp.dot(p.astype(BF16), cn, preferred_element_type=F32)

    for pg in range(pages_per_step):
        ck_scr[pg * page:(pg + 1) * page, :] = ckv_refs[pg][...].astype(BF16)
        kp_scr[:, pg * page:(pg + 1) * page] = kpe_refs[pg][...].astype(BF16)

    sub = pages_per_step * page // n_sub
    cks = [ck_scr[i * sub:(i + 1) * sub, :] for i in range(n_sub)]
    ss = [lax.dot_general(q_lat, cks[i], (((1,), (1,)), ((), ())), preferred_element_type=F32)
          + jnp.dot(q_pe, kp_scr[:, i * sub:(i + 1) * sub], preferred_element_type=F32) for i in range(n_sub)]
    ms = [jnp.max(s, axis=-1, keepdims=True) for s in ss]
    ps = [jnp.exp(s - m) for s, m in zip(ss, ms)]
    ls = [jnp.sum(p, axis=-1, keepdims=True) for p in ps]
    accs = [jnp.dot(p.astype(BF16), ck, preferred_element_type=F32) for p, ck in zip(ps, cks)]
    m_prev = m_scr[...]
    m_new = functools.reduce(jnp.maximum, ms, m_prev)
    alpha = jnp.exp(m_prev - m_new)
    l_new = alpha * l_scr[...]
    acc_new = alpha * acc_scr[...]
    for m, l, acc in zip(ms, ls, accs):
        coef = jnp.exp(m - m_new)
        l_new = l_new + coef * l
        acc_new = acc_new + coef * acc
    m_scr[...] = m_new
    l_scr[...] = l_new
    acc_scr[...] = acc_new

    @pl.when(j == pl.num_programs(1) - 1)
    def _():
        o_ref[...] = (acc_new / l_new).reshape(o_ref.shape).astype(o_ref.dtype)


def mla_paged_attention(q_abs, ckv_new, kpe_new, cache_ckv, cache_kpe_t, page_table, dec_seq):
    n_heads, ts, key_dim = q_abs.shape
    n_seq, n_pages = page_table.shape
    _, page, rank = cache_ckv.shape
    rope_dim = cache_kpe_t.shape[1]
    pps = _divisor(n_pages, PAGES_PER_STEP)
    rows = n_heads * dec_seq

    def page_map(pg):
        return lambda s, j, pt: (pt[s, j * pps + pg], 0, 0)

    in_specs = [pl.BlockSpec((n_heads, dec_seq, key_dim), lambda s, j, pt: (0, s, 0)),
                pl.BlockSpec((dec_seq, rank), lambda s, j, pt: (s, 0)),
                pl.BlockSpec((dec_seq, rope_dim), lambda s, j, pt: (s, 0))]
    in_specs += [pl.BlockSpec((None, page, rank), page_map(pg)) for pg in range(pps)]
    in_specs += [pl.BlockSpec((None, rope_dim, page), page_map(pg)) for pg in range(pps)]
    grid_spec = pltpu.PrefetchScalarGridSpec(
        num_scalar_prefetch=1,
        grid=(n_seq, n_pages // pps),
        in_specs=in_specs,
        out_specs=pl.BlockSpec((n_heads, dec_seq, rank), lambda s, j, pt: (0, s, 0)),
        scratch_shapes=[pltpu.VMEM((rows, 1), F32), pltpu.VMEM((rows, 1), F32), pltpu.VMEM((rows, rank), F32),
                        pltpu.VMEM((pps * page, rank), BF16), pltpu.VMEM((rope_dim, pps * page), BF16)],
    )
    return pl.pallas_call(
        functools.partial(_paged_kernel, pages_per_step=pps, rank=rank, dec_seq=dec_seq, page=page,
                          n_sub=_divisor(pps, PAGED_SUB_BLOCKS)),
        grid_spec=grid_spec,
        out_shape=jax.ShapeDtypeStruct((n_heads, ts, rank), BF16),
        compiler_params=_params("parallel", "arbitrary"),
        name="mla_paged_attention",
    )(page_table, q_abs, ckv_new, kpe_new, *([cache_ckv] * pps), *([cache_kpe_t] * pps))


def _unabsorb_kernel(o_ref, wv_ref, out_ref):
    out_ref[...] = _dot(o_ref[...], wv_ref[...]).astype(out_ref.dtype)


def mla_unabsorb(o_lat, w_uv_heads):
    n_heads, ts, rank = o_lat.shape
    dv = w_uv_heads.shape[2]
    tm = _pick(ts, 1024)
    return pl.pallas_call(
        _unabsorb_kernel,
        grid=(n_heads, ts // tm),
        in_specs=[pl.BlockSpec((None, tm, rank), lambda h, i: (h, i, 0)),
                  pl.BlockSpec((None, rank, dv), lambda h, i: (h, 0, 0))],
        out_specs=pl.BlockSpec((tm, dv), lambda h, i: (i, h)),
        out_shape=jax.ShapeDtypeStruct((ts, n_heads * dv), BF16),
        compiler_params=_params("parallel", "parallel"),
        name="mla_unabsorb",
    )(o_lat, w_uv_heads)


def kernel(x_prompt, x_sample, state_ssm, state_conv, cache_ckv, cache_kpe, page_table, norm_mix, norm_ffn,
           norm_final, gdn_w_in, gdn_conv_w, gdn_a_log, gdn_dt_bias, gdn_norm, gdn_w_out, kv_norm, w_dkv,
           kv_latent_norm, w_uk, w_uv, mla_w_q, mla_w_o, moe_w_group, moe_w_expert, moe_w1, moe_w3, moe_w2):
    batch, seq, d = x_prompt.shape
    n_seq, dec_seq, _ = x_sample.shape
    assert batch == 1 and dec_seq == SUBLANES
    tp, ts = seq, n_seq * dec_seq
    depth = norm_mix.shape[0]
    n_a = state_ssm.shape[0]
    n_vh, dk, dv = state_ssm.shape[2:]
    conv_dim = state_conv.shape[3]
    qk_dim = (conv_dim - n_vh * dv) // 2
    n_kh = qk_dim // dk
    n_taps = gdn_conv_w.shape[1]
    rank = kv_latent_norm.shape[0]
    rope_dim = w_dkv.shape[1] - rank
    n_heads, nope = w_uk.shape[1:]
    past_len = page_table.shape[1] * cache_ckv.shape[1]
    scale = float((nope + rope_dim) ** -0.5)

    h = jnp.concatenate([x_prompt.reshape(tp, d), x_sample.reshape(ts, d)], axis=0)
    positions = np.concatenate([np.arange(tp), past_len + np.arange(dec_seq)[None, :].repeat(n_seq, 0).reshape(-1)])
    cos2, sin2 = _rope_tables(positions, rope_dim // 2)

    ssm_p, conv_p, ssm_s, conv_s = [], [], [], []
    ckv = kpe = None
    for i in range(depth):
        if i < n_a:
            xn = rmsnorm(h, norm_mix[i], BF16)
            w_in = gdn_w_in[i]
            main_cols = conv_dim + n_vh * dv
            proj = matmul(xn, w_in, F32, n_cols=main_cols)
            gates = gdn_gates(xn, w_in[:, main_cols:], gdn_a_log[i], gdn_dt_bias[i])
            hist = jnp.pad(state_conv[i], ((0, 0), (SUBLANES - (n_taps - 1), 0), (0, 0)))
            qkv = gdn_conv(proj, hist, gdn_conv_w[i], tp, qk_dim, conv_dim, dk)
            gates_t = gates.T
            o_p, s_p = gdn_prompt(qkv, proj, gates_t, gdn_norm[i], tp, n_kh, n_vh, dk, dv, GDN_CHUNK)
            o_s, s_s = gdn_sample(qkv, proj, gates_t, gdn_norm[i], state_ssm[i], tp, n_kh, n_vh, dk, dv, dec_seq)
            h = matmul(jnp.concatenate([o_p, o_s], axis=0), gdn_w_out[i], F32, res=h, tn_pref=512)
            ssm_p.append(s_p[None])
            ssm_s.append(s_s)
            conv_p.append(proj[tp - (n_taps - 1):tp, :conv_dim][None])
            conv_s.append(proj[tp:, :conv_dim].reshape(n_seq, dec_seq, conv_dim)[:, dec_seq - (n_taps - 1):])
        else:
            if i == n_a:
                ckv, kpe = mla_shared_kv(rmsnorm(h, kv_norm, BF16), w_dkv, kv_latent_norm, cos2, sin2, rank)
                w_uk_heads = jnp.transpose(w_uk, (1, 0, 2))
                w_uv_heads = jnp.transpose(w_uv, (1, 0, 2))
                k_heads, v_heads = mla_expand_kv(ckv, kpe, w_uk_heads, w_uv_heads, tp)
                cache_kpe_t = jnp.swapaxes(cache_kpe, 1, 2)
            j = i - n_a
            xn = rmsnorm(h, norm_mix[i], BF16)
            w_q_heads = jnp.transpose(mla_w_q[j].reshape(d, n_heads, nope + rope_dim), (1, 0, 2))
            q_heads = mla_query(xn, w_q_heads, cos2, sin2, nope, scale, tp)
            o_p = mla_prompt_attention(q_heads, k_heads, v_heads, tp)
            q_abs = mla_absorb_query(q_heads, w_uk_heads, tp)
            o_lat = mla_paged_attention(q_abs, ckv[tp:], kpe[tp:], cache_ckv, cache_kpe_t, page_table, dec_seq)
            o_s = mla_unabsorb(o_lat, w_uv_heads)
            h = matmul(jnp.concatenate([o_p, o_s], axis=0), mla_w_o[j], F32, res=h)
        h = hier_moe_layer(h, norm_ffn[i], moe_w_group[i], moe_w_expert[i], moe_w1, moe_w3, moe_w2, i)

    y_prompt = rmsnorm(h, norm_final, F32, 0, tp).reshape(batch, seq, d)
    y_sample = rmsnorm(h, norm_final, F32, tp, ts).reshape(n_seq, dec_seq, d)
    return (y_prompt, y_sample,
            jnp.stack(ssm_p), jnp.stack(conv_p),
            ckv[:tp].reshape(batch, seq, rank), kpe[:tp].reshape(batch, seq, rope_dim),
            jnp.stack(ssm_s), jnp.stack(conv_s),
            ckv[tp:].reshape(n_seq, dec_seq, rank), kpe[tp:].reshape(n_seq, dec_seq, rope_dim))
```

```python
import functools
import math

import numpy as np
import jax
import jax.numpy as jnp
from jax import lax
from jax.experimental import pallas as pl
from jax.experimental.pallas import tpu as pltpu

F32 = jnp.float32
BF16 = jnp.bfloat16

NORM_EPS = 1e-6
ROPE_THETA = 10000.0
GDN_CHUNK = 64
MOE_TOP_K = 2
SUBLANES = 8
LANES = 128
VMEM_LIMIT_BYTES = 56 * 1024 * 1024
GDN_ROWS = 128
GDN_PROMPT_HEADS = 8
GDN_SAMPLE_HEADS = 4
MOE_TILE = 256
ATTN_TILE = 1024
PAGES_PER_STEP = 32
PAGED_SUB_BLOCKS = 4
NEG_BIG = -1e30
LOG2E = math.log2(math.e)


def _params(*sem):
    return pltpu.CompilerParams(dimension_semantics=sem, vmem_limit_bytes=VMEM_LIMIT_BYTES)


def _pick(n, pref):
    t = min(pref, n)
    while n % t:
        t -= SUBLANES
    return t


def _divisor(n, pref):
    t = min(pref, n)
    while n % t:
        t -= 1
    return t


def _dot(a, b):
    return jnp.dot(a.astype(BF16), b.astype(BF16), preferred_element_type=F32)


def _dot_nt(a, b):
    return lax.dot_general(a.astype(BF16), b.astype(BF16), (((1,), (1,)), ((), ())),
                           preferred_element_type=F32)


def _split(a):
    hi = a.astype(BF16)
    lo = (a - hi.astype(F32)).astype(BF16)
    return hi, lo


def _dot3(a, b):
    ah, al = _split(a)
    bh, bl = _split(b)
    d = lambda x, y: jnp.dot(x, y, preferred_element_type=F32)
    return d(ah, bh) + (d(ah, bl) + d(al, bh))


def _sigmoid(x):
    return 1.0 / (1.0 + jnp.exp(-x))


def _silu(x):
    return x * _sigmoid(x)


def _rmsnorm_kernel(x_ref, g_ref, o_ref):
    x = x_ref[...]
    y = x * lax.rsqrt(jnp.mean(x * x, axis=-1, keepdims=True) + NORM_EPS) * g_ref[...]
    o_ref[...] = y.astype(o_ref.dtype)


def rmsnorm(x, g, out_dtype, row_start=0, rows=None):
    t, d = x.shape
    rows = t if rows is None else rows
    tm = _pick(math.gcd(rows, row_start) if row_start else rows, 512)
    off = row_start // tm
    return pl.pallas_call(
        _rmsnorm_kernel,
        grid=(rows // tm,),
        in_specs=[pl.BlockSpec((tm, d), lambda i: (i + off, 0)),
                  pl.BlockSpec((1, d), lambda i: (0, 0))],
        out_specs=pl.BlockSpec((tm, d), lambda i: (i, 0)),
        out_shape=jax.ShapeDtypeStruct((rows, d), out_dtype),
        compiler_params=_params("parallel"),
        name="rmsnorm",
    )(x, g.reshape(1, d))


def _matmul_kernel(*refs, has_res):
    if has_res:
        x_ref, w_ref, r_ref, o_ref, wb_ref = refs
    else:
        x_ref, w_ref, o_ref, wb_ref = refs

    @pl.when(pl.program_id(1) == 0)
    def _():
        wb_ref[...] = w_ref[...].astype(BF16)

    acc = jnp.dot(x_ref[...], wb_ref[...], preferred_element_type=F32)
    if has_res:
        acc = acc + r_ref[...]
    o_ref[...] = acc.astype(o_ref.dtype)


def matmul(x, w, out_dtype, n_cols=None, res=None, tm_pref=1024, tn_pref=1024):
    m, k = x.shape
    n = w.shape[1] if n_cols is None else n_cols
    tm = _pick(m, tm_pref)
    tn = _pick(n, tn_pref)
    in_specs = [pl.BlockSpec((tm, k), lambda j, i: (i, 0)),
                pl.BlockSpec((k, tn), lambda j, i: (0, j))]
    args = [x, w]
    if res is not None:
        in_specs.append(pl.BlockSpec((tm, tn), lambda j, i: (i, j)))
        args.append(res)
    return pl.pallas_call(
        functools.partial(_matmul_kernel, has_res=res is not None),
        grid=(n // tn, m // tm),
        in_specs=in_specs,
        out_specs=pl.BlockSpec((tm, tn), lambda j, i: (i, j)),
        out_shape=jax.ShapeDtypeStruct((m, n), out_dtype),
        scratch_shapes=[pltpu.VMEM((k, tn), BF16)],
        compiler_params=_params("parallel", "arbitrary"),
        name="matmul",
    )(*args)


def _gates_kernel(x_ref, w_ref, a_ref, dt_ref, o_ref, *, n_heads):
    ba = _dot(x_ref[...], w_ref[...])
    lane = lax.broadcasted_iota(jnp.int32, ba.shape, 1)
    beta = _sigmoid(ba)
    xs = ba + dt_ref[...]
    softplus = jnp.maximum(xs, 0.0) + jnp.log(1.0 + jnp.exp(-jnp.abs(xs)))
    g = -jnp.exp(a_ref[...]) * softplus
    o_ref[...] = jnp.where(lane < n_heads, beta, g)


def gdn_gates(xn, w_ba, a_log, dt_bias):
    t, d = xn.shape
    hv = a_log.shape[0]
    tm = _pick(t, 512)
    zeros = jnp.zeros((hv,), F32)
    a_pad = jnp.concatenate([zeros, a_log.astype(F32)]).reshape(1, 2 * hv)
    dt_pad = jnp.concatenate([zeros, dt_bias.astype(F32)]).reshape(1, 2 * hv)
    return pl.pallas_call(
        functools.partial(_gates_kernel, n_heads=hv),
        grid=(t // tm,),
        in_specs=[pl.BlockSpec((tm, d), lambda i: (i, 0)),
                  pl.BlockSpec((d, 2 * hv), lambda i: (0, 0)),
                  pl.BlockSpec((1, 2 * hv), lambda i: (0, 0)),
                  pl.BlockSpec((1, 2 * hv), lambda i: (0, 0))],
        out_specs=pl.BlockSpec((tm, 2 * hv), lambda i: (i, 0)),
        out_shape=jax.ShapeDtypeStruct((t, 2 * hv), F32),
        compiler_params=_params("parallel"),
        name="gdn_gates",
    )(xn, w_ba, a_pad, dt_pad)


def _conv_kernel(x_ref, prev_ref, st_ref, w_ref, o_ref, *, n_prompt_blocks, n_q_blocks, n_qk_blocks,
                 n_taps, head_dim):
    i = pl.program_id(0)
    j = pl.program_id(1)
    x3 = x_ref[...]
    prev = jnp.where(i == 0, 0.0, prev_ref[...])
    hist_prompt = jnp.concatenate([prev, x3[:-1]], axis=0)
    hist = jnp.where(i >= n_prompt_blocks, st_ref[...], hist_prompt)
    row = lax.broadcasted_iota(jnp.int32, x3.shape, 1)
    tap = lambda s: w_ref[n_taps - 1 - s:n_taps - s, :][None]
    acc = x3 * tap(0)
    for s in range(1, n_taps):
        shifted = jnp.where(row >= s, pltpu.roll(x3, s, 1), pltpu.roll(hist, s, 1))
        acc = acc + shifted * tap(s)
    act = _silu(acc).reshape(x3.shape[0] * SUBLANES, x3.shape[2])
    scale = jnp.where(j < n_q_blocks, head_dim ** -0.5, 1.0)
    is_qk = j < n_qk_blocks
    for c in range(act.shape[1] // head_dim):
        seg = act[:, c * head_dim:(c + 1) * head_dim]
        nrm = seg * (lax.rsqrt(jnp.sum(seg * seg, axis=-1, keepdims=True) + NORM_EPS) * scale)
        o_ref[:, c * head_dim:(c + 1) * head_dim] = jnp.where(is_qk, nrm, seg)


def gdn_conv(proj, state_hist, conv_w, tp, qk_dim, conv_dim, head_dim):
    t = proj.shape[0]
    n_taps = conv_w.shape[0]
    tm = _pick(math.gcd(tp, t - tp), 512)
    tc = _pick(qk_dim, 1024)
    tiles = tm // SUBLANES
    proj3 = proj.reshape(t // SUBLANES, SUBLANES, proj.shape[1])
    npb = tp // tm
    return pl.pallas_call(
        functools.partial(_conv_kernel, n_prompt_blocks=npb, n_q_blocks=qk_dim // tc,
                          n_qk_blocks=2 * qk_dim // tc, n_taps=n_taps, head_dim=head_dim),
        grid=(t // tm, conv_dim // tc),
        in_specs=[pl.BlockSpec((tiles, SUBLANES, tc), lambda i, j: (i, 0, j)),
                  pl.BlockSpec((1, SUBLANES, tc), lambda i, j: (jnp.maximum(i * tiles - 1, 0), 0, j)),
                  pl.BlockSpec((tiles, SUBLANES, tc), lambda i, j: (jnp.maximum(i - npb, 0), 0, j)),
                  pl.BlockSpec((n_taps, tc), lambda i, j: (0, j))],
        out_specs=pl.BlockSpec((tm, tc), lambda i, j: (i, j)),
        out_shape=jax.ShapeDtypeStruct((t, conv_dim), F32),
        compiler_params=_params("parallel", "parallel"),
        name="gdn_conv",
    )(proj3, proj3, state_hist, conv_w)


def _gdn_blocks(qs, ks, qk_of, vs, zs, g_rows, b_rows, norm_g, states, chunk):
    r = GDN_ROWS
    n_groups = r // chunk
    units = range(len(vs))
    ri = lax.broadcasted_iota(jnp.int32, (r, r), 0)
    ci = lax.broadcasted_iota(jnp.int32, (r, r), 1)
    shift = int(math.log2(chunk))
    rg = ri >> shift
    cg = ci >> shift
    same = rg == cg
    causal = same & (ri >= ci)
    strict = same & (ri > ci)
    eye = ri == ci
    last = ci == (rg << shift) + (chunk - 1)
    eye_f = jnp.where(eye, 1.0, 0.0)
    bcast = lambda a: jnp.broadcast_to(a, (r, r))

    gc_col = [jnp.sum(jnp.where(causal, bcast(g_rows[u]), 0.0), axis=1, keepdims=True) for u in units]
    gc_rb = [bcast(jnp.sum(jnp.where(eye, bcast(gc_col[u]), 0.0), axis=0, keepdims=True)) for u in units]
    b_col = [jnp.sum(jnp.where(eye, bcast(b_rows[u]), 0.0), axis=1, keepdims=True) for u in units]
    glast_col = [jnp.sum(jnp.where(last, gc_rb[u], 0.0), axis=1, keepdims=True) for u in units]
    decay = [jnp.where(causal, jnp.exp(jnp.where(causal, gc_col[u] - gc_rb[u], 0.0)), 0.0) for u in units]

    kk = [_dot_nt(k, k) for k in ks]
    qk = [_dot_nt(q, k) for q, k in zip(qs, ks)]
    lmat = [jnp.where(strict, b_col[u] * kk[qk_of[u]] * decay[u], 0.0) for u in units]
    tmat = [eye_f - lmat[u] for u in units]
    pw = lmat
    for _ in range(shift - 1):
        pw = [_dot(pw[u], pw[u]) for u in units]
        tmat = [tmat[u] + _dot(tmat[u], pw[u]) for u in units]

    eg_col = [jnp.exp(gc_col[u]) for u in units]
    uu = [_dot(tmat[u], vs[u] * b_col[u]) for u in units]
    ww = [_dot(tmat[u], ks[qk_of[u]] * (b_col[u] * eg_col[u])) for u in units]
    qg = [qs[qk_of[u]] * eg_col[u] for u in units]
    amat = [jnp.where(causal, qk[qk_of[u]] * decay[u], 0.0) for u in units]

    ws_qs = [[_dot(jnp.concatenate([ww[u][gi * chunk:(gi + 1) * chunk], qg[u][gi * chunk:(gi + 1) * chunk]], axis=0),
                   states[u][gi]) for gi in range(n_groups)] for u in units]
    vn = [uu[u] - jnp.concatenate([ws_qs[u][gi][:chunk] for gi in range(n_groups)], axis=0) for u in units]
    o = [jnp.concatenate([ws_qs[u][gi][chunk:] for gi in range(n_groups)], axis=0) + _dot(amat[u], vn[u])
         for u in units]

    kd_t = [(ks[qk_of[u]] * jnp.exp(glast_col[u] - gc_col[u])).T for u in units]
    new_states = [[states[u][gi] * jnp.exp(glast_col[u][gi * chunk:gi * chunk + 1, :])
                   + _dot(jnp.where(cg == gi, kd_t[u], 0.0), vn[u]) for gi in range(n_groups)] for u in units]

    outs = [o[u] * lax.rsqrt(jnp.mean(o[u] * o[u], axis=-1, keepdims=True) + NORM_EPS) * norm_g * _silu(zs[u])
            for u in units]
    return outs, new_states


def _gdn_prompt_kernel(q_ref, k_ref, v_ref, z_ref, g_ref, b_ref, ng_ref, o_ref, sfin_ref, s_scr, *,
                       heads_per_step, chunk, dv):
    c = pl.program_id(1)
    rep = GDN_ROWS // chunk

    @pl.when(c == 0)
    def _():
        s_scr[...] = jnp.zeros_like(s_scr)

    heads = range(heads_per_step)
    tile = lambda ref, hh: jnp.concatenate([ref[:, hh * LANES:(hh + 1) * LANES]] * rep, axis=0)
    stack = lambda ref, hh: jnp.concatenate(
        [ref[:, (hh * rep + e) * dv:(hh * rep + e + 1) * dv] for e in range(rep)], axis=0)
    outs, new_states = _gdn_blocks(
        [tile(q_ref, hh) for hh in heads], [tile(k_ref, hh) for hh in heads], list(heads),
        [stack(v_ref, hh) for hh in heads], [stack(z_ref, hh) for hh in heads],
        [g_ref[hh] for hh in heads], [b_ref[hh] for hh in heads], ng_ref[...],
        [[s_scr[hh * rep + e] for e in range(rep)] for hh in heads], chunk)
    for hh in heads:
        for e in range(rep):
            s_scr[hh * rep + e] = new_states[hh][e]
            o_ref[:, (hh * rep + e) * dv:(hh * rep + e + 1) * dv] = (
                outs[hh][e * chunk:(e + 1) * chunk].astype(o_ref.dtype))

    @pl.when(c == pl.num_programs(1) - 1)
    def _():
        sfin_ref[...] = s_scr[...]


def gdn_prompt(qkv, proj, gates_t, norm_g, tp, n_kh, n_vh, dk, dv, chunk):
    rep = n_vh // n_kh
    assert rep * chunk == GDN_ROWS and dk == LANES and dv == LANES
    hps = _divisor(n_kh, GDN_PROMPT_HEADS)
    n_chunks = tp // chunk
    qk_blocks = n_kh // hps
    conv_dim = 2 * n_kh * dk + n_vh * dv

    def arrange(gt):
        a = gt[:, :tp].reshape(qk_blocks, hps, rep, n_chunks, chunk)
        return a.transpose(0, 3, 1, 2, 4).reshape(qk_blocks, n_chunks, hps, 1, GDN_ROWS)

    beta_a = arrange(gates_t[:n_vh])
    g_a = arrange(gates_t[n_vh:])
    vw = hps * rep * dv
    v_off = 2 * n_kh * dk // vw
    z_off = conv_dim // vw
    gate_spec = pl.BlockSpec((None, None, hps, 1, GDN_ROWS), lambda h, c: (h, c, 0, 0, 0))
    return pl.pallas_call(
        functools.partial(_gdn_prompt_kernel, heads_per_step=hps, chunk=chunk, dv=dv),
        grid=(qk_blocks, n_chunks),
        in_specs=[pl.BlockSpec((chunk, hps * dk), lambda h, c: (c, h)),
                  pl.BlockSpec((chunk, hps * dk), lambda h, c: (c, qk_blocks + h)),
                  pl.BlockSpec((chunk, vw), lambda h, c: (c, v_off + h)),
                  pl.BlockSpec((chunk, vw), lambda h, c: (c, z_off + h)),
                  gate_spec, gate_spec,
                  pl.BlockSpec((1, dv), lambda h, c: (0, 0))],
        out_specs=[pl.BlockSpec((chunk, vw), lambda h, c: (c, h)),
                   pl.BlockSpec((hps * rep, dk, dv), lambda h, c: (h, 0, 0))],
        out_shape=[jax.ShapeDtypeStruct((tp, n_vh * dv), BF16),
                   jax.ShapeDtypeStruct((n_vh, dk, dv), F32)],
        scratch_shapes=[pltpu.VMEM((hps * rep, dk, dv), F32)],
        compiler_params=_params("parallel", "arbitrary"),
        name="gdn_prompt",
    )(qkv, qkv, qkv, proj, g_a, beta_a, norm_g.reshape(1, dv))


def _gdn_sample_kernel(q_ref, k_ref, v_ref, z_ref, g_ref, b_ref, ng_ref, s_ref, o_ref, snew_ref, *,
                       heads_per_step, rep, chunk, dv):
    n_groups = GDN_ROWS // chunk
    n_qk = heads_per_step // rep
    heads = range(heads_per_step)
    cols = lambda ref, n, w: [ref[:, i * w:(i + 1) * w] for i in range(n)]
    outs, new_states = _gdn_blocks(
        cols(q_ref, n_qk, LANES), cols(k_ref, n_qk, LANES), [hh // rep for hh in heads],
        cols(v_ref, heads_per_step, dv), cols(z_ref, heads_per_step, dv),
        [g_ref[hh] for hh in heads], [b_ref[hh] for hh in heads], ng_ref[...],
        [[s_ref[gi, hh] for gi in range(n_groups)] for hh in heads], chunk)
    for hh in heads:
        o_ref[:, hh * dv:(hh + 1) * dv] = outs[hh].astype(o_ref.dtype)
        for gi in range(n_groups):
            snew_ref[gi, hh] = new_states[hh][gi]


def gdn_sample(qkv, proj, gates_t, norm_g, state, tp, n_kh, n_vh, dk, dv, chunk):
    rep = n_vh // n_kh
    t = qkv.shape[0]
    ts = t - tp
    assert dk == LANES and dv == LANES and ts % GDN_ROWS == 0 and tp % GDN_ROWS == 0
    hps = rep * _divisor(n_kh, GDN_SAMPLE_HEADS // rep)
    n_blocks = ts // GDN_ROWS
    spb = GDN_ROWS // chunk
    row_off = tp // GDN_ROWS
    conv_dim = 2 * n_kh * dk + n_vh * dv
    n_qk = hps // rep

    def arrange(gt):
        return gt[:, tp:].reshape(n_vh // hps, hps, n_blocks, 1, GDN_ROWS).transpose(0, 2, 1, 3, 4)

    beta_a = arrange(gates_t[:n_vh])
    g_a = arrange(gates_t[n_vh:])
    gate_spec = pl.BlockSpec((None, None, hps, 1, GDN_ROWS), lambda b, h: (h, b, 0, 0, 0))
    state_spec = pl.BlockSpec((spb, hps, dk, dv), lambda b, h: (b, h, 0, 0))
    return pl.pallas_call(
        functools.partial(_gdn_sample_kernel, heads_per_step=hps, rep=rep, chunk=chunk, dv=dv),
        grid=(n_blocks, n_vh // hps),
        in_specs=[pl.BlockSpec((GDN_ROWS, n_qk * dk), lambda b, h: (row_off + b, h)),
                  pl.BlockSpec((GDN_ROWS, n_qk * dk), lambda b, h: (row_off + b, n_kh // n_qk + h)),
                  pl.BlockSpec((GDN_ROWS, hps * dv), lambda b, h: (row_off + b, 2 * n_kh * dk // (hps * dv) + h)),
                  pl.BlockSpec((GDN_ROWS, hps * dv), lambda b, h: (row_off + b, conv_dim // (hps * dv) + h)),
                  gate_spec, gate_spec,
                  pl.BlockSpec((1, dv), lambda b, h: (0, 0)),
                  state_spec],
        out_specs=[pl.BlockSpec((GDN_ROWS, hps * dv), lambda b, h: (b, h)),
                   state_spec],
        out_shape=[jax.ShapeDtypeStruct((ts, n_vh * dv), BF16),
                   jax.ShapeDtypeStruct(state.shape, F32)],
        compiler_params=_params("parallel", "parallel"),
        name="gdn_sample",
    )(qkv, qkv, qkv, proj, g_a, beta_a, norm_g.reshape(1, dv), state)


def _router_kernel(x_ref, g_ref, w_ref, xn_ref, r_ref, *, n_groups, per_group):
    x = x_ref[...]
    xn = x * lax.rsqrt(jnp.mean(x * x, axis=-1, keepdims=True) + NORM_EPS) * g_ref[...]
    xn_ref[...] = xn.astype(xn_ref.dtype)
    logits = _dot3(xn, w_ref[...])
    lane = lax.broadcasted_iota(jnp.int32, logits.shape, 1).astype(F32)
    is_grp = lane < n_groups
    m = jnp.max(jnp.where(is_grp, logits, NEG_BIG), axis=-1, keepdims=True)
    e = jnp.where(is_grp, jnp.exp(jnp.where(is_grp, logits - m, 0.0)), 0.0)
    pg = e / jnp.sum(e, axis=-1, keepdims=True)
    pg_top = jnp.max(pg, axis=-1, keepdims=True)
    g_idx = jnp.min(jnp.where(is_grp & (pg == pg_top), lane, LANES), axis=-1, keepdims=True)
    elane = lane - n_groups
    in_grp = (elane >= g_idx * per_group) & (elane < (g_idx + 1.0) * per_group)
    m2 = jnp.max(jnp.where(in_grp, logits, NEG_BIG), axis=-1, keepdims=True)
    e2 = jnp.where(in_grp, jnp.exp(jnp.where(in_grp, logits - m2, 0.0)), 0.0)
    pe = e2 / jnp.sum(e2, axis=-1, keepdims=True)
    p1 = jnp.max(jnp.where(in_grp, pe, -1.0), axis=-1, keepdims=True)
    i1 = jnp.min(jnp.where(in_grp & (pe == p1), elane, LANES), axis=-1, keepdims=True)
    rest = in_grp & (elane != i1)
    p2 = jnp.max(jnp.where(rest, pe, -1.0), axis=-1, keepdims=True)
    i2 = jnp.min(jnp.where(rest & (pe == p2), elane, LANES), axis=-1, keepdims=True)
    denom = p1 + p2
    w1 = pg_top * p1 / denom
    w2 = pg_top * p2 / denom
    r_ref[...] = jnp.where(lane == 0, i1,
                           jnp.where(lane == 1, i2, jnp.where(lane == 2, w1, jnp.where(lane == 3, w2, 0.0))))


def moe_router(h, g, w_group, w_expert):
    t, d = h.shape
    n_groups = w_group.shape[1]
    n_exp = w_expert.shape[1]
    tm = _pick(t, 256)
    w_r = jnp.concatenate([w_group.astype(F32), w_expert.astype(F32),
                           jnp.zeros((d, LANES - n_groups - n_exp), F32)], axis=1)
    return pl.pallas_call(
        functools.partial(_router_kernel, n_groups=n_groups, per_group=n_exp // n_groups),
        grid=(t // tm,),
        in_specs=[pl.BlockSpec((tm, d), lambda i: (i, 0)),
                  pl.BlockSpec((1, d), lambda i: (0, 0)),
                  pl.BlockSpec((d, LANES), lambda i: (0, 0))],
        out_specs=[pl.BlockSpec((tm, d), lambda i: (i, 0)),
                   pl.BlockSpec((tm, LANES), lambda i: (i, 0))],
        out_shape=[jax.ShapeDtypeStruct((t, d), F32),
                   jax.ShapeDtypeStruct((t, LANES), F32)],
        compiler_params=_params("parallel"),
        name="moe_router",
    )(h, g.reshape(1, d), w_r)


def _row_gather_start(idx_ref, base, n_rows, src_hbm, dst, sem, inline=False):
    def body(r, carry):
        pltpu.make_async_copy(src_hbm.at[pl.ds(idx_ref[base + r], 1), :], dst.at[pl.ds(r, 1), :], sem).start()
        return carry
    if inline:
        for r in range(n_rows):
            body(r, 0)
    else:
        lax.fori_loop(0, n_rows, body, 0, unroll=8)


def _row_gather_wait(n_rows, src_hbm, dst, sem):
    pltpu.make_async_copy(src_hbm.at[pl.ds(0, n_rows), :], dst, sem).wait()


def _ffn_kernel(te_ref, nu_ref, rt_ref, x_hbm, w1_ref, w3_ref, w2_ref, o_ref, xbuf, sem, w1b, w3b, w2b):
    i = pl.program_id(0)
    n_used = nu_ref[0]

    @pl.when(i >= n_used)
    def _():
        o_ref[...] = jnp.zeros_like(o_ref)

    @pl.when(i < n_used)
    def _():
        slot = i % 2

        @pl.when(i == 0)
        def _():
            _row_gather_start(rt_ref, 0, MOE_TILE, x_hbm, xbuf.at[0], sem.at[0])

        prev = te_ref[jnp.maximum(i - 1, 0)]

        @pl.when((i == 0) | (te_ref[i] != prev))
        def _():
            w1b[...] = w1_ref[...].astype(BF16)
            w3b[...] = w3_ref[...].astype(BF16)
            w2b[...] = w2_ref[...].astype(BF16)

        _row_gather_wait(MOE_TILE, x_hbm, xbuf.at[slot], sem.at[slot])
        nxt = jnp.where(i + 1 < n_used, i + 1, 0)
        _row_gather_start(rt_ref, nxt * MOE_TILE, MOE_TILE, x_hbm, xbuf.at[1 - slot], sem.at[1 - slot], inline=True)
        x = xbuf[slot].astype(BF16)
        h1 = jnp.dot(x, w1b[...], preferred_element_type=F32)
        h3 = jnp.dot(x, w3b[...], preferred_element_type=F32)
        hdn = (_silu(h1) * h3).astype(BF16)
        o_ref[...] = jnp.dot(hdn, w2b[...], preferred_element_type=F32).astype(o_ref.dtype)

        @pl.when(i == n_used - 1)
        def _():
            _row_gather_wait(MOE_TILE, x_hbm, xbuf.at[1 - slot], sem.at[1 - slot])


def moe_ffn(xn, row_token, tile_expert, n_used, w1, w3, w2, layer):
    d = xn.shape[1]
    r = row_token.shape[0]
    ff = w1.shape[3]
    n_tiles = r // MOE_TILE
    grid_spec = pltpu.PrefetchScalarGridSpec(
        num_scalar_prefetch=3,
        grid=(n_tiles,),
        in_specs=[pl.BlockSpec(memory_space=pl.ANY),
                  pl.BlockSpec((None, None, d, ff), lambda i, te, nu, rt: (layer, te[i], 0, 0)),
                  pl.BlockSpec((None, None, d, ff), lambda i, te, nu, rt: (layer, te[i], 0, 0)),
                  pl.BlockSpec((None, None, ff, d), lambda i, te, nu, rt: (layer, te[i], 0, 0))],
        out_specs=pl.BlockSpec((MOE_TILE, d), lambda i, te, nu, rt: (i, 0)),
        scratch_shapes=[pltpu.VMEM((2, MOE_TILE, d), F32), pltpu.SemaphoreType.DMA((2,)),
                        pltpu.VMEM((d, ff), BF16), pltpu.VMEM((d, ff), BF16), pltpu.VMEM((ff, d), BF16)],
    )
    return pl.pallas_call(
        _ffn_kernel,
        grid_spec=grid_spec,
        out_shape=jax.ShapeDtypeStruct((r, d), F32),
        compiler_params=_params("arbitrary"),
        name="moe_ffn",
    )(tile_expert, n_used, row_token, xn, w1, w3, w2)


def _combine_kernel(dest_ref, h_ref, r_ref, y_hbm, o_ref, ybuf, sem, *, tm, n_tokens):
    i = pl.program_id(0)
    slot = i % 2

    def start(tile, to_slot):
        for k in range(MOE_TOP_K):
            _row_gather_start(dest_ref, k * n_tokens + tile * tm, tm, y_hbm, ybuf.at[to_slot, k], sem.at[to_slot])

    @pl.when(i == 0)
    def _():
        start(0, 0)

    @pl.when(i + 1 < pl.num_programs(0))
    def _():
        start(i + 1, 1 - slot)

    for k in range(MOE_TOP_K):
        _row_gather_wait(tm, y_hbm, ybuf.at[slot, k], sem.at[slot])
    r = r_ref[...]
    o_ref[...] = h_ref[...] + r[:, 2:3] * ybuf[slot, 0] + r[:, 3:4] * ybuf[slot, 1]


def moe_combine(h, y_sorted, dest, route):
    t, d = h.shape
    tm = _pick(t, 256)
    spec = pl.BlockSpec((tm, d), lambda i, dest: (i, 0))
    grid_spec = pltpu.PrefetchScalarGridSpec(
        num_scalar_prefetch=1,
        grid=(t // tm,),
        in_specs=[spec, pl.BlockSpec((tm, LANES), lambda i, dest: (i, 0)), pl.BlockSpec(memory_space=pl.ANY)],
        out_specs=spec,
        scratch_shapes=[pltpu.VMEM((2, MOE_TOP_K, tm, d), F32), pltpu.SemaphoreType.DMA((2,))],
    )
    return pl.pallas_call(
        functools.partial(_combine_kernel, tm=tm, n_tokens=t),
        grid_spec=grid_spec,
        out_shape=jax.ShapeDtypeStruct((t, d), F32),
        compiler_params=_params("arbitrary"),
        name="moe_combine",
    )(dest, h, route, y_sorted)


def hier_moe_layer(h, g, w_group, w_expert, w1, w3, w2, layer):
    t, d = h.shape
    n_exp = w1.shape[1]
    xn, route = moe_router(h, g, w_group, w_expert)
    e_flat = jnp.concatenate([route[:, 0], route[:, 1]]).astype(jnp.int32)
    onehot = (e_flat[:, None] == jnp.arange(n_exp, dtype=jnp.int32)[None, :]).astype(jnp.int32)
    counts = jnp.sum(onehot, axis=0)
    rank = jnp.sum((jnp.cumsum(onehot, axis=0) - onehot) * onehot, axis=1)
    padded = ((counts + MOE_TILE - 1) // MOE_TILE) * MOE_TILE
    ends = jnp.cumsum(padded)
    dest = jnp.sum(onehot * (ends - padded)[None, :], axis=1) + rank
    n_rows = MOE_TOP_K * t + n_exp * MOE_TILE
    n_rows = ((n_rows + MOE_TILE - 1) // MOE_TILE) * MOE_TILE
    token = jnp.concatenate([jnp.arange(t, dtype=jnp.int32)] * MOE_TOP_K)
    row_token = jnp.zeros((n_rows,), jnp.int32).at[dest].set(token)
    tile_start = jnp.arange(n_rows // MOE_TILE, dtype=jnp.int32) * MOE_TILE
    tile_expert = jnp.minimum(jnp.sum((ends[None, :] <= tile_start[:, None]).astype(jnp.int32), axis=1), n_exp - 1)
    n_used = (ends[-1] // MOE_TILE).astype(jnp.int32).reshape(1)
    y_sorted = moe_ffn(xn, row_token, tile_expert, n_used, w1, w3, w2, layer)
    return moe_combine(h, y_sorted, dest, route)


def _rope_tables(positions, half):
    inv = ROPE_THETA ** (-np.arange(half, dtype=np.float64) / half)
    ang = positions.astype(np.float64)[:, None] * inv[None, :]
    cos, sin = np.cos(ang), np.sin(ang)
    return (jnp.asarray(np.concatenate([cos, cos], axis=1), F32),
            jnp.asarray(np.concatenate([-sin, sin], axis=1), F32))


def _rope(x, cos2, sin2):
    half = x.shape[-1] // 2
    return x * cos2 + jnp.concatenate([x[:, half:], x[:, :half]], axis=-1) * sin2


def _kv_kernel(x_ref, w_ref, g_ref, cos_ref, sin_ref, ckv_ref, kpe_ref, wb_ref, *, rank):
    @pl.when(pl.program_id(0) == 0)
    def _():
        wb_ref[...] = w_ref[...].astype(BF16)

    c = jnp.dot(x_ref[...], wb_ref[...], preferred_element_type=F32)
    lat = c[:, :rank]
    ckv_ref[...] = lat * lax.rsqrt(jnp.mean(lat * lat, axis=-1, keepdims=True) + NORM_EPS) * g_ref[...]
    kpe_ref[...] = _rope(c[:, rank:], cos_ref[...], sin_ref[...])


def mla_shared_kv(xkv, w_dkv, latent_g, cos2, sin2, rank):
    t, d = xkv.shape
    rope_dim = w_dkv.shape[1] - rank
    tm = _pick(t, 512)
    return pl.pallas_call(
        functools.partial(_kv_kernel, rank=rank),
        grid=(t // tm,),
        in_specs=[pl.BlockSpec((tm, d), lambda i: (i, 0)),
                  pl.BlockSpec((d, rank + rope_dim), lambda i: (0, 0)),
                  pl.BlockSpec((1, rank), lambda i: (0, 0)),
                  pl.BlockSpec((tm, rope_dim), lambda i: (i, 0)),
                  pl.BlockSpec((tm, rope_dim), lambda i: (i, 0))],
        out_specs=[pl.BlockSpec((tm, rank), lambda i: (i, 0)),
                   pl.BlockSpec((tm, rope_dim), lambda i: (i, 0))],
        out_shape=[jax.ShapeDtypeStruct((t, rank), F32),
                   jax.ShapeDtypeStruct((t, rope_dim), F32)],
        scratch_shapes=[pltpu.VMEM((d, rank + rope_dim), BF16)],
        compiler_params=_params("arbitrary"),
        name="mla_shared_kv",
    )(xkv, w_dkv, latent_g.reshape(1, rank), cos2, sin2)


def _q_kernel(x_ref, w_ref, cos_ref, sin_ref, o_ref, *, nope, scale, n_prompt_blocks):
    sc = jnp.where(pl.program_id(0) < n_prompt_blocks, scale * LOG2E, scale)
    q = _dot(x_ref[...], w_ref[...])
    o_ref[:, :nope] = (q[:, :nope] * sc).astype(o_ref.dtype)
    o_ref[:, nope:] = (_rope(q[:, nope:], cos_ref[...], sin_ref[...]) * sc).astype(o_ref.dtype)


def mla_query(xn, w_q_heads, cos2, sin2, nope, scale, tp):
    t, d = xn.shape
    n_heads, _, qk = w_q_heads.shape
    rope_dim = qk - nope
    tm = _pick(math.gcd(tp, t - tp), 1024)
    return pl.pallas_call(
        functools.partial(_q_kernel, nope=nope, scale=scale, n_prompt_blocks=tp // tm),
        grid=(t // tm, n_heads),
        in_specs=[pl.BlockSpec((tm, d), lambda i, h: (i, 0)),
                  pl.BlockSpec((None, d, qk), lambda i, h: (h, 0, 0)),
                  pl.BlockSpec((tm, rope_dim), lambda i, h: (i, 0)),
                  pl.BlockSpec((tm, rope_dim), lambda i, h: (i, 0))],
        out_specs=pl.BlockSpec((None, tm, qk), lambda i, h: (h, i, 0)),
        out_shape=jax.ShapeDtypeStruct((n_heads, t, qk), BF16),
        compiler_params=_params("parallel", "arbitrary"),
        name="mla_query",
    )(xn, w_q_heads, cos2, sin2)


def _kv_expand_kernel(ckv_ref, kpe_ref, wk_ref, wv_ref, k_ref, v_ref, *, nope):
    ckv = ckv_ref[...]
    k_ref[:, :nope] = _dot(ckv, wk_ref[...]).astype(k_ref.dtype)
    k_ref[:, nope:] = kpe_ref[...].astype(k_ref.dtype)
    v_ref[...] = _dot(ckv, wv_ref[...]).astype(v_ref.dtype)


def mla_expand_kv(ckv, kpe, w_uk_heads, w_uv_heads, tp):
    rank = ckv.shape[1]
    rope_dim = kpe.shape[1]
    n_heads, _, nope = w_uk_heads.shape
    dv = w_uv_heads.shape[2]
    tm = _pick(tp, 1024)
    return pl.pallas_call(
        functools.partial(_kv_expand_kernel, nope=nope),
        grid=(tp // tm, n_heads),
        in_specs=[pl.BlockSpec((tm, rank), lambda i, h: (i, 0)),
                  pl.BlockSpec((tm, rope_dim), lambda i, h: (i, 0)),
                  pl.BlockSpec((None, rank, nope), lambda i, h: (h, 0, 0)),
                  pl.BlockSpec((None, rank, dv), lambda i, h: (h, 0, 0))],
        out_specs=[pl.BlockSpec((None, tm, nope + rope_dim), lambda i, h: (h, i, 0)),
                   pl.BlockSpec((None, tm, dv), lambda i, h: (h, i, 0))],
        out_shape=[jax.ShapeDtypeStruct((n_heads, tp, nope + rope_dim), BF16),
                   jax.ShapeDtypeStruct((n_heads, tp, dv), BF16)],
        compiler_params=_params("parallel", "arbitrary"),
        name="mla_expand_kv",
    )(ckv, kpe, w_uk_heads, w_uv_heads)


def _flash_kernel(qi_ref, kj_ref, q_ref, k_ref, v_ref, o_ref, m_scr, l_scr, acc_scr, *, tile):
    step = pl.program_id(1)
    qi = qi_ref[step]
    kj = kj_ref[step]

    @pl.when(kj == 0)
    def _():
        m_scr[...] = jnp.full_like(m_scr, NEG_BIG)
        l_scr[...] = jnp.zeros_like(l_scr)
        acc_scr[...] = jnp.zeros_like(acc_scr)

    s = lax.dot_general(q_ref[...], k_ref[...], (((1,), (1,)), ((), ())), preferred_element_type=F32)

    def update(s):
        m_prev = m_scr[...]
        m_new = jnp.maximum(m_prev, jnp.max(s, axis=-1, keepdims=True))
        alpha = jnp.exp2(m_prev - m_new)
        p = jnp.exp2(s - m_new)
        l_scr[...] = alpha * l_scr[...] + jnp.sum(p, axis=-1, keepdims=True)
        acc_scr[...] = alpha * acc_scr[...] + jnp.dot(p.astype(BF16), v_ref[...], preferred_element_type=F32)
        m_scr[...] = m_new

    @pl.when(kj < qi)
    def _():
        update(s)

    @pl.when(kj == qi)
    def _():
        row = lax.broadcasted_iota(jnp.int32, s.shape, 0)
        col = lax.broadcasted_iota(jnp.int32, s.shape, 1)
        update(jnp.where(col <= row, s, NEG_BIG))
        o_ref[...] = (acc_scr[...] / l_scr[...]).astype(o_ref.dtype)


def mla_prompt_attention(q_heads, k_heads, v_heads, tp):
    n_heads, _, qk = q_heads.shape
    dv = v_heads.shape[2]
    tile = _pick(tp, ATTN_TILE)
    n_blocks = tp // tile
    pairs = [(i, j) for i in range(n_blocks) for j in range(i + 1)]
    qi_list = jnp.asarray([p[0] for p in pairs], jnp.int32)
    kj_list = jnp.asarray([p[1] for p in pairs], jnp.int32)
    grid_spec = pltpu.PrefetchScalarGridSpec(
        num_scalar_prefetch=2,
        grid=(n_heads, len(pairs)),
        in_specs=[pl.BlockSpec((None, tile, qk), lambda h, s, qi, kj: (h, qi[s], 0)),
                  pl.BlockSpec((None, tile, qk), lambda h, s, qi, kj: (h, kj[s], 0)),
                  pl.BlockSpec((None, tile, dv), lambda h, s, qi, kj: (h, kj[s], 0))],
        out_specs=pl.BlockSpec((tile, dv), lambda h, s, qi, kj: (qi[s], h)),
        scratch_shapes=[pltpu.VMEM((tile, 1), F32), pltpu.VMEM((tile, 1), F32), pltpu.VMEM((tile, dv), F32)],
    )
    return pl.pallas_call(
        functools.partial(_flash_kernel, tile=tile),
        grid_spec=grid_spec,
        out_shape=jax.ShapeDtypeStruct((tp, n_heads * dv), BF16),
        compiler_params=_params("parallel", "arbitrary"),
        name="mla_prompt_attention",
    )(qi_list, kj_list, q_heads, k_heads, v_heads)


def _absorb_kernel(q_ref, wk_ref, o_ref, *, nope):
    q = q_ref[...]
    rank = wk_ref.shape[0]
    o_ref[:, :rank] = _dot_nt(q[:, :nope], wk_ref[...]).astype(o_ref.dtype)
    o_ref[:, rank:] = q[:, nope:]


def mla_absorb_query(q_heads, w_uk_heads, tp):
    n_heads, t, qk = q_heads.shape
    _, rank, nope = w_uk_heads.shape
    ts = t - tp
    tm = _pick(math.gcd(ts, tp), 1024)
    off = tp // tm
    return pl.pallas_call(
        functools.partial(_absorb_kernel, nope=nope),
        grid=(n_heads, ts // tm),
        in_specs=[pl.BlockSpec((None, tm, qk), lambda h, i: (h, off + i, 0)),
                  pl.BlockSpec((None, rank, nope), lambda h, i: (h, 0, 0))],
        out_specs=pl.BlockSpec((None, tm, rank + qk - nope), lambda h, i: (h, i, 0)),
        out_shape=jax.ShapeDtypeStruct((n_heads, ts, rank + qk - nope), BF16),
        compiler_params=_params("parallel", "parallel"),
        name="mla_absorb_query",
    )(q_heads, w_uk_heads)


def _paged_kernel(pt_ref, q_ref, cn_ref, kn_ref, ckv_hbm, kpe_hbm, o_ref, m_scr, l_scr, acc_scr, ckbuf, kpbuf, sem,
                  ck_scr, kp_scr, *, pages_per_step, rank, dec_seq, page, n_sub):
    s_idx = pl.program_id(0)
    j = pl.program_id(1)
    n_j = pl.num_programs(1)
    step = s_idx * n_j + j
    n_steps = pl.num_programs(0) * n_j
    slot = step % 2

    def page_copies(seq, blk, to_slot):
        copies = []
        for pg in range(pages_per_step):
            pid = pt_ref[seq, blk * pages_per_step + pg]
            copies.append(pltpu.make_async_copy(
                ckv_hbm.at[pid], ckbuf.at[to_slot, pl.ds(pg * page, page), :], sem.at[to_slot]))
            copies.append(pltpu.make_async_copy(
                kpe_hbm.at[pid], kpbuf.at[to_slot, :, pl.ds(pg * page, page)], sem.at[to_slot]))
        return copies

    @pl.when(step == 0)
    def _():
        for cp in page_copies(s_idx, j, slot):
            cp.start()

    for cp in page_copies(s_idx, j, slot):
        cp.wait()

    n_heads = q_ref.shape[0]
    q = q_ref[...].reshape(n_heads * dec_seq, q_ref.shape[2])
    q_lat, q_pe = q[:, :rank], q[:, rank:]

    @pl.when(j == 0)
    def _():
        pad = lambda a: jnp.concatenate([a, jnp.zeros((LANES - dec_seq, a.shape[1]), a.dtype)], axis=0)
        cn = pad(cn_ref[...]).astype(BF16)
        s = _dot_nt(q_lat, cn) + _dot_nt(q_pe, pad(kn_ref[...]))
        qtok = lax.broadcasted_iota(jnp.int32, s.shape, 0) & (dec_seq - 1)
        ktok = lax.broadcasted_iota(jnp.int32, s.shape, 1)
        s = jnp.where(ktok <= qtok, s, NEG_BIG)
        m0 = jnp.max(s, axis=-1, keepdims=True)
        p = jnp.exp(s - m0)
        m_scr[...] = m0
        l_scr[...] = jnp.sum(p, axis=-1, keepdims=True)
        acc_scr[...] = jnp.dot(p.astype(BF16), cn, preferred_element_type=F32)

    nxt = jnp.where(step + 1 < n_steps, step + 1, 0)
    for cp in page_copies(nxt // n_j, nxt % n_j, 1 - slot):
        cp.start()

    for pg in range(pages_per_step):
        ck_scr[pg * page:(pg + 1) * page, :] = ckbuf[slot, pg * page:(pg + 1) * page, :].astype(BF16)
    kp_scr[...] = kpbuf[slot].astype(BF16)

    sub = pages_per_step * page // n_sub
    cks = [ck_scr[i * sub:(i + 1) * sub, :] for i in range(n_sub)]
    ss = [lax.dot_general(q_lat, cks[i], (((1,), (1,)), ((), ())), preferred_element_type=F32)
          + jnp.dot(q_pe, kp_scr[:, i * sub:(i + 1) * sub], preferred_element_type=F32) for i in range(n_sub)]
    ms = [jnp.max(s, axis=-1, keepdims=True) for s in ss]
    ps = [jnp.exp(s - m) for s, m in zip(ss, ms)]
    ls = [jnp.sum(p, axis=-1, keepdims=True) for p in ps]
    accs = [jnp.dot(p.astype(BF16), ck, preferred_element_type=F32) for p, ck in zip(ps, cks)]
    m_prev = m_scr[...]
    m_new = functools.reduce(jnp.maximum, ms, m_prev)
    alpha = jnp.exp(m_prev - m_new)
    l_new = alpha * l_scr[...]
    acc_new = alpha * acc_scr[...]
    for m, l, acc in zip(ms, ls, accs):
        coef = jnp.exp(m - m_new)
        l_new = l_new + coef * l
        acc_new = acc_new + coef * acc
    m_scr[...] = m_new
    l_scr[...] = l_new
    acc_scr[...] = acc_new

    @pl.when(j == n_j - 1)
    def _():
        o_ref[...] = (acc_new / l_new).reshape(o_ref.shape).astype(o_ref.dtype)

    @pl.when(step == n_steps - 1)
    def _():
        for cp in page_copies(0, 0, 1 - slot):
            cp.wait()


def mla_paged_attention(q_abs, ckv_new, kpe_new, cache_ckv, cache_kpe_t, page_table, dec_seq):
    n_heads, ts, key_dim = q_abs.shape
    n_seq, n_pages = page_table.shape
    _, page, rank = cache_ckv.shape
    rope_dim = cache_kpe_t.shape[1]
    pps = _divisor(n_pages, PAGES_PER_STEP)
    rows = n_heads * dec_seq
    keys = pps * page
    grid_spec = pltpu.PrefetchScalarGridSpec(
        num_scalar_prefetch=1,
        grid=(n_seq, n_pages // pps),
        in_specs=[pl.BlockSpec((n_heads, dec_seq, key_dim), lambda s, j, pt: (0, s, 0)),
                  pl.BlockSpec((dec_seq, rank), lambda s, j, pt: (s, 0)),
                  pl.BlockSpec((dec_seq, rope_dim), lambda s, j, pt: (s, 0)),
                  pl.BlockSpec(memory_space=pl.ANY),
                  pl.BlockSpec(memory_space=pl.ANY)],
        out_specs=pl.BlockSpec((n_heads, dec_seq, rank), lambda s, j, pt: (0, s, 0)),
        scratch_shapes=[pltpu.VMEM((rows, 1), F32), pltpu.VMEM((rows, 1), F32), pltpu.VMEM((rows, rank), F32),
                        pltpu.VMEM((2, keys, rank), F32), pltpu.VMEM((2, rope_dim, keys), F32),
                        pltpu.SemaphoreType.DMA((2,)),
                        pltpu.VMEM((keys, rank), BF16), pltpu.VMEM((rope_dim, keys), BF16)],
    )
    return pl.pallas_call(
        functools.partial(_paged_kernel, pages_per_step=pps, rank=rank, dec_seq=dec_seq, page=page,
                          n_sub=_divisor(pps, PAGED_SUB_BLOCKS)),
        grid_spec=grid_spec,
        out_shape=jax.ShapeDtypeStruct((n_heads, ts, rank), BF16),
        compiler_params=_params("arbitrary", "arbitrary"),
        name="mla_paged_attention",
    )(page_table, q_abs, ckv_new, kpe_new, cache_ckv, cache_kpe_t)


def _unabsorb_kernel(o_ref, wv_ref, out_ref):
    out_ref[...] = _dot(o_ref[...], wv_ref[...]).astype(out_ref.dtype)


def mla_unabsorb(o_lat, w_uv_heads):
    n_heads, ts, rank = o_lat.shape
    dv = w_uv_heads.shape[2]
    tm = _pick(ts, 1024)
    return pl.pallas_call(
        _unabsorb_kernel,
        grid=(n_heads, ts // tm),
        in_specs=[pl.BlockSpec((None, tm, rank), lambda h, i: (h, i, 0)),
                  pl.BlockSpec((None, rank, dv), lambda h, i: (h, 0, 0))],
        out_specs=pl.BlockSpec((tm, dv), lambda h, i: (i, h)),
        out_shape=jax.ShapeDtypeStruct((ts, n_heads * dv), BF16),
        compiler_params=_params("parallel", "parallel"),
        name="mla_unabsorb",
    )(o_lat, w_uv_heads)


def kernel(x_prompt, x_sample, state_ssm, state_conv, cache_ckv, cache_kpe, page_table, norm_mix, norm_ffn,
           norm_final, gdn_w_in, gdn_conv_w, gdn_a_log, gdn_dt_bias, gdn_norm, gdn_w_out, kv_norm, w_dkv,
           kv_latent_norm, w_uk, w_uv, mla_w_q, mla_w_o, moe_w_group, moe_w_expert, moe_w1, moe_w3, moe_w2):
    batch, seq, d = x_prompt.shape
    n_seq, dec_seq, _ = x_sample.shape
    assert batch == 1 and dec_seq == SUBLANES
    tp, ts = seq, n_seq * dec_seq
    depth = norm_mix.shape[0]
    n_a = state_ssm.shape[0]
    n_vh, dk, dv = state_ssm.shape[2:]
    conv_dim = state_conv.shape[3]
    qk_dim = (conv_dim - n_vh * dv) // 2
    n_kh = qk_dim // dk
    n_taps = gdn_conv_w.shape[1]
    rank = kv_latent_norm.shape[0]
    rope_dim = w_dkv.shape[1] - rank
    n_heads, nope = w_uk.shape[1:]
    past_len = page_table.shape[1] * cache_ckv.shape[1]
    scale = float((nope + rope_dim) ** -0.5)

    h = jnp.concatenate([x_prompt.reshape(tp, d), x_sample.reshape(ts, d)], axis=0)
    positions = np.concatenate([np.arange(tp), past_len + np.arange(dec_seq)[None, :].repeat(n_seq, 0).reshape(-1)])
    cos2, sin2 = _rope_tables(positions, rope_dim // 2)

    ssm_p, conv_p, ssm_s, conv_s = [], [], [], []
    ckv = kpe = None
    for i in range(depth):
        if i < n_a:
            xn = rmsnorm(h, norm_mix[i], BF16)
            w_in = gdn_w_in[i]
            main_cols = conv_dim + n_vh * dv
            proj = matmul(xn, w_in, F32, n_cols=main_cols)
            gates = gdn_gates(xn, w_in[:, main_cols:], gdn_a_log[i], gdn_dt_bias[i])
            hist = jnp.pad(state_conv[i], ((0, 0), (SUBLANES - (n_taps - 1), 0), (0, 0)))
            qkv = gdn_conv(proj, hist, gdn_conv_w[i], tp, qk_dim, conv_dim, dk)
            gates_t = gates.T
            o_p, s_p = gdn_prompt(qkv, proj, gates_t, gdn_norm[i], tp, n_kh, n_vh, dk, dv, GDN_CHUNK)
            o_s, s_s = gdn_sample(qkv, proj, gates_t, gdn_norm[i], state_ssm[i], tp, n_kh, n_vh, dk, dv, dec_seq)
            h = matmul(jnp.concatenate([o_p, o_s], axis=0), gdn_w_out[i], F32, res=h, tn_pref=512)
            ssm_p.append(s_p[None])
            ssm_s.append(s_s)
            conv_p.append(proj[tp - (n_taps - 1):tp, :conv_dim][None])
            conv_s.append(proj[tp:, :conv_dim].reshape(n_seq, dec_seq, conv_dim)[:, dec_seq - (n_taps - 1):])
        else:
            if i == n_a:
                ckv, kpe = mla_shared_kv(rmsnorm(h, kv_norm, BF16), w_dkv, kv_latent_norm, cos2, sin2, rank)
                w_uk_heads = jnp.transpose(w_uk, (1, 0, 2))
                w_uv_heads = jnp.transpose(w_uv, (1, 0, 2))
                k_heads, v_heads = mla_expand_kv(ckv, kpe, w_uk_heads, w_uv_heads, tp)
                cache_kpe_t = jnp.swapaxes(cache_kpe, 1, 2)
            j = i - n_a
            xn = rmsnorm(h, norm_mix[i], BF16)
            w_q_heads = jnp.transpose(mla_w_q[j].reshape(d, n_heads, nope + rope_dim), (1, 0, 2))
            q_heads = mla_query(xn, w_q_heads, cos2, sin2, nope, scale, tp)
            o_p = mla_prompt_attention(q_heads, k_heads, v_heads, tp)
            q_abs = mla_absorb_query(q_heads, w_uk_heads, tp)
            o_lat = mla_paged_attention(q_abs, ckv[tp:], kpe[tp:], cache_ckv, cache_kpe_t, page_table, dec_seq)
            o_s = mla_unabsorb(o_lat, w_uv_heads)
            h = matmul(jnp.concatenate([o_p, o_s], axis=0), mla_w_o[j], F32, res=h)
        h = hier_moe_layer(h, norm_ffn[i], moe_w_group[i], moe_w_expert[i], moe_w1, moe_w3, moe_w2, i)

    y_prompt = rmsnorm(h, norm_final, F32, 0, tp).reshape(batch, seq, d)
    y_sample = rmsnorm(h, norm_final, F32, tp, ts).reshape(n_seq, dec_seq, d)
    return (y_prompt, y_sample,
            jnp.stack(ssm_p), jnp.stack(conv_p),
            ckv[:tp].reshape(batch, seq, rank), kpe[:tp].reshape(batch, seq, rope_dim),
            jnp.stack(ssm_s), jnp.stack(conv_s),
            ckv[tp:].reshape(n_seq, dec_seq, rank), kpe[tp:].reshape(n_seq, dec_seq, rope_dim))
```

```python
import functools
import math

import numpy as np
import jax
import jax.numpy as jnp
from jax import lax
from jax.experimental import pallas as pl
from jax.experimental.pallas import tpu as pltpu

F32 = jnp.float32
BF16 = jnp.bfloat16

NORM_EPS = 1e-6
ROPE_THETA = 10000.0
GDN_CHUNK = 64
MOE_TOP_K = 2
SUBLANES = 8
LANES = 128
VMEM_LIMIT_BYTES = 56 * 1024 * 1024
GDN_ROWS = 128
GDN_PROMPT_HEADS = 8
GDN_SAMPLE_HEADS = 4
MOE_TILE = 256
ATTN_TILE = 1024
PAGES_PER_STEP = 32
PAGED_SUB_BLOCKS = 4
NEG_BIG = -1e30
LOG2E = math.log2(math.e)


def _params(*sem):
    return pltpu.CompilerParams(dimension_semantics=sem, vmem_limit_bytes=VMEM_LIMIT_BYTES)


def _pick(n, pref):
    t = min(pref, n)
    while n % t:
        t -= SUBLANES
    return t


def _divisor(n, pref):
    t = min(pref, n)
    while n % t:
        t -= 1
    return t


def _dot(a, b):
    return jnp.dot(a.astype(BF16), b.astype(BF16), preferred_element_type=F32)


def _dot_nt(a, b):
    return lax.dot_general(a.astype(BF16), b.astype(BF16), (((1,), (1,)), ((), ())),
                           preferred_element_type=F32)


def _split(a):
    hi = a.astype(BF16)
    lo = (a - hi.astype(F32)).astype(BF16)
    return hi, lo


def _dot3(a, b):
    ah, al = _split(a)
    bh, bl = _split(b)
    d = lambda x, y: jnp.dot(x, y, preferred_element_type=F32)
    return d(ah, bh) + (d(ah, bl) + d(al, bh))


def _sigmoid(x):
    return 1.0 / (1.0 + jnp.exp(-x))


def _silu(x):
    return x * _sigmoid(x)


def _rmsnorm_kernel(x_ref, g_ref, o_ref):
    x = x_ref[...]
    y = x * lax.rsqrt(jnp.mean(x * x, axis=-1, keepdims=True) + NORM_EPS) * g_ref[...]
    o_ref[...] = y.astype(o_ref.dtype)


def rmsnorm(x, g, out_dtype, row_start=0, rows=None):
    t, d = x.shape
    rows = t if rows is None else rows
    tm = _pick(math.gcd(rows, row_start) if row_start else rows, 512)
    off = row_start // tm
    return pl.pallas_call(
        _rmsnorm_kernel,
        grid=(rows // tm,),
        in_specs=[pl.BlockSpec((tm, d), lambda i: (i + off, 0)),
                  pl.BlockSpec((1, d), lambda i: (0, 0))],
        out_specs=pl.BlockSpec((tm, d), lambda i: (i, 0)),
        out_shape=jax.ShapeDtypeStruct((rows, d), out_dtype),
        compiler_params=_params("parallel"),
        name="rmsnorm",
    )(x, g.reshape(1, d))


def _matmul_kernel(*refs, has_res):
    if has_res:
        x_ref, w_ref, r_ref, o_ref, wb_ref = refs
    else:
        x_ref, w_ref, o_ref, wb_ref = refs

    @pl.when(pl.program_id(1) == 0)
    def _():
        wb_ref[...] = w_ref[...].astype(BF16)

    acc = jnp.dot(x_ref[...], wb_ref[...], preferred_element_type=F32)
    if has_res:
        acc = acc + r_ref[...]
    o_ref[...] = acc.astype(o_ref.dtype)


def matmul(x, w, out_dtype, n_cols=None, res=None, tm_pref=1024, tn_pref=1024):
    m, k = x.shape
    n = w.shape[1] if n_cols is None else n_cols
    tm = _pick(m, tm_pref)
    tn = _pick(n, tn_pref)
    in_specs = [pl.BlockSpec((tm, k), lambda j, i: (i, 0)),
                pl.BlockSpec((k, tn), lambda j, i: (0, j))]
    args = [x, w]
    if res is not None:
        in_specs.append(pl.BlockSpec((tm, tn), lambda j, i: (i, j)))
        args.append(res)
    return pl.pallas_call(
        functools.partial(_matmul_kernel, has_res=res is not None),
        grid=(n // tn, m // tm),
        in_specs=in_specs,
        out_specs=pl.BlockSpec((tm, tn), lambda j, i: (i, j)),
        out_shape=jax.ShapeDtypeStruct((m, n), out_dtype),
        scratch_shapes=[pltpu.VMEM((k, tn), BF16)],
        compiler_params=_params("parallel", "arbitrary"),
        name="matmul",
    )(*args)


def _gates_kernel(x_ref, w_ref, a_ref, dt_ref, o_ref, *, n_heads):
    ba = _dot(x_ref[...], w_ref[...])
    lane = lax.broadcasted_iota(jnp.int32, ba.shape, 1)
    beta = _sigmoid(ba)
    xs = ba + dt_ref[...]
    softplus = jnp.maximum(xs, 0.0) + jnp.log(1.0 + jnp.exp(-jnp.abs(xs)))
    g = -jnp.exp(a_ref[...]) * softplus
    o_ref[...] = jnp.where(lane < n_heads, beta, g)


def gdn_gates(xn, w_ba, a_log, dt_bias):
    t, d = xn.shape
    hv = a_log.shape[0]
    tm = _pick(t, 512)
    zeros = jnp.zeros((hv,), F32)
    a_pad = jnp.concatenate([zeros, a_log.astype(F32)]).reshape(1, 2 * hv)
    dt_pad = jnp.concatenate([zeros, dt_bias.astype(F32)]).reshape(1, 2 * hv)
    return pl.pallas_call(
        functools.partial(_gates_kernel, n_heads=hv),
        grid=(t // tm,),
        in_specs=[pl.BlockSpec((tm, d), lambda i: (i, 0)),
                  pl.BlockSpec((d, 2 * hv), lambda i: (0, 0)),
                  pl.BlockSpec((1, 2 * hv), lambda i: (0, 0)),
                  pl.BlockSpec((1, 2 * hv), lambda i: (0, 0))],
        out_specs=pl.BlockSpec((tm, 2 * hv), lambda i: (i, 0)),
        out_shape=jax.ShapeDtypeStruct((t, 2 * hv), F32),
        compiler_params=_params("parallel"),
        name="gdn_gates",
    )(xn, w_ba, a_pad, dt_pad)


def _conv_kernel(x_ref, prev_ref, st_ref, w_ref, o_ref, *, n_prompt_blocks, n_q_blocks, n_qk_blocks,
                 n_taps, head_dim):
    i = pl.program_id(0)
    j = pl.program_id(1)
    x3 = x_ref[...]
    prev = jnp.where(i == 0, 0.0, prev_ref[...])
    hist_prompt = jnp.concatenate([prev, x3[:-1]], axis=0)
    hist = jnp.where(i >= n_prompt_blocks, st_ref[...], hist_prompt)
    row = lax.broadcasted_iota(jnp.int32, x3.shape, 1)
    tap = lambda s: w_ref[n_taps - 1 - s:n_taps - s, :][None]
    acc = x3 * tap(0)
    for s in range(1, n_taps):
        shifted = jnp.where(row >= s, pltpu.roll(x3, s, 1), pltpu.roll(hist, s, 1))
        acc = acc + shifted * tap(s)
    act = _silu(acc).reshape(x3.shape[0] * SUBLANES, x3.shape[2])
    scale = jnp.where(j < n_q_blocks, head_dim ** -0.5, 1.0)
    is_qk = j < n_qk_blocks
    for c in range(act.shape[1] // head_dim):
        seg = act[:, c * head_dim:(c + 1) * head_dim]
        nrm = seg * (lax.rsqrt(jnp.sum(seg * seg, axis=-1, keepdims=True) + NORM_EPS) * scale)
        o_ref[:, c * head_dim:(c + 1) * head_dim] = jnp.where(is_qk, nrm, seg)


def gdn_conv(proj, state_hist, conv_w, tp, qk_dim, conv_dim, head_dim):
    t = proj.shape[0]
    n_taps = conv_w.shape[0]
    tm = _pick(math.gcd(tp, t - tp), 512)
    tc = _pick(qk_dim, 1024)
    tiles = tm // SUBLANES
    proj3 = proj.reshape(t // SUBLANES, SUBLANES, proj.shape[1])
    npb = tp // tm
    return pl.pallas_call(
        functools.partial(_conv_kernel, n_prompt_blocks=npb, n_q_blocks=qk_dim // tc,
                          n_qk_blocks=2 * qk_dim // tc, n_taps=n_taps, head_dim=head_dim),
        grid=(t // tm, conv_dim // tc),
        in_specs=[pl.BlockSpec((tiles, SUBLANES, tc), lambda i, j: (i, 0, j)),
                  pl.BlockSpec((1, SUBLANES, tc), lambda i, j: (jnp.maximum(i * tiles - 1, 0), 0, j)),
                  pl.BlockSpec((tiles, SUBLANES, tc), lambda i, j: (jnp.maximum(i - npb, 0), 0, j)),
                  pl.BlockSpec((n_taps, tc), lambda i, j: (0, j))],
        out_specs=pl.BlockSpec((tm, tc), lambda i, j: (i, j)),
        out_shape=jax.ShapeDtypeStruct((t, conv_dim), F32),
        compiler_params=_params("parallel", "parallel"),
        name="gdn_conv",
    )(proj3, proj3, state_hist, conv_w)


def _gdn_blocks(qs, ks, qk_of, vs, zs, g_rows, b_rows, norm_g, states, chunk):
    r = GDN_ROWS
    n_groups = r // chunk
    units = range(len(vs))
    ri = lax.broadcasted_iota(jnp.int32, (r, r), 0)
    ci = lax.broadcasted_iota(jnp.int32, (r, r), 1)
    shift = int(math.log2(chunk))
    rg = ri >> shift
    cg = ci >> shift
    same = rg == cg
    causal = same & (ri >= ci)
    strict = same & (ri > ci)
    eye = ri == ci
    last = ci == (rg << shift) + (chunk - 1)
    eye_f = jnp.where(eye, 1.0, 0.0)
    bcast = lambda a: jnp.broadcast_to(a, (r, r))

    gc_col = [jnp.sum(jnp.where(causal, bcast(g_rows[u]), 0.0), axis=1, keepdims=True) for u in units]
    gc_rb = [bcast(jnp.sum(jnp.where(eye, bcast(gc_col[u]), 0.0), axis=0, keepdims=True)) for u in units]
    b_col = [jnp.sum(jnp.where(eye, bcast(b_rows[u]), 0.0), axis=1, keepdims=True) for u in units]
    glast_col = [jnp.sum(jnp.where(last, gc_rb[u], 0.0), axis=1, keepdims=True) for u in units]
    decay = [jnp.where(causal, jnp.exp(jnp.where(causal, gc_col[u] - gc_rb[u], 0.0)), 0.0) for u in units]

    kk = [_dot_nt(k, k) for k in ks]
    qk = [_dot_nt(q, k) for q, k in zip(qs, ks)]
    lmat = [jnp.where(strict, b_col[u] * kk[qk_of[u]] * decay[u], 0.0) for u in units]
    tmat = [eye_f - lmat[u] for u in units]
    pw = lmat
    for _ in range(shift - 1):
        pw = [_dot(pw[u], pw[u]) for u in units]
        tmat = [tmat[u] + _dot(tmat[u], pw[u]) for u in units]

    eg_col = [jnp.exp(gc_col[u]) for u in units]
    uu = [_dot(tmat[u], vs[u] * b_col[u]) for u in units]
    ww = [_dot(tmat[u], ks[qk_of[u]] * (b_col[u] * eg_col[u])) for u in units]
    qg = [qs[qk_of[u]] * eg_col[u] for u in units]
    amat = [jnp.where(causal, qk[qk_of[u]] * decay[u], 0.0) for u in units]

    ws_qs = [[_dot(jnp.concatenate([ww[u][gi * chunk:(gi + 1) * chunk], qg[u][gi * chunk:(gi + 1) * chunk]], axis=0),
                   states[u][gi]) for gi in range(n_groups)] for u in units]
    vn = [uu[u] - jnp.concatenate([ws_qs[u][gi][:chunk] for gi in range(n_groups)], axis=0) for u in units]
    o = [jnp.concatenate([ws_qs[u][gi][chunk:] for gi in range(n_groups)], axis=0) + _dot(amat[u], vn[u])
         for u in units]

    kd_t = [(ks[qk_of[u]] * jnp.exp(glast_col[u] - gc_col[u])).T for u in units]
    new_states = [[states[u][gi] * jnp.exp(glast_col[u][gi * chunk:gi * chunk + 1, :])
                   + _dot(jnp.where(cg == gi, kd_t[u], 0.0), vn[u]) for gi in range(n_groups)] for u in units]

    outs = [o[u] * lax.rsqrt(jnp.mean(o[u] * o[u], axis=-1, keepdims=True) + NORM_EPS) * norm_g * _silu(zs[u])
            for u in units]
    return outs, new_states


def _gdn_prompt_kernel(q_ref, k_ref, v_ref, z_ref, g_ref, b_ref, ng_ref, o_ref, sfin_ref, s_scr, *,
                       heads_per_step, chunk, dv):
    c = pl.program_id(1)
    rep = GDN_ROWS // chunk

    @pl.when(c == 0)
    def _():
        s_scr[...] = jnp.zeros_like(s_scr)

    heads = range(heads_per_step)
    tile = lambda ref, hh: jnp.concatenate([ref[:, hh * LANES:(hh + 1) * LANES]] * rep, axis=0)
    stack = lambda ref, hh: jnp.concatenate(
        [ref[:, (hh * rep + e) * dv:(hh * rep + e + 1) * dv] for e in range(rep)], axis=0)
    outs, new_states = _gdn_blocks(
        [tile(q_ref, hh) for hh in heads], [tile(k_ref, hh) for hh in heads], list(heads),
        [stack(v_ref, hh) for hh in heads], [stack(z_ref, hh) for hh in heads],
        [g_ref[hh] for hh in heads], [b_ref[hh] for hh in heads], ng_ref[...],
        [[s_scr[hh * rep + e] for e in range(rep)] for hh in heads], chunk)
    for hh in heads:
        for e in range(rep):
            s_scr[hh * rep + e] = new_states[hh][e]
            o_ref[:, (hh * rep + e) * dv:(hh * rep + e + 1) * dv] = (
                outs[hh][e * chunk:(e + 1) * chunk].astype(o_ref.dtype))

    @pl.when(c == pl.num_programs(1) - 1)
    def _():
        sfin_ref[...] = s_scr[...]


def gdn_prompt(qkv, proj, gates_t, norm_g, tp, n_kh, n_vh, dk, dv, chunk):
    rep = n_vh // n_kh
    assert rep * chunk == GDN_ROWS and dk == LANES and dv == LANES
    hps = _divisor(n_kh, GDN_PROMPT_HEADS)
    n_chunks = tp // chunk
    qk_blocks = n_kh // hps
    conv_dim = 2 * n_kh * dk + n_vh * dv

    def arrange(gt):
        a = gt[:, :tp].reshape(qk_blocks, hps, rep, n_chunks, chunk)
        return a.transpose(0, 3, 1, 2, 4).reshape(qk_blocks, n_chunks, hps, 1, GDN_ROWS)

    beta_a = arrange(gates_t[:n_vh])
    g_a = arrange(gates_t[n_vh:])
    vw = hps * rep * dv
    v_off = 2 * n_kh * dk // vw
    z_off = conv_dim // vw
    gate_spec = pl.BlockSpec((None, None, hps, 1, GDN_ROWS), lambda h, c: (h, c, 0, 0, 0))
    return pl.pallas_call(
        functools.partial(_gdn_prompt_kernel, heads_per_step=hps, chunk=chunk, dv=dv),
        grid=(qk_blocks, n_chunks),
        in_specs=[pl.BlockSpec((chunk, hps * dk), lambda h, c: (c, h)),
                  pl.BlockSpec((chunk, hps * dk), lambda h, c: (c, qk_blocks + h)),
                  pl.BlockSpec((chunk, vw), lambda h, c: (c, v_off + h)),
                  pl.BlockSpec((chunk, vw), lambda h, c: (c, z_off + h)),
                  gate_spec, gate_spec,
                  pl.BlockSpec((1, dv), lambda h, c: (0, 0))],
        out_specs=[pl.BlockSpec((chunk, vw), lambda h, c: (c, h)),
                   pl.BlockSpec((hps * rep, dk, dv), lambda h, c: (h, 0, 0))],
        out_shape=[jax.ShapeDtypeStruct((tp, n_vh * dv), BF16),
                   jax.ShapeDtypeStruct((n_vh, dk, dv), F32)],
        scratch_shapes=[pltpu.VMEM((hps * rep, dk, dv), F32)],
        compiler_params=_params("parallel", "arbitrary"),
        name="gdn_prompt",
    )(qkv, qkv, qkv, proj, g_a, beta_a, norm_g.reshape(1, dv))


def _gdn_sample_kernel(q_ref, k_ref, v_ref, z_ref, g_ref, b_ref, ng_ref, s_ref, o_ref, snew_ref, *,
                       heads_per_step, rep, chunk, dv):
    n_groups = GDN_ROWS // chunk
    n_qk = heads_per_step // rep
    heads = range(heads_per_step)
    cols = lambda ref, n, w: [ref[:, i * w:(i + 1) * w] for i in range(n)]
    outs, new_states = _gdn_blocks(
        cols(q_ref, n_qk, LANES), cols(k_ref, n_qk, LANES), [hh // rep for hh in heads],
        cols(v_ref, heads_per_step, dv), cols(z_ref, heads_per_step, dv),
        [g_ref[hh] for hh in heads], [b_ref[hh] for hh in heads], ng_ref[...],
        [[s_ref[gi, hh] for gi in range(n_groups)] for hh in heads], chunk)
    for hh in heads:
        o_ref[:, hh * dv:(hh + 1) * dv] = outs[hh].astype(o_ref.dtype)
        for gi in range(n_groups):
            snew_ref[gi, hh] = new_states[hh][gi]


def gdn_sample(qkv, proj, gates_t, norm_g, state, tp, n_kh, n_vh, dk, dv, chunk):
    rep = n_vh // n_kh
    t = qkv.shape[0]
    ts = t - tp
    assert dk == LANES and dv == LANES and ts % GDN_ROWS == 0 and tp % GDN_ROWS == 0
    hps = rep * _divisor(n_kh, GDN_SAMPLE_HEADS // rep)
    n_blocks = ts // GDN_ROWS
    spb = GDN_ROWS // chunk
    row_off = tp // GDN_ROWS
    conv_dim = 2 * n_kh * dk + n_vh * dv
    n_qk = hps // rep

    def arrange(gt):
        return gt[:, tp:].reshape(n_vh // hps, hps, n_blocks, 1, GDN_ROWS).transpose(0, 2, 1, 3, 4)

    beta_a = arrange(gates_t[:n_vh])
    g_a = arrange(gates_t[n_vh:])
    gate_spec = pl.BlockSpec((None, None, hps, 1, GDN_ROWS), lambda b, h: (h, b, 0, 0, 0))
    state_spec = pl.BlockSpec((spb, hps, dk, dv), lambda b, h: (b, h, 0, 0))
    return pl.pallas_call(
        functools.partial(_gdn_sample_kernel, heads_per_step=hps, rep=rep, chunk=chunk, dv=dv),
        grid=(n_blocks, n_vh // hps),
        in_specs=[pl.BlockSpec((GDN_ROWS, n_qk * dk), lambda b, h: (row_off + b, h)),
                  pl.BlockSpec((GDN_ROWS, n_qk * dk), lambda b, h: (row_off + b, n_kh // n_qk + h)),
                  pl.BlockSpec((GDN_ROWS, hps * dv), lambda b, h: (row_off + b, 2 * n_kh * dk // (hps * dv) + h)),
                  pl.BlockSpec((GDN_ROWS, hps * dv), lambda b, h: (row_off + b, conv_dim // (hps * dv) + h)),
                  gate_spec, gate_spec,
                  pl.BlockSpec((1, dv), lambda b, h: (0, 0)),
                  state_spec],
        out_specs=[pl.BlockSpec((GDN_ROWS, hps * dv), lambda b, h: (b, h)),
                   state_spec],
        out_shape=[jax.ShapeDtypeStruct((ts, n_vh * dv), BF16),
                   jax.ShapeDtypeStruct(state.shape, F32)],
        compiler_params=_params("parallel", "parallel"),
        name="gdn_sample",
    )(qkv, qkv, qkv, proj, g_a, beta_a, norm_g.reshape(1, dv), state)


def _store_row_chunks(ref, x):
    rows, d = x.shape
    n_chunks = d // LANES
    for k in range(n_chunks):
        ref[pl.ds(k, rows, stride=n_chunks), :] = x[:, k * LANES:(k + 1) * LANES].astype(ref.dtype)


def _load_row_chunks(ref, rows, d):
    n_chunks = d // LANES
    return jnp.concatenate([ref[pl.ds(k, rows, stride=n_chunks), :] for k in range(n_chunks)], axis=1)


def _router_kernel(x_ref, g_ref, w_ref, xn_ref, r_ref, *, n_groups, per_group):
    x = x_ref[...]
    xn = x * lax.rsqrt(jnp.mean(x * x, axis=-1, keepdims=True) + NORM_EPS) * g_ref[...]
    _store_row_chunks(xn_ref, xn)
    logits = _dot3(xn, w_ref[...])
    lane = lax.broadcasted_iota(jnp.int32, logits.shape, 1).astype(F32)
    is_grp = lane < n_groups
    m = jnp.max(jnp.where(is_grp, logits, NEG_BIG), axis=-1, keepdims=True)
    e = jnp.where(is_grp, jnp.exp(jnp.where(is_grp, logits - m, 0.0)), 0.0)
    pg = e / jnp.sum(e, axis=-1, keepdims=True)
    pg_top = jnp.max(pg, axis=-1, keepdims=True)
    g_idx = jnp.min(jnp.where(is_grp & (pg == pg_top), lane, LANES), axis=-1, keepdims=True)
    elane = lane - n_groups
    in_grp = (elane >= g_idx * per_group) & (elane < (g_idx + 1.0) * per_group)
    m2 = jnp.max(jnp.where(in_grp, logits, NEG_BIG), axis=-1, keepdims=True)
    e2 = jnp.where(in_grp, jnp.exp(jnp.where(in_grp, logits - m2, 0.0)), 0.0)
    pe = e2 / jnp.sum(e2, axis=-1, keepdims=True)
    p1 = jnp.max(jnp.where(in_grp, pe, -1.0), axis=-1, keepdims=True)
    i1 = jnp.min(jnp.where(in_grp & (pe == p1), elane, LANES), axis=-1, keepdims=True)
    rest = in_grp & (elane != i1)
    p2 = jnp.max(jnp.where(rest, pe, -1.0), axis=-1, keepdims=True)
    i2 = jnp.min(jnp.where(rest & (pe == p2), elane, LANES), axis=-1, keepdims=True)
    denom = p1 + p2
    w1 = pg_top * p1 / denom
    w2 = pg_top * p2 / denom
    r_ref[...] = jnp.where(lane == 0, i1,
                           jnp.where(lane == 1, i2, jnp.where(lane == 2, w1, jnp.where(lane == 3, w2, 0.0))))


def moe_router(h, g, w_group, w_expert):
    t, d = h.shape
    n_groups = w_group.shape[1]
    n_exp = w_expert.shape[1]
    tm = _pick(t, 256)
    nc = d // LANES
    w_r = jnp.concatenate([w_group.astype(F32), w_expert.astype(F32),
                           jnp.zeros((d, LANES - n_groups - n_exp), F32)], axis=1)
    return pl.pallas_call(
        functools.partial(_router_kernel, n_groups=n_groups, per_group=n_exp // n_groups),
        grid=(t // tm,),
        in_specs=[pl.BlockSpec((tm, d), lambda i: (i, 0)),
                  pl.BlockSpec((1, d), lambda i: (0, 0)),
                  pl.BlockSpec((d, LANES), lambda i: (0, 0))],
        out_specs=[pl.BlockSpec((tm * nc, LANES), lambda i: (i, 0)),
                   pl.BlockSpec((tm, LANES), lambda i: (i, 0))],
        out_shape=[jax.ShapeDtypeStruct((t * nc, LANES), F32),
                   jax.ShapeDtypeStruct((t, LANES), F32)],
        compiler_params=_params("parallel"),
        name="moe_router",
    )(h, g.reshape(1, d), w_r)


def _row_gather_start(idx_ref, base, n_rows, nc, src_hbm, dst, sem, inline=False):
    def body(r, carry):
        src_row = pl.multiple_of(idx_ref[base + r] * nc, nc)
        dst_row = r * nc if isinstance(r, int) else pl.multiple_of(r * nc, nc)
        pltpu.make_async_copy(src_hbm.at[pl.ds(src_row, nc), :], dst.at[pl.ds(dst_row, nc), :], sem).start()
        return carry
    if inline:
        for r in range(n_rows):
            body(r, 0)
    else:
        lax.fori_loop(0, n_rows, body, 0, unroll=8)


def _row_gather_wait(n_rows, nc, src_hbm, dst, sem):
    pltpu.make_async_copy(src_hbm.at[pl.ds(0, n_rows * nc), :], dst, sem).wait()


def _ffn_kernel(te_ref, nu_ref, rt_ref, x_hbm, w1_ref, w3_ref, w2_ref, o_ref, xbuf, sem, w1b, w3b, w2b):
    i = pl.program_id(0)
    n_used = nu_ref[0]
    d = w1b.shape[0]
    nc = d // LANES

    @pl.when(i >= n_used)
    def _():
        o_ref[...] = jnp.zeros_like(o_ref)

    @pl.when(i < n_used)
    def _():
        slot = i % 2

        @pl.when(i == 0)
        def _():
            _row_gather_start(rt_ref, 0, MOE_TILE, nc, x_hbm, xbuf.at[0], sem.at[0])

        prev = te_ref[jnp.maximum(i - 1, 0)]

        @pl.when((i == 0) | (te_ref[i] != prev))
        def _():
            w1b[...] = w1_ref[...].astype(BF16)
            w3b[...] = w3_ref[...].astype(BF16)
            w2b[...] = w2_ref[...].astype(BF16)

        _row_gather_wait(MOE_TILE, nc, x_hbm, xbuf.at[slot], sem.at[slot])
        nxt = jnp.where(i + 1 < n_used, i + 1, 0)
        _row_gather_start(rt_ref, nxt * MOE_TILE, MOE_TILE, nc, x_hbm, xbuf.at[1 - slot], sem.at[1 - slot],
                          inline=True)
        x = _load_row_chunks(xbuf.at[slot], MOE_TILE, d).astype(BF16)
        h1 = jnp.dot(x, w1b[...], preferred_element_type=F32)
        h3 = jnp.dot(x, w3b[...], preferred_element_type=F32)
        hdn = (_silu(h1) * h3).astype(BF16)
        _store_row_chunks(o_ref, jnp.dot(hdn, w2b[...], preferred_element_type=F32))

        @pl.when(i == n_used - 1)
        def _():
            _row_gather_wait(MOE_TILE, nc, x_hbm, xbuf.at[1 - slot], sem.at[1 - slot])


def moe_ffn(xn, row_token, tile_expert, n_used, w1, w3, w2, layer):
    d = w1.shape[2]
    nc = d // LANES
    r = row_token.shape[0]
    ff = w1.shape[3]
    n_tiles = r // MOE_TILE
    grid_spec = pltpu.PrefetchScalarGridSpec(
        num_scalar_prefetch=3,
        grid=(n_tiles,),
        in_specs=[pl.BlockSpec(memory_space=pl.ANY),
                  pl.BlockSpec((None, None, d, ff), lambda i, te, nu, rt: (layer, te[i], 0, 0)),
                  pl.BlockSpec((None, None, d, ff), lambda i, te, nu, rt: (layer, te[i], 0, 0)),
                  pl.BlockSpec((None, None, ff, d), lambda i, te, nu, rt: (layer, te[i], 0, 0))],
        out_specs=pl.BlockSpec((MOE_TILE * nc, LANES), lambda i, te, nu, rt: (i, 0)),
        scratch_shapes=[pltpu.VMEM((2, MOE_TILE * nc, LANES), F32), pltpu.SemaphoreType.DMA((2,)),
                        pltpu.VMEM((d, ff), BF16), pltpu.VMEM((d, ff), BF16), pltpu.VMEM((ff, d), BF16)],
    )
    return pl.pallas_call(
        _ffn_kernel,
        grid_spec=grid_spec,
        out_shape=jax.ShapeDtypeStruct((r * nc, LANES), F32),
        compiler_params=_params("arbitrary"),
        name="moe_ffn",
    )(tile_expert, n_used, row_token, xn, w1, w3, w2)


def _combine_kernel(dest_ref, h_ref, r_ref, y_hbm, o_ref, ybuf, sem, *, tm, n_tokens):
    i = pl.program_id(0)
    slot = i % 2
    d = h_ref.shape[1]
    nc = d // LANES

    def start(tile, to_slot):
        for k in range(MOE_TOP_K):
            _row_gather_start(dest_ref, k * n_tokens + tile * tm, tm, nc, y_hbm, ybuf.at[to_slot, k],
                              sem.at[to_slot])

    @pl.when(i == 0)
    def _():
        start(0, 0)

    @pl.when(i + 1 < pl.num_programs(0))
    def _():
        start(i + 1, 1 - slot)

    for k in range(MOE_TOP_K):
        _row_gather_wait(tm, nc, y_hbm, ybuf.at[slot, k], sem.at[slot])
    r = r_ref[...]
    for c in range(nc):
        cols = slice(c * LANES, (c + 1) * LANES)
        chunk = lambda k: ybuf[slot, k, pl.ds(c, tm, stride=nc), :]
        o_ref[:, cols] = h_ref[:, cols] + r[:, 2:3] * chunk(0) + r[:, 3:4] * chunk(1)


def moe_combine(h, y_sorted, dest, route):
    t, d = h.shape
    tm = _pick(t, 256)
    nc = d // LANES
    spec = pl.BlockSpec((tm, d), lambda i, dest: (i, 0))
    grid_spec = pltpu.PrefetchScalarGridSpec(
        num_scalar_prefetch=1,
        grid=(t // tm,),
        in_specs=[spec, pl.BlockSpec((tm, LANES), lambda i, dest: (i, 0)), pl.BlockSpec(memory_space=pl.ANY)],
        out_specs=spec,
        scratch_shapes=[pltpu.VMEM((2, MOE_TOP_K, tm * nc, LANES), F32), pltpu.SemaphoreType.DMA((2,))],
    )
    return pl.pallas_call(
        functools.partial(_combine_kernel, tm=tm, n_tokens=t),
        grid_spec=grid_spec,
        out_shape=jax.ShapeDtypeStruct((t, d), F32),
        compiler_params=_params("arbitrary"),
        name="moe_combine",
    )(dest, h, route, y_sorted)


def hier_moe_layer(h, g, w_group, w_expert, w1, w3, w2, layer):
    t, d = h.shape
    n_exp = w1.shape[1]
    xn, route = moe_router(h, g, w_group, w_expert)
    e_flat = jnp.concatenate([route[:, 0], route[:, 1]]).astype(jnp.int32)
    onehot = (e_flat[:, None] == jnp.arange(n_exp, dtype=jnp.int32)[None, :]).astype(jnp.int32)
    counts = jnp.sum(onehot, axis=0)
    rank = jnp.sum((jnp.cumsum(onehot, axis=0) - onehot) * onehot, axis=1)
    padded = ((counts + MOE_TILE - 1) // MOE_TILE) * MOE_TILE
    ends = jnp.cumsum(padded)
    dest = jnp.sum(onehot * (ends - padded)[None, :], axis=1) + rank
    n_rows = MOE_TOP_K * t + n_exp * MOE_TILE
    n_rows = ((n_rows + MOE_TILE - 1) // MOE_TILE) * MOE_TILE
    token = jnp.concatenate([jnp.arange(t, dtype=jnp.int32)] * MOE_TOP_K)
    row_token = jnp.zeros((n_rows,), jnp.int32).at[dest].set(token)
    tile_start = jnp.arange(n_rows // MOE_TILE, dtype=jnp.int32) * MOE_TILE
    tile_expert = jnp.minimum(jnp.sum((ends[None, :] <= tile_start[:, None]).astype(jnp.int32), axis=1), n_exp - 1)
    n_used = (ends[-1] // MOE_TILE).astype(jnp.int32).reshape(1)
    y_sorted = moe_ffn(xn, row_token, tile_expert, n_used, w1, w3, w2, layer)
    return moe_combine(h, y_sorted, dest, route)


def _rope_tables(positions, half):
    inv = ROPE_THETA ** (-np.arange(half, dtype=np.float64) / half)
    ang = positions.astype(np.float64)[:, None] * inv[None, :]
    cos, sin = np.cos(ang), np.sin(ang)
    return (jnp.asarray(np.concatenate([cos, cos], axis=1), F32),
            jnp.asarray(np.concatenate([-sin, sin], axis=1), F32))


def _rope(x, cos2, sin2):
    half = x.shape[-1] // 2
    return x * cos2 + jnp.concatenate([x[:, half:], x[:, :half]], axis=-1) * sin2


def _kv_kernel(x_ref, w_ref, g_ref, cos_ref, sin_ref, ckv_ref, kpe_ref, wb_ref, *, rank):
    @pl.when(pl.program_id(0) == 0)
    def _():
        wb_ref[...] = w_ref[...].astype(BF16)

    c = jnp.dot(x_ref[...], wb_ref[...], preferred_element_type=F32)
    lat = c[:, :rank]
    ckv_ref[...] = lat * lax.rsqrt(jnp.mean(lat * lat, axis=-1, keepdims=True) + NORM_EPS) * g_ref[...]
    kpe_ref[...] = _rope(c[:, rank:], cos_ref[...], sin_ref[...])


def mla_shared_kv(xkv, w_dkv, latent_g, cos2, sin2, rank):
    t, d = xkv.shape
    rope_dim = w_dkv.shape[1] - rank
    tm = _pick(t, 512)
    return pl.pallas_call(
        functools.partial(_kv_kernel, rank=rank),
        grid=(t // tm,),
        in_specs=[pl.BlockSpec((tm, d), lambda i: (i, 0)),
                  pl.BlockSpec((d, rank + rope_dim), lambda i: (0, 0)),
                  pl.BlockSpec((1, rank), lambda i: (0, 0)),
                  pl.BlockSpec((tm, rope_dim), lambda i: (i, 0)),
                  pl.BlockSpec((tm, rope_dim), lambda i: (i, 0))],
        out_specs=[pl.BlockSpec((tm, rank), lambda i: (i, 0)),
                   pl.BlockSpec((tm, rope_dim), lambda i: (i, 0))],
        out_shape=[jax.ShapeDtypeStruct((t, rank), F32),
                   jax.ShapeDtypeStruct((t, rope_dim), F32)],
        scratch_shapes=[pltpu.VMEM((d, rank + rope_dim), BF16)],
        compiler_params=_params("arbitrary"),
        name="mla_shared_kv",
    )(xkv, w_dkv, latent_g.reshape(1, rank), cos2, sin2)


def _q_kernel(x_ref, w_ref, cos_ref, sin_ref, o_ref, *, nope, scale, n_prompt_blocks):
    sc = jnp.where(pl.program_id(0) < n_prompt_blocks, scale * LOG2E, scale)
    q = _dot(x_ref[...], w_ref[...])
    o_ref[:, :nope] = (q[:, :nope] * sc).astype(o_ref.dtype)
    o_ref[:, nope:] = (_rope(q[:, nope:], cos_ref[...], sin_ref[...]) * sc).astype(o_ref.dtype)


def mla_query(xn, w_q_heads, cos2, sin2, nope, scale, tp):
    t, d = xn.shape
    n_heads, _, qk = w_q_heads.shape
    rope_dim = qk - nope
    tm = _pick(math.gcd(tp, t - tp), 1024)
    return pl.pallas_call(
        functools.partial(_q_kernel, nope=nope, scale=scale, n_prompt_blocks=tp // tm),
        grid=(t // tm, n_heads),
        in_specs=[pl.BlockSpec((tm, d), lambda i, h: (i, 0)),
                  pl.BlockSpec((None, d, qk), lambda i, h: (h, 0, 0)),
                  pl.BlockSpec((tm, rope_dim), lambda i, h: (i, 0)),
                  pl.BlockSpec((tm, rope_dim), lambda i, h: (i, 0))],
        out_specs=pl.BlockSpec((None, tm, qk), lambda i, h: (h, i, 0)),
        out_shape=jax.ShapeDtypeStruct((n_heads, t, qk), BF16),
        compiler_params=_params("parallel", "arbitrary"),
        name="mla_query",
    )(xn, w_q_heads, cos2, sin2)


def _kv_expand_kernel(ckv_ref, kpe_ref, wk_ref, wv_ref, k_ref, v_ref, *, nope):
    ckv = ckv_ref[...]
    k_ref[:, :nope] = _dot(ckv, wk_ref[...]).astype(k_ref.dtype)
    k_ref[:, nope:] = kpe_ref[...].astype(k_ref.dtype)
    v_ref[...] = _dot(ckv, wv_ref[...]).astype(v_ref.dtype)


def mla_expand_kv(ckv, kpe, w_uk_heads, w_uv_heads, tp):
    rank = ckv.shape[1]
    rope_dim = kpe.shape[1]
    n_heads, _, nope = w_uk_heads.shape
    dv = w_uv_heads.shape[2]
    tm = _pick(tp, 1024)
    return pl.pallas_call(
        functools.partial(_kv_expand_kernel, nope=nope),
        grid=(tp // tm, n_heads),
        in_specs=[pl.BlockSpec((tm, rank), lambda i, h: (i, 0)),
                  pl.BlockSpec((tm, rope_dim), lambda i, h: (i, 0)),
                  pl.BlockSpec((None, rank, nope), lambda i, h: (h, 0, 0)),
                  pl.BlockSpec((None, rank, dv), lambda i, h: (h, 0, 0))],
        out_specs=[pl.BlockSpec((None, tm, nope + rope_dim), lambda i, h: (h, i, 0)),
                   pl.BlockSpec((None, tm, dv), lambda i, h: (h, i, 0))],
        out_shape=[jax.ShapeDtypeStruct((n_heads, tp, nope + rope_dim), BF16),
                   jax.ShapeDtypeStruct((n_heads, tp, dv), BF16)],
        compiler_params=_params("parallel", "arbitrary"),
        name="mla_expand_kv",
    )(ckv, kpe, w_uk_heads, w_uv_heads)


def _flash_kernel(qi_ref, kj_ref, q_ref, k_ref, v_ref, o_ref, m_scr, l_scr, acc_scr, *, tile):
    step = pl.program_id(1)
    qi = qi_ref[step]
    kj = kj_ref[step]

    @pl.when(kj == 0)
    def _():
        m_scr[...] = jnp.full_like(m_scr, NEG_BIG)
        l_scr[...] = jnp.zeros_like(l_scr)
        acc_scr[...] = jnp.zeros_like(acc_scr)

    s = lax.dot_general(q_ref[...], k_ref[...], (((1,), (1,)), ((), ())), preferred_element_type=F32)

    def update(s):
        m_prev = m_scr[...]
        m_new = jnp.maximum(m_prev, jnp.max(s, axis=-1, keepdims=True))
        alpha = jnp.exp2(m_prev - m_new)
        p = jnp.exp2(s - m_new)
        l_scr[...] = alpha * l_scr[...] + jnp.sum(p, axis=-1, keepdims=True)
        acc_scr[...] = alpha * acc_scr[...] + jnp.dot(p.astype(BF16), v_ref[...], preferred_element_type=F32)
        m_scr[...] = m_new

    @pl.when(kj < qi)
    def _():
        update(s)

    @pl.when(kj == qi)
    def _():
        row = lax.broadcasted_iota(jnp.int32, s.shape, 0)
        col = lax.broadcasted_iota(jnp.int32, s.shape, 1)
        update(jnp.where(col <= row, s, NEG_BIG))
        o_ref[...] = (acc_scr[...] / l_scr[...]).astype(o_ref.dtype)


def mla_prompt_attention(q_heads, k_heads, v_heads, tp):
    n_heads, _, qk = q_heads.shape
    dv = v_heads.shape[2]
    tile = _pick(tp, ATTN_TILE)
    n_blocks = tp // tile
    pairs = [(i, j) for i in range(n_blocks) for j in range(i + 1)]
    qi_list = jnp.asarray([p[0] for p in pairs], jnp.int32)
    kj_list = jnp.asarray([p[1] for p in pairs], jnp.int32)
    grid_spec = pltpu.PrefetchScalarGridSpec(
        num_scalar_prefetch=2,
        grid=(n_heads, len(pairs)),
        in_specs=[pl.BlockSpec((None, tile, qk), lambda h, s, qi, kj: (h, qi[s], 0)),
                  pl.BlockSpec((None, tile, qk), lambda h, s, qi, kj: (h, kj[s], 0)),
                  pl.BlockSpec((None, tile, dv), lambda h, s, qi, kj: (h, kj[s], 0))],
        out_specs=pl.BlockSpec((tile, dv), lambda h, s, qi, kj: (qi[s], h)),
        scratch_shapes=[pltpu.VMEM((tile, 1), F32), pltpu.VMEM((tile, 1), F32), pltpu.VMEM((tile, dv), F32)],
    )
    return pl.pallas_call(
        functools.partial(_flash_kernel, tile=tile),
        grid_spec=grid_spec,
        out_shape=jax.ShapeDtypeStruct((tp, n_heads * dv), BF16),
        compiler_params=_params("parallel", "arbitrary"),
        name="mla_prompt_attention",
    )(qi_list, kj_list, q_heads, k_heads, v_heads)


def _absorb_kernel(q_ref, wk_ref, o_ref, *, nope):
    q = q_ref[...]
    rank = wk_ref.shape[0]
    o_ref[:, :rank] = _dot_nt(q[:, :nope], wk_ref[...]).astype(o_ref.dtype)
    o_ref[:, rank:] = q[:, nope:]


def mla_absorb_query(q_heads, w_uk_heads, tp):
    n_heads, t, qk = q_heads.shape
    _, rank, nope = w_uk_heads.shape
    ts = t - tp
    tm = _pick(math.gcd(ts, tp), 1024)
    off = tp // tm
    return pl.pallas_call(
        functools.partial(_absorb_kernel, nope=nope),
        grid=(n_heads, ts // tm),
        in_specs=[pl.BlockSpec((None, tm, qk), lambda h, i: (h, off + i, 0)),
                  pl.BlockSpec((None, rank, nope), lambda h, i: (h, 0, 0))],
        out_specs=pl.BlockSpec((None, tm, rank + qk - nope), lambda h, i: (h, i, 0)),
        out_shape=jax.ShapeDtypeStruct((n_heads, ts, rank + qk - nope), BF16),
        compiler_params=_params("parallel", "parallel"),
        name="mla_absorb_query",
    )(q_heads, w_uk_heads)


def _paged_kernel(pt_ref, q_ref, cn_ref, kn_ref, ckv_hbm, kpe_hbm, o_ref, m_scr, l_scr, acc_scr, ckbuf, kpbuf, sem,
                  ck_scr, kp_scr, *, pages_per_step, rank, dec_seq, page, n_sub):
    s_idx = pl.program_id(0)
    j = pl.program_id(1)
    n_j = pl.num_programs(1)
    step = s_idx * n_j + j
    n_steps = pl.num_programs(0) * n_j
    slot = step % 2

    def page_copies(seq, blk, to_slot):
        copies = []
        for pg in range(pages_per_step):
            pid = pt_ref[seq, blk * pages_per_step + pg]
            copies.append(pltpu.make_async_copy(
                ckv_hbm.at[pid], ckbuf.at[to_slot, pl.ds(pg * page, page), :], sem.at[to_slot]))
            copies.append(pltpu.make_async_copy(
                kpe_hbm.at[pid], kpbuf.at[to_slot, :, pl.ds(pg * page, page)], sem.at[to_slot]))
        return copies

    @pl.when(step == 0)
    def _():
        for cp in page_copies(s_idx, j, slot):
            cp.start()

    for cp in page_copies(s_idx, j, slot):
        cp.wait()

    n_heads = q_ref.shape[0]
    q = q_ref[...].reshape(n_heads * dec_seq, q_ref.shape[2])
    q_lat, q_pe = q[:, :rank], q[:, rank:]

    @pl.when(j == 0)
    def _():
        pad = lambda a: jnp.concatenate([a, jnp.zeros((LANES - dec_seq, a.shape[1]), a.dtype)], axis=0)
        cn = pad(cn_ref[...]).astype(BF16)
        s = _dot_nt(q_lat, cn) + _dot_nt(q_pe, pad(kn_ref[...]))
        qtok = lax.broadcasted_iota(jnp.int32, s.shape, 0) & (dec_seq - 1)
        ktok = lax.broadcasted_iota(jnp.int32, s.shape, 1)
        s = jnp.where(ktok <= qtok, s, NEG_BIG)
        m0 = jnp.max(s, axis=-1, keepdims=True)
        p = jnp.exp(s - m0)
        m_scr[...] = m0
        l_scr[...] = jnp.sum(p, axis=-1, keepdims=True)
        acc_scr[...] = jnp.dot(p.astype(BF16), cn, preferred_element_type=F32)

    nxt = jnp.where(step + 1 < n_steps, step + 1, 0)
    for cp in page_copies(nxt // n_j, nxt % n_j, 1 - slot):
        cp.start()

    for pg in range(pages_per_step):
        ck_scr[pg * page:(pg + 1) * page, :] = ckbuf[slot, pg * page:(pg + 1) * page, :].astype(BF16)
    kp_scr[...] = kpbuf[slot].astype(BF16)

    sub = pages_per_step * page // n_sub
    cks = [ck_scr[i * sub:(i + 1) * sub, :] for i in range(n_sub)]
    ss = [lax.dot_general(q_lat, cks[i], (((1,), (1,)), ((), ())), preferred_element_type=F32)
          + jnp.dot(q_pe, kp_scr[:, i * sub:(i + 1) * sub], preferred_element_type=F32) for i in range(n_sub)]
    ms = [jnp.max(s, axis=-1, keepdims=True) for s in ss]
    ps = [jnp.exp(s - m) for s, m in zip(ss, ms)]
    ls = [jnp.sum(p, axis=-1, keepdims=True) for p in ps]
    accs = [jnp.dot(p.astype(BF16), ck, preferred_element_type=F32) for p, ck in zip(ps, cks)]
    m_prev = m_scr[...]
    m_new = functools.reduce(jnp.maximum, ms, m_prev)
    alpha = jnp.exp(m_prev - m_new)
    l_new = alpha * l_scr[...]
    acc_new = alpha * acc_scr[...]
    for m, l, acc in zip(ms, ls, accs):
        coef = jnp.exp(m - m_new)
        l_new = l_new + coef * l
        acc_new = acc_new + coef * acc
    m_scr[...] = m_new
    l_scr[...] = l_new
    acc_scr[...] = acc_new

    @pl.when(j == n_j - 1)
    def _():
        o_ref[...] = (acc_new / l_new).reshape(o_ref.shape).astype(o_ref.dtype)

    @pl.when(step == n_steps - 1)
    def _():
        for cp in page_copies(0, 0, 1 - slot):
            cp.wait()


def mla_paged_attention(q_abs, ckv_new, kpe_new, cache_ckv, cache_kpe_t, page_table, dec_seq):
    n_heads, ts, key_dim = q_abs.shape
    n_seq, n_pages = page_table.shape
    _, page, rank = cache_ckv.shape
    rope_dim = cache_kpe_t.shape[1]
    pps = _divisor(n_pages, PAGES_PER_STEP)
    rows = n_heads * dec_seq
    keys = pps * page
    grid_spec = pltpu.PrefetchScalarGridSpec(
        num_scalar_prefetch=1,
        grid=(n_seq, n_pages // pps),
        in_specs=[pl.BlockSpec((n_heads, dec_seq, key_dim), lambda s, j, pt: (0, s, 0)),
                  pl.BlockSpec((dec_seq, rank), lambda s, j, pt: (s, 0)),
                  pl.BlockSpec((dec_seq, rope_dim), lambda s, j, pt: (s, 0)),
                  pl.BlockSpec(memory_space=pl.ANY),
                  pl.BlockSpec(memory_space=pl.ANY)],
        out_specs=pl.BlockSpec((n_heads, dec_seq, rank), lambda s, j, pt: (0, s, 0)),
        scratch_shapes=[pltpu.VMEM((rows, 1), F32), pltpu.VMEM((rows, 1), F32), pltpu.VMEM((rows, rank), F32),
                        pltpu.VMEM((2, keys, rank), F32), pltpu.VMEM((2, rope_dim, keys), F32),
                        pltpu.SemaphoreType.DMA((2,)),
                        pltpu.VMEM((keys, rank), BF16), pltpu.VMEM((rope_dim, keys), BF16)],
    )
    return pl.pallas_call(
        functools.partial(_paged_kernel, pages_per_step=pps, rank=rank, dec_seq=dec_seq, page=page,
                          n_sub=_divisor(pps, PAGED_SUB_BLOCKS)),
        grid_spec=grid_spec,
        out_shape=jax.ShapeDtypeStruct((n_heads, ts, rank), BF16),
        compiler_params=_params("arbitrary", "arbitrary"),
        name="mla_paged_attention",
    )(page_table, q_abs, ckv_new, kpe_new, cache_ckv, cache_kpe_t)


def _unabsorb_kernel(o_ref, wv_ref, out_ref):
    out_ref[...] = _dot(o_ref[...], wv_ref[...]).astype(out_ref.dtype)


def mla_unabsorb(o_lat, w_uv_heads):
    n_heads, ts, rank = o_lat.shape
    dv = w_uv_heads.shape[2]
    tm = _pick(ts, 1024)
    return pl.pallas_call(
        _unabsorb_kernel,
        grid=(n_heads, ts // tm),
        in_specs=[pl.BlockSpec((None, tm, rank), lambda h, i: (h, i, 0)),
                  pl.BlockSpec((None, rank, dv), lambda h, i: (h, 0, 0))],
        out_specs=pl.BlockSpec((tm, dv), lambda h, i: (i, h)),
        out_shape=jax.ShapeDtypeStruct((ts, n_heads * dv), BF16),
        compiler_params=_params("parallel", "parallel"),
        name="mla_unabsorb",
    )(o_lat, w_uv_heads)


def kernel(x_prompt, x_sample, state_ssm, state_conv, cache_ckv, cache_kpe, page_table, norm_mix, norm_ffn,
           norm_final, gdn_w_in, gdn_conv_w, gdn_a_log, gdn_dt_bias, gdn_norm, gdn_w_out, kv_norm, w_dkv,
           kv_latent_norm, w_uk, w_uv, mla_w_q, mla_w_o, moe_w_group, moe_w_expert, moe_w1, moe_w3, moe_w2):
    batch, seq, d = x_prompt.shape
    n_seq, dec_seq, _ = x_sample.shape
    assert batch == 1 and dec_seq == SUBLANES
    tp, ts = seq, n_seq * dec_seq
    depth = norm_mix.shape[0]
    n_a = state_ssm.shape[0]
    n_vh, dk, dv = state_ssm.shape[2:]
    conv_dim = state_conv.shape[3]
    qk_dim = (conv_dim - n_vh * dv) // 2
    n_kh = qk_dim // dk
    n_taps = gdn_conv_w.shape[1]
    rank = kv_latent_norm.shape[0]
    rope_dim = w_dkv.shape[1] - rank
    n_heads, nope = w_uk.shape[1:]
    past_len = page_table.shape[1] * cache_ckv.shape[1]
    scale = float((nope + rope_dim) ** -0.5)

    h = jnp.concatenate([x_prompt.reshape(tp, d), x_sample.reshape(ts, d)], axis=0)
    positions = np.concatenate([np.arange(tp), past_len + np.arange(dec_seq)[None, :].repeat(n_seq, 0).reshape(-1)])
    cos2, sin2 = _rope_tables(positions, rope_dim // 2)

    ssm_p, conv_p, ssm_s, conv_s = [], [], [], []
    ckv = kpe = None
    for i in range(depth):
        if i < n_a:
            xn = rmsnorm(h, norm_mix[i], BF16)
            w_in = gdn_w_in[i]
            main_cols = conv_dim + n_vh * dv
            proj = matmul(xn, w_in, F32, n_cols=main_cols)
            gates = gdn_gates(xn, w_in[:, main_cols:], gdn_a_log[i], gdn_dt_bias[i])
            hist = jnp.pad(state_conv[i], ((0, 0), (SUBLANES - (n_taps - 1), 0), (0, 0)))
            qkv = gdn_conv(proj, hist, gdn_conv_w[i], tp, qk_dim, conv_dim, dk)
            gates_t = gates.T
            o_p, s_p = gdn_prompt(qkv, proj, gates_t, gdn_norm[i], tp, n_kh, n_vh, dk, dv, GDN_CHUNK)
            o_s, s_s = gdn_sample(qkv, proj, gates_t, gdn_norm[i], state_ssm[i], tp, n_kh, n_vh, dk, dv, dec_seq)
            h = matmul(jnp.concatenate([o_p, o_s], axis=0), gdn_w_out[i], F32, res=h, tn_pref=512)
            ssm_p.append(s_p[None])
            ssm_s.append(s_s)
            conv_p.append(proj[tp - (n_taps - 1):tp, :conv_dim][None])
            conv_s.append(proj[tp:, :conv_dim].reshape(n_seq, dec_seq, conv_dim)[:, dec_seq - (n_taps - 1):])
        else:
            if i == n_a:
                ckv, kpe = mla_shared_kv(rmsnorm(h, kv_norm, BF16), w_dkv, kv_latent_norm, cos2, sin2, rank)
                w_uk_heads = jnp.transpose(w_uk, (1, 0, 2))
                w_uv_heads = jnp.transpose(w_uv, (1, 0, 2))
                k_heads, v_heads = mla_expand_kv(ckv, kpe, w_uk_heads, w_uv_heads, tp)
                cache_kpe_t = jnp.swapaxes(cache_kpe, 1, 2)
            j = i - n_a
            xn = rmsnorm(h, norm_mix[i], BF16)
            w_q_heads = jnp.transpose(mla_w_q[j].reshape(d, n_heads, nope + rope_dim), (1, 0, 2))
            q_heads = mla_query(xn, w_q_heads, cos2, sin2, nope, scale, tp)
            o_p = mla_prompt_attention(q_heads, k_heads, v_heads, tp)
            q_abs = mla_absorb_query(q_heads, w_uk_heads, tp)
            o_lat = mla_paged_attention(q_abs, ckv[tp:], kpe[tp:], cache_ckv, cache_kpe_t, page_table, dec_seq)
            o_s = mla_unabsorb(o_lat, w_uv_heads)
            h = matmul(jnp.concatenate([o_p, o_s], axis=0), mla_w_o[j], F32, res=h)
        h = hier_moe_layer(h, norm_ffn[i], moe_w_group[i], moe_w_expert[i], moe_w1, moe_w3, moe_w2, i)

    y_prompt = rmsnorm(h, norm_final, F32, 0, tp).reshape(batch, seq, d)
    y_sample = rmsnorm(h, norm_final, F32, tp, ts).reshape(n_seq, dec_seq, d)
    return (y_prompt, y_sample,
            jnp.stack(ssm_p), jnp.stack(conv_p),
            ckv[:tp].reshape(batch, seq, rank), kpe[:tp].reshape(batch, seq, rope_dim),
            jnp.stack(ssm_s), jnp.stack(conv_s),
            ckv[tp:].reshape(n_seq, dec_seq, rank), kpe[tp:].reshape(n_seq, dec_seq, rope_dim))
```

```python
import functools
import math

import numpy as np
import jax
import jax.numpy as jnp
from jax import lax
from jax.experimental import pallas as pl
from jax.experimental.pallas import tpu as pltpu

F32 = jnp.float32
BF16 = jnp.bfloat16

NORM_EPS = 1e-6
ROPE_THETA = 10000.0
GDN_CHUNK = 64
MOE_TOP_K = 2
SUBLANES = 8
LANES = 128
VMEM_LIMIT_BYTES = 56 * 1024 * 1024
GDN_ROWS = 128
GDN_PROMPT_HEADS = 16
GDN_SAMPLE_HEADS = 4
MOE_TILE = 256
ATTN_TILE = 1024
MLA_QUERY_HEADS = 4
PAGES_PER_STEP = 32
PAGED_SUB_BLOCKS = 4
NEG_BIG = -1e30
LOG2E = math.log2(math.e)


def _params(*sem):
    return pltpu.CompilerParams(dimension_semantics=sem, vmem_limit_bytes=VMEM_LIMIT_BYTES)


def _pick(n, pref):
    t = min(pref, n)
    while n % t:
        t -= SUBLANES
    return t


def _divisor(n, pref):
    t = min(pref, n)
    while n % t:
        t -= 1
    return t


def _dot(a, b):
    return jnp.dot(a.astype(BF16), b.astype(BF16), preferred_element_type=F32)


def _dot_nt(a, b):
    return lax.dot_general(a.astype(BF16), b.astype(BF16), (((1,), (1,)), ((), ())),
                           preferred_element_type=F32)


def _split(a):
    hi = a.astype(BF16)
    lo = (a - hi.astype(F32)).astype(BF16)
    return hi, lo


def _dot3(a, b):
    ah, al = _split(a)
    bh, bl = _split(b)
    d = lambda x, y: jnp.dot(x, y, preferred_element_type=F32)
    return d(ah, bh) + (d(ah, bl) + d(al, bh))


def _sigmoid(x):
    return 1.0 / (1.0 + jnp.exp(-x))


def _silu(x):
    return x * _sigmoid(x)


def _rmsnorm_kernel(x_ref, g_ref, o_ref):
    x = x_ref[...]
    y = x * lax.rsqrt(jnp.mean(x * x, axis=-1, keepdims=True) + NORM_EPS) * g_ref[...]
    o_ref[...] = y.astype(o_ref.dtype)


def rmsnorm(x, g, out_dtype, row_start=0, rows=None):
    t, d = x.shape
    rows = t if rows is None else rows
    tm = _pick(math.gcd(rows, row_start) if row_start else rows, 512)
    off = row_start // tm
    return pl.pallas_call(
        _rmsnorm_kernel,
        grid=(rows // tm,),
        in_specs=[pl.BlockSpec((tm, d), lambda i: (i + off, 0)),
                  pl.BlockSpec((1, d), lambda i: (0, 0))],
        out_specs=pl.BlockSpec((tm, d), lambda i: (i, 0)),
        out_shape=jax.ShapeDtypeStruct((rows, d), out_dtype),
        compiler_params=_params("parallel"),
        name="rmsnorm",
    )(x, g.reshape(1, d))


def _matmul_kernel(*refs, has_res, n_first):
    n_x = 1 if n_first is None else 2
    xa_ref, xb_ref, w_ref = refs[0], refs[n_x - 1], refs[n_x]
    if has_res:
        r_ref, o_ref, wb_ref = refs[n_x + 1:]
    else:
        o_ref, wb_ref = refs[n_x + 1:]
    i = pl.program_id(1)

    @pl.when(i == 0)
    def _():
        wb_ref[...] = w_ref[...].astype(BF16)

    def emit(x_ref):
        acc = jnp.dot(x_ref[...], wb_ref[...], preferred_element_type=F32)
        if has_res:
            acc = acc + r_ref[...]
        o_ref[...] = acc.astype(o_ref.dtype)

    if n_first is None:
        emit(xa_ref)
    else:
        @pl.when(i < n_first)
        def _():
            emit(xa_ref)

        @pl.when(i >= n_first)
        def _():
            emit(xb_ref)


def matmul(x, w, out_dtype, n_cols=None, res=None, tm_pref=1024, tn_pref=1024):
    xs = x if isinstance(x, tuple) else (x,)
    ma, k = xs[0].shape
    m = sum(a.shape[0] for a in xs)
    n = w.shape[1] if n_cols is None else n_cols
    tm = _pick(math.gcd(ma, m - ma) if m > ma else m, tm_pref)
    tn = _pick(n, tn_pref)
    na = ma // tm if len(xs) == 2 else None
    if na is None:
        in_specs = [pl.BlockSpec((tm, k), lambda j, i: (i, 0))]
    else:
        in_specs = [pl.BlockSpec((tm, k), lambda j, i: (jnp.minimum(i, na - 1), 0)),
                    pl.BlockSpec((tm, k), lambda j, i: (jnp.maximum(i - na, 0), 0))]
    in_specs.append(pl.BlockSpec((k, tn), lambda j, i: (0, j)))
    args = [*xs, w]
    if res is not None:
        in_specs.append(pl.BlockSpec((tm, tn), lambda j, i: (i, j)))
        args.append(res)
    return pl.pallas_call(
        functools.partial(_matmul_kernel, has_res=res is not None, n_first=na),
        grid=(n // tn, m // tm),
        in_specs=in_specs,
        out_specs=pl.BlockSpec((tm, tn), lambda j, i: (i, j)),
        out_shape=jax.ShapeDtypeStruct((m, n), out_dtype),
        scratch_shapes=[pltpu.VMEM((k, tn), BF16)],
        compiler_params=_params("parallel", "arbitrary"),
        name="matmul",
    )(*args)


def _gates_kernel(x_ref, w_ref, a_ref, dt_ref, o_ref, *, n_heads):
    ba = _dot(x_ref[...], w_ref[...])
    lane = lax.broadcasted_iota(jnp.int32, ba.shape, 1)
    beta = _sigmoid(ba)
    xs = ba + dt_ref[...]
    softplus = jnp.maximum(xs, 0.0) + jnp.log(1.0 + jnp.exp(-jnp.abs(xs)))
    g = -jnp.exp(a_ref[...]) * softplus
    o_ref[...] = jnp.where(lane < n_heads, beta, g)


def gdn_gates(xn, w_ba, a_log, dt_bias):
    t, d = xn.shape
    hv = a_log.shape[0]
    tm = _pick(t, 512)
    zeros = jnp.zeros((hv,), F32)
    a_pad = jnp.concatenate([zeros, a_log.astype(F32)]).reshape(1, 2 * hv)
    dt_pad = jnp.concatenate([zeros, dt_bias.astype(F32)]).reshape(1, 2 * hv)
    return pl.pallas_call(
        functools.partial(_gates_kernel, n_heads=hv),
        grid=(t // tm,),
        in_specs=[pl.BlockSpec((tm, d), lambda i: (i, 0)),
                  pl.BlockSpec((d, 2 * hv), lambda i: (0, 0)),
                  pl.BlockSpec((1, 2 * hv), lambda i: (0, 0)),
                  pl.BlockSpec((1, 2 * hv), lambda i: (0, 0))],
        out_specs=pl.BlockSpec((tm, 2 * hv), lambda i: (i, 0)),
        out_shape=jax.ShapeDtypeStruct((t, 2 * hv), F32),
        compiler_params=_params("parallel"),
        name="gdn_gates",
    )(xn, w_ba, a_pad, dt_pad)


def _conv_kernel(x_ref, prev_ref, st_ref, w_ref, o_ref, *, n_prompt_blocks, n_q_blocks, n_qk_blocks,
                 n_taps, head_dim):
    i = pl.program_id(0)
    j = pl.program_id(1)
    x3 = x_ref[...]
    prev = jnp.where(i == 0, 0.0, prev_ref[...])
    hist_prompt = jnp.concatenate([prev, x3[:-1]], axis=0)
    hist = jnp.where(i >= n_prompt_blocks, st_ref[...], hist_prompt)
    row = lax.broadcasted_iota(jnp.int32, x3.shape, 1)
    tap = lambda s: w_ref[n_taps - 1 - s:n_taps - s, :][None]
    acc = x3 * tap(0)
    for s in range(1, n_taps):
        shifted = jnp.where(row >= s, pltpu.roll(x3, s, 1), pltpu.roll(hist, s, 1))
        acc = acc + shifted * tap(s)
    act = _silu(acc).reshape(x3.shape[0] * SUBLANES, x3.shape[2])
    scale = jnp.where(j < n_q_blocks, head_dim ** -0.5, 1.0)
    is_qk = j < n_qk_blocks
    for c in range(act.shape[1] // head_dim):
        seg = act[:, c * head_dim:(c + 1) * head_dim]
        nrm = seg * (lax.rsqrt(jnp.sum(seg * seg, axis=-1, keepdims=True) + NORM_EPS) * scale)
        o_ref[:, c * head_dim:(c + 1) * head_dim] = jnp.where(is_qk, nrm, seg)


def gdn_conv(proj, state_hist, conv_w, tp, qk_dim, conv_dim, head_dim):
    t = proj.shape[0]
    n_taps = conv_w.shape[0]
    tm = _pick(math.gcd(tp, t - tp), 512)
    tc = _pick(qk_dim, 1024)
    tiles = tm // SUBLANES
    proj3 = proj.reshape(t // SUBLANES, SUBLANES, proj.shape[1])
    npb = tp // tm
    return pl.pallas_call(
        functools.partial(_conv_kernel, n_prompt_blocks=npb, n_q_blocks=qk_dim // tc,
                          n_qk_blocks=2 * qk_dim // tc, n_taps=n_taps, head_dim=head_dim),
        grid=(t // tm, conv_dim // tc),
        in_specs=[pl.BlockSpec((tiles, SUBLANES, tc), lambda i, j: (i, 0, j)),
                  pl.BlockSpec((1, SUBLANES, tc), lambda i, j: (jnp.maximum(i * tiles - 1, 0), 0, j)),
                  pl.BlockSpec((tiles, SUBLANES, tc), lambda i, j: (jnp.maximum(i - npb, 0), 0, j)),
                  pl.BlockSpec((n_taps, tc), lambda i, j: (0, j))],
        out_specs=pl.BlockSpec((tm, tc), lambda i, j: (i, j)),
        out_shape=jax.ShapeDtypeStruct((t, conv_dim), F32),
        compiler_params=_params("parallel", "parallel"),
        name="gdn_conv",
    )(proj3, proj3, state_hist, conv_w)


def _gdn_blocks(qs, ks, qk_of, vs, zs, g_rows, b_rows, norm_g, states, chunk):
    r = GDN_ROWS
    n_groups = r // chunk
    units = range(len(vs))
    ri = lax.broadcasted_iota(jnp.int32, (r, r), 0)
    ci = lax.broadcasted_iota(jnp.int32, (r, r), 1)
    shift = int(math.log2(chunk))
    rg = ri >> shift
    cg = ci >> shift
    same = rg == cg
    causal = same & (ri >= ci)
    strict = same & (ri > ci)
    eye = ri == ci
    last = ci == (rg << shift) + (chunk - 1)
    eye_f = jnp.where(eye, 1.0, 0.0)
    bcast = lambda a: jnp.broadcast_to(a, (r, r))

    gc_col = [jnp.sum(jnp.where(causal, bcast(g_rows[u]), 0.0), axis=1, keepdims=True) for u in units]
    gc_rb = [bcast(jnp.sum(jnp.where(eye, bcast(gc_col[u]), 0.0), axis=0, keepdims=True)) for u in units]
    b_col = [jnp.sum(jnp.where(eye, bcast(b_rows[u]), 0.0), axis=1, keepdims=True) for u in units]
    glast_col = [jnp.sum(jnp.where(last, gc_rb[u], 0.0), axis=1, keepdims=True) for u in units]
    decay = [jnp.where(causal, jnp.exp(jnp.where(causal, gc_col[u] - gc_rb[u], 0.0)), 0.0) for u in units]

    kk = [_dot_nt(k, k) for k in ks]
    qk = [_dot_nt(q, k) for q, k in zip(qs, ks)]
    lmat = [jnp.where(strict, b_col[u] * kk[qk_of[u]] * decay[u], 0.0) for u in units]
    tmat = [eye_f - lmat[u] for u in units]
    pw = lmat
    for _ in range(shift - 1):
        pw = [_dot(pw[u], pw[u]) for u in units]
        tmat = [tmat[u] + _dot(tmat[u], pw[u]) for u in units]

    eg_col = [jnp.exp(gc_col[u]) for u in units]
    uu = [_dot(tmat[u], vs[u] * b_col[u]) for u in units]
    ww = [_dot(tmat[u], ks[qk_of[u]] * (b_col[u] * eg_col[u])) for u in units]
    qg = [qs[qk_of[u]] * eg_col[u] for u in units]
    amat = [jnp.where(causal, qk[qk_of[u]] * decay[u], 0.0) for u in units]

    ws_qs = [[_dot(jnp.concatenate([ww[u][gi * chunk:(gi + 1) * chunk], qg[u][gi * chunk:(gi + 1) * chunk]], axis=0),
                   states[u][gi]) for gi in range(n_groups)] for u in units]
    vn = [uu[u] - jnp.concatenate([ws_qs[u][gi][:chunk] for gi in range(n_groups)], axis=0) for u in units]
    o = [jnp.concatenate([ws_qs[u][gi][chunk:] for gi in range(n_groups)], axis=0) + _dot(amat[u], vn[u])
         for u in units]

    kd_t = [(ks[qk_of[u]] * jnp.exp(glast_col[u] - gc_col[u])).T for u in units]
    new_states = [[states[u][gi] * jnp.exp(glast_col[u][gi * chunk:gi * chunk + 1, :])
                   + _dot(jnp.where(cg == gi, kd_t[u], 0.0), vn[u]) for gi in range(n_groups)] for u in units]

    outs = [o[u] * lax.rsqrt(jnp.mean(o[u] * o[u], axis=-1, keepdims=True) + NORM_EPS) * norm_g * _silu(zs[u])
            for u in units]
    return outs, new_states


def _gdn_prompt_kernel(q_ref, k_ref, v_ref, z_ref, g_ref, b_ref, ng_ref, o_ref, sfin_ref, s_scr, *,
                       heads_per_step, chunk, dv):
    c = pl.program_id(1)
    rep = GDN_ROWS // chunk

    @pl.when(c == 0)
    def _():
        s_scr[...] = jnp.zeros_like(s_scr)

    heads = range(heads_per_step)
    tile = lambda ref, hh: jnp.concatenate([ref[:, hh * LANES:(hh + 1) * LANES]] * rep, axis=0)
    stack = lambda ref, hh: jnp.concatenate(
        [ref[:, (hh * rep + e) * dv:(hh * rep + e + 1) * dv] for e in range(rep)], axis=0)
    outs, new_states = _gdn_blocks(
        [tile(q_ref, hh) for hh in heads], [tile(k_ref, hh) for hh in heads], list(heads),
        [stack(v_ref, hh) for hh in heads], [stack(z_ref, hh) for hh in heads],
        [g_ref[hh] for hh in heads], [b_ref[hh] for hh in heads], ng_ref[...],
        [[s_scr[hh * rep + e] for e in range(rep)] for hh in heads], chunk)
    for hh in heads:
        for e in range(rep):
            s_scr[hh * rep + e] = new_states[hh][e]
            o_ref[:, (hh * rep + e) * dv:(hh * rep + e + 1) * dv] = (
                outs[hh][e * chunk:(e + 1) * chunk].astype(o_ref.dtype))

    @pl.when(c == pl.num_programs(1) - 1)
    def _():
        sfin_ref[...] = s_scr[...]


def gdn_prompt(qkv, proj, gates_t, norm_g, tp, n_kh, n_vh, dk, dv, chunk):
    rep = n_vh // n_kh
    assert rep * chunk == GDN_ROWS and dk == LANES and dv == LANES
    hps = _divisor(n_kh, GDN_PROMPT_HEADS)
    n_chunks = tp // chunk
    qk_blocks = n_kh // hps
    conv_dim = 2 * n_kh * dk + n_vh * dv

    def arrange(gt):
        a = gt[:, :tp].reshape(qk_blocks, hps, rep, n_chunks, chunk)
        return a.transpose(0, 3, 1, 2, 4).reshape(qk_blocks, n_chunks, hps, 1, GDN_ROWS)

    beta_a = arrange(gates_t[:n_vh])
    g_a = arrange(gates_t[n_vh:])
    vw = hps * rep * dv
    v_off = 2 * n_kh * dk // vw
    z_off = conv_dim // vw
    gate_spec = pl.BlockSpec((None, None, hps, 1, GDN_ROWS), lambda h, c: (h, c, 0, 0, 0))
    return pl.pallas_call(
        functools.partial(_gdn_prompt_kernel, heads_per_step=hps, chunk=chunk, dv=dv),
        grid=(qk_blocks, n_chunks),
        in_specs=[pl.BlockSpec((chunk, hps * dk), lambda h, c: (c, h)),
                  pl.BlockSpec((chunk, hps * dk), lambda h, c: (c, qk_blocks + h)),
                  pl.BlockSpec((chunk, vw), lambda h, c: (c, v_off + h)),
                  pl.BlockSpec((chunk, vw), lambda h, c: (c, z_off + h)),
                  gate_spec, gate_spec,
                  pl.BlockSpec((1, dv), lambda h, c: (0, 0))],
        out_specs=[pl.BlockSpec((chunk, vw), lambda h, c: (c, h)),
                   pl.BlockSpec((hps * rep, dk, dv), lambda h, c: (h, 0, 0))],
        out_shape=[jax.ShapeDtypeStruct((tp, n_vh * dv), BF16),
                   jax.ShapeDtypeStruct((n_vh, dk, dv), F32)],
        scratch_shapes=[pltpu.VMEM((hps * rep, dk, dv), F32)],
        compiler_params=_params("parallel", "arbitrary"),
        name="gdn_prompt",
    )(qkv, qkv, qkv, proj, g_a, beta_a, norm_g.reshape(1, dv))


def _gdn_sample_kernel(q_ref, k_ref, v_ref, z_ref, g_ref, b_ref, ng_ref, s_ref, o_ref, snew_ref, *,
                       heads_per_step, rep, chunk, dv):
    n_groups = GDN_ROWS // chunk
    n_qk = heads_per_step // rep
    heads = range(heads_per_step)
    cols = lambda ref, n, w: [ref[:, i * w:(i + 1) * w] for i in range(n)]
    outs, new_states = _gdn_blocks(
        cols(q_ref, n_qk, LANES), cols(k_ref, n_qk, LANES), [hh // rep for hh in heads],
        cols(v_ref, heads_per_step, dv), cols(z_ref, heads_per_step, dv),
        [g_ref[hh] for hh in heads], [b_ref[hh] for hh in heads], ng_ref[...],
        [[s_ref[gi, hh] for gi in range(n_groups)] for hh in heads], chunk)
    for hh in heads:
        o_ref[:, hh * dv:(hh + 1) * dv] = outs[hh].astype(o_ref.dtype)
        for gi in range(n_groups):
            snew_ref[gi, hh] = new_states[hh][gi]


def gdn_sample(qkv, proj, gates_t, norm_g, state, tp, n_kh, n_vh, dk, dv, chunk):
    rep = n_vh // n_kh
    t = qkv.shape[0]
    ts = t - tp
    assert dk == LANES and dv == LANES and ts % GDN_ROWS == 0 and tp % GDN_ROWS == 0
    hps = rep * _divisor(n_kh, GDN_SAMPLE_HEADS // rep)
    n_blocks = ts // GDN_ROWS
    spb = GDN_ROWS // chunk
    row_off = tp // GDN_ROWS
    conv_dim = 2 * n_kh * dk + n_vh * dv
    n_qk = hps // rep

    def arrange(gt):
        return gt[:, tp:].reshape(n_vh // hps, hps, n_blocks, 1, GDN_ROWS).transpose(0, 2, 1, 3, 4)

    beta_a = arrange(gates_t[:n_vh])
    g_a = arrange(gates_t[n_vh:])
    gate_spec = pl.BlockSpec((None, None, hps, 1, GDN_ROWS), lambda b, h: (h, b, 0, 0, 0))
    state_spec = pl.BlockSpec((spb, hps, dk, dv), lambda b, h: (b, h, 0, 0))
    return pl.pallas_call(
        functools.partial(_gdn_sample_kernel, heads_per_step=hps, rep=rep, chunk=chunk, dv=dv),
        grid=(n_blocks, n_vh // hps),
        in_specs=[pl.BlockSpec((GDN_ROWS, n_qk * dk), lambda b, h: (row_off + b, h)),
                  pl.BlockSpec((GDN_ROWS, n_qk * dk), lambda b, h: (row_off + b, n_kh // n_qk + h)),
                  pl.BlockSpec((GDN_ROWS, hps * dv), lambda b, h: (row_off + b, 2 * n_kh * dk // (hps * dv) + h)),
                  pl.BlockSpec((GDN_ROWS, hps * dv), lambda b, h: (row_off + b, conv_dim // (hps * dv) + h)),
                  gate_spec, gate_spec,
                  pl.BlockSpec((1, dv), lambda b, h: (0, 0)),
                  state_spec],
        out_specs=[pl.BlockSpec((GDN_ROWS, hps * dv), lambda b, h: (b, h)),
                   state_spec],
        out_shape=[jax.ShapeDtypeStruct((ts, n_vh * dv), BF16),
                   jax.ShapeDtypeStruct(state.shape, F32)],
        compiler_params=_params("parallel", "parallel"),
        name="gdn_sample",
    )(qkv, qkv, qkv, proj, g_a, beta_a, norm_g.reshape(1, dv), state)


def _router_kernel(x_ref, g_ref, w_ref, xn_ref, r_ref, *, n_groups, per_group):
    x = x_ref[...]
    xn = x * lax.rsqrt(jnp.mean(x * x, axis=-1, keepdims=True) + NORM_EPS) * g_ref[...]
    xn_ref[...] = xn.astype(xn_ref.dtype)
    logits = _dot3(xn, w_ref[...])
    lane = lax.broadcasted_iota(jnp.int32, logits.shape, 1).astype(F32)
    is_grp = lane < n_groups
    m = jnp.max(jnp.where(is_grp, logits, NEG_BIG), axis=-1, keepdims=True)
    e = jnp.where(is_grp, jnp.exp(jnp.where(is_grp, logits - m, 0.0)), 0.0)
    pg = e / jnp.sum(e, axis=-1, keepdims=True)
    pg_top = jnp.max(pg, axis=-1, keepdims=True)
    g_idx = jnp.min(jnp.where(is_grp & (pg == pg_top), lane, LANES), axis=-1, keepdims=True)
    elane = lane - n_groups
    in_grp = (elane >= g_idx * per_group) & (elane < (g_idx + 1.0) * per_group)
    m2 = jnp.max(jnp.where(in_grp, logits, NEG_BIG), axis=-1, keepdims=True)
    e2 = jnp.where(in_grp, jnp.exp(jnp.where(in_grp, logits - m2, 0.0)), 0.0)
    pe = e2 / jnp.sum(e2, axis=-1, keepdims=True)
    p1 = jnp.max(jnp.where(in_grp, pe, -1.0), axis=-1, keepdims=True)
    i1 = jnp.min(jnp.where(in_grp & (pe == p1), elane, LANES), axis=-1, keepdims=True)
    rest = in_grp & (elane != i1)
    p2 = jnp.max(jnp.where(rest, pe, -1.0), axis=-1, keepdims=True)
    i2 = jnp.min(jnp.where(rest & (pe == p2), elane, LANES), axis=-1, keepdims=True)
    denom = p1 + p2
    w1 = pg_top * p1 / denom
    w2 = pg_top * p2 / denom
    r_ref[...] = jnp.where(lane == 0, i1,
                           jnp.where(lane == 1, i2, jnp.where(lane == 2, w1, jnp.where(lane == 3, w2, 0.0))))


def moe_router(h, g, w_group, w_expert):
    t, d = h.shape
    n_groups = w_group.shape[1]
    n_exp = w_expert.shape[1]
    tm = _pick(t, 256)
    w_r = jnp.concatenate([w_group.astype(F32), w_expert.astype(F32),
                           jnp.zeros((d, LANES - n_groups - n_exp), F32)], axis=1)
    return pl.pallas_call(
        functools.partial(_router_kernel, n_groups=n_groups, per_group=n_exp // n_groups),
        grid=(t // tm,),
        in_specs=[pl.BlockSpec((tm, d), lambda i: (i, 0)),
                  pl.BlockSpec((1, d), lambda i: (0, 0)),
                  pl.BlockSpec((d, LANES), lambda i: (0, 0))],
        out_specs=[pl.BlockSpec((tm, d), lambda i: (i, 0)),
                   pl.BlockSpec((tm, LANES), lambda i: (i, 0))],
        out_shape=[jax.ShapeDtypeStruct((t, d), F32),
                   jax.ShapeDtypeStruct((t, LANES), F32)],
        compiler_params=_params("parallel"),
        name="moe_router",
    )(h, g.reshape(1, d), w_r)


def _row_gather_start(idx_ref, base, n_rows, src_hbm, dst, sem, inline=False):
    def body(r, carry):
        pltpu.make_async_copy(src_hbm.at[pl.ds(idx_ref[base + r], 1), :], dst.at[pl.ds(r, 1), :], sem).start()
        return carry
    if inline:
        for r in range(n_rows):
            body(r, 0)
    else:
        lax.fori_loop(0, n_rows, body, 0, unroll=8)


def _row_gather_wait(n_rows, src_hbm, dst, sem):
    pltpu.make_async_copy(src_hbm.at[pl.ds(0, n_rows), :], dst, sem).wait()


def _ffn_kernel(te_ref, nu_ref, rt_ref, x_hbm, w1_ref, w3_ref, w2_ref, o_ref, xbuf, sem, w1b, w3b, w2b):
    i = pl.program_id(0)
    n_used = nu_ref[0]

    @pl.when(i >= n_used)
    def _():
        o_ref[...] = jnp.zeros_like(o_ref)

    @pl.when(i < n_used)
    def _():
        slot = i % 2

        @pl.when(i == 0)
        def _():
            _row_gather_start(rt_ref, 0, MOE_TILE, x_hbm, xbuf.at[0], sem.at[0])

        prev = te_ref[jnp.maximum(i - 1, 0)]

        @pl.when((i == 0) | (te_ref[i] != prev))
        def _():
            w1b[...] = w1_ref[...].astype(BF16)
            w3b[...] = w3_ref[...].astype(BF16)
            w2b[...] = w2_ref[...].astype(BF16)

        _row_gather_wait(MOE_TILE, x_hbm, xbuf.at[slot], sem.at[slot])
        nxt = jnp.where(i + 1 < n_used, i + 1, 0)
        _row_gather_start(rt_ref, nxt * MOE_TILE, MOE_TILE, x_hbm, xbuf.at[1 - slot], sem.at[1 - slot], inline=True)
        x = xbuf[slot].astype(BF16)
        h1 = jnp.dot(x, w1b[...], preferred_element_type=F32)
        h3 = jnp.dot(x, w3b[...], preferred_element_type=F32)
        hdn = (_silu(h1) * h3).astype(BF16)
        o_ref[...] = jnp.dot(hdn, w2b[...], preferred_element_type=F32).astype(o_ref.dtype)

        @pl.when(i == n_used - 1)
        def _():
            _row_gather_wait(MOE_TILE, x_hbm, xbuf.at[1 - slot], sem.at[1 - slot])


def moe_ffn(xn, row_token, tile_expert, n_used, w1, w3, w2, layer):
    d = xn.shape[1]
    r = row_token.shape[0]
    ff = w1.shape[3]
    n_tiles = r // MOE_TILE
    grid_spec = pltpu.PrefetchScalarGridSpec(
        num_scalar_prefetch=3,
        grid=(n_tiles,),
        in_specs=[pl.BlockSpec(memory_space=pl.ANY),
                  pl.BlockSpec((None, None, d, ff), lambda i, te, nu, rt: (layer, te[i], 0, 0)),
                  pl.BlockSpec((None, None, d, ff), lambda i, te, nu, rt: (layer, te[i], 0, 0)),
                  pl.BlockSpec((None, None, ff, d), lambda i, te, nu, rt: (layer, te[i], 0, 0))],
        out_specs=pl.BlockSpec((MOE_TILE, d), lambda i, te, nu, rt: (i, 0)),
        scratch_shapes=[pltpu.VMEM((2, MOE_TILE, d), F32), pltpu.SemaphoreType.DMA((2,)),
                        pltpu.VMEM((d, ff), BF16), pltpu.VMEM((d, ff), BF16), pltpu.VMEM((ff, d), BF16)],
    )
    return pl.pallas_call(
        _ffn_kernel,
        grid_spec=grid_spec,
        out_shape=jax.ShapeDtypeStruct((r, d), F32),
        compiler_params=_params("arbitrary"),
        name="moe_ffn",
    )(tile_expert, n_used, row_token, xn, w1, w3, w2)


def _combine_kernel(dest_ref, h_ref, r_ref, y_hbm, o_ref, ybuf, sem, *, tm, n_tokens):
    i = pl.program_id(0)
    slot = i % 2

    def start(tile, to_slot):
        for k in range(MOE_TOP_K):
            _row_gather_start(dest_ref, k * n_tokens + tile * tm, tm, y_hbm, ybuf.at[to_slot, k], sem.at[to_slot])

    @pl.when(i == 0)
    def _():
        start(0, 0)

    @pl.when(i + 1 < pl.num_programs(0))
    def _():
        start(i + 1, 1 - slot)

    for k in range(MOE_TOP_K):
        _row_gather_wait(tm, y_hbm, ybuf.at[slot, k], sem.at[slot])
    r = r_ref[...]
    o_ref[...] = h_ref[...] + r[:, 2:3] * ybuf[slot, 0] + r[:, 3:4] * ybuf[slot, 1]


def moe_combine(h, y_sorted, dest, route):
    t, d = h.shape
    tm = _pick(t, 256)
    spec = pl.BlockSpec((tm, d), lambda i, dest: (i, 0))
    grid_spec = pltpu.PrefetchScalarGridSpec(
        num_scalar_prefetch=1,
        grid=(t // tm,),
        in_specs=[spec, pl.BlockSpec((tm, LANES), lambda i, dest: (i, 0)), pl.BlockSpec(memory_space=pl.ANY)],
        out_specs=spec,
        scratch_shapes=[pltpu.VMEM((2, MOE_TOP_K, tm, d), F32), pltpu.SemaphoreType.DMA((2,))],
    )
    return pl.pallas_call(
        functools.partial(_combine_kernel, tm=tm, n_tokens=t),
        grid_spec=grid_spec,
        out_shape=jax.ShapeDtypeStruct((t, d), F32),
        compiler_params=_params("arbitrary"),
        name="moe_combine",
    )(dest, h, route, y_sorted)


def hier_moe_layer(h, g, w_group, w_expert, w1, w3, w2, layer):
    t, d = h.shape
    n_exp = w1.shape[1]
    xn, route = moe_router(h, g, w_group, w_expert)
    e_flat = jnp.concatenate([route[:, 0], route[:, 1]]).astype(jnp.int32)
    onehot = (e_flat[:, None] == jnp.arange(n_exp, dtype=jnp.int32)[None, :]).astype(jnp.int32)
    counts = jnp.sum(onehot, axis=0)
    rank = jnp.sum((jnp.cumsum(onehot, axis=0) - onehot) * onehot, axis=1)
    padded = ((counts + MOE_TILE - 1) // MOE_TILE) * MOE_TILE
    ends = jnp.cumsum(padded)
    dest = jnp.sum(onehot * (ends - padded)[None, :], axis=1) + rank
    n_rows = MOE_TOP_K * t + n_exp * MOE_TILE
    n_rows = ((n_rows + MOE_TILE - 1) // MOE_TILE) * MOE_TILE
    token = jnp.concatenate([jnp.arange(t, dtype=jnp.int32)] * MOE_TOP_K)
    row_token = jnp.zeros((n_rows,), jnp.int32).at[dest].set(token)
    tile_start = jnp.arange(n_rows // MOE_TILE, dtype=jnp.int32) * MOE_TILE
    tile_expert = jnp.minimum(jnp.sum((ends[None, :] <= tile_start[:, None]).astype(jnp.int32), axis=1), n_exp - 1)
    n_used = (ends[-1] // MOE_TILE).astype(jnp.int32).reshape(1)
    y_sorted = moe_ffn(xn, row_token, tile_expert, n_used, w1, w3, w2, layer)
    return moe_combine(h, y_sorted, dest, route)


def _rope_tables(positions, half):
    inv = ROPE_THETA ** (-np.arange(half, dtype=np.float64) / half)
    ang = positions.astype(np.float64)[:, None] * inv[None, :]
    cos, sin = np.cos(ang), np.sin(ang)
    return (jnp.asarray(np.concatenate([cos, cos], axis=1), F32),
            jnp.asarray(np.concatenate([-sin, sin], axis=1), F32))


def _rope(x, cos2, sin2):
    half = x.shape[-1] // 2
    return x * cos2 + jnp.concatenate([x[:, half:], x[:, :half]], axis=-1) * sin2


def _kv_kernel(x_ref, w_ref, g_ref, cos_ref, sin_ref, ckv_ref, kpe_ref, wb_ref, *, rank):
    @pl.when(pl.program_id(0) == 0)
    def _():
        wb_ref[...] = w_ref[...].astype(BF16)

    c = jnp.dot(x_ref[...], wb_ref[...], preferred_element_type=F32)
    lat = c[:, :rank]
    ckv_ref[...] = lat * lax.rsqrt(jnp.mean(lat * lat, axis=-1, keepdims=True) + NORM_EPS) * g_ref[...]
    kpe_ref[...] = _rope(c[:, rank:], cos_ref[...], sin_ref[...])


def mla_shared_kv(xkv, w_dkv, latent_g, cos2, sin2, rank):
    t, d = xkv.shape
    rope_dim = w_dkv.shape[1] - rank
    tm = _pick(t, 512)
    return pl.pallas_call(
        functools.partial(_kv_kernel, rank=rank),
        grid=(t // tm,),
        in_specs=[pl.BlockSpec((tm, d), lambda i: (i, 0)),
                  pl.BlockSpec((d, rank + rope_dim), lambda i: (0, 0)),
                  pl.BlockSpec((1, rank), lambda i: (0, 0)),
                  pl.BlockSpec((tm, rope_dim), lambda i: (i, 0)),
                  pl.BlockSpec((tm, rope_dim), lambda i: (i, 0))],
        out_specs=[pl.BlockSpec((tm, rank), lambda i: (i, 0)),
                   pl.BlockSpec((tm, rope_dim), lambda i: (i, 0))],
        out_shape=[jax.ShapeDtypeStruct((t, rank), F32),
                   jax.ShapeDtypeStruct((t, rope_dim), F32)],
        scratch_shapes=[pltpu.VMEM((d, rank + rope_dim), BF16)],
        compiler_params=_params("arbitrary"),
        name="mla_shared_kv",
    )(xkv, w_dkv, latent_g.reshape(1, rank), cos2, sin2)


def _q_kernel(x_ref, w_ref, cos_ref, sin_ref, o_ref, *, nope, qk, scale, n_prompt_blocks):
    sc = jnp.where(pl.program_id(0) < n_prompt_blocks, scale * LOG2E, scale)
    q = _dot(x_ref[...], w_ref[...])
    cos2, sin2 = cos_ref[...], sin_ref[...]
    for hh in range(o_ref.shape[0]):
        qh = q[:, hh * qk:(hh + 1) * qk]
        o_ref[hh, :, :nope] = (qh[:, :nope] * sc).astype(o_ref.dtype)
        o_ref[hh, :, nope:] = (_rope(qh[:, nope:], cos2, sin2) * sc).astype(o_ref.dtype)


def mla_query(xn, w_q, cos2, sin2, n_heads, nope, scale, tp):
    t, d = xn.shape
    qk = w_q.shape[1] // n_heads
    rope_dim = qk - nope
    tm = _pick(math.gcd(tp, t - tp), 1024)
    hb = MLA_QUERY_HEADS if n_heads % MLA_QUERY_HEADS == 0 and (MLA_QUERY_HEADS * qk) % LANES == 0 else n_heads
    return pl.pallas_call(
        functools.partial(_q_kernel, nope=nope, qk=qk, scale=scale, n_prompt_blocks=tp // tm),
        grid=(t // tm, n_heads // hb),
        in_specs=[pl.BlockSpec((tm, d), lambda i, h: (i, 0)),
                  pl.BlockSpec((d, hb * qk), lambda i, h: (0, h)),
                  pl.BlockSpec((tm, rope_dim), lambda i, h: (i, 0)),
                  pl.BlockSpec((tm, rope_dim), lambda i, h: (i, 0))],
        out_specs=pl.BlockSpec((hb, tm, qk), lambda i, h: (h, i, 0)),
        out_shape=jax.ShapeDtypeStruct((n_heads, t, qk), BF16),
        compiler_params=_params("parallel", "arbitrary"),
        name="mla_query",
    )(xn, w_q, cos2, sin2)


def _kv_expand_kernel(ckv_ref, kpe_ref, wk_ref, wv_ref, k_ref, v_ref, *, nope):
    ckv = ckv_ref[...]
    k_ref[:, :nope] = _dot(ckv, wk_ref[...]).astype(k_ref.dtype)
    k_ref[:, nope:] = kpe_ref[...].astype(k_ref.dtype)
    v_ref[...] = _dot_nt(wv_ref[...], ckv).astype(v_ref.dtype)


def mla_expand_kv(ckv, kpe, w_uk_heads, w_uv_t_heads, tp):
    rank = ckv.shape[1]
    rope_dim = kpe.shape[1]
    n_heads, _, nope = w_uk_heads.shape
    dv = w_uv_t_heads.shape[1]
    tm = _pick(tp, 1024)
    return pl.pallas_call(
        functools.partial(_kv_expand_kernel, nope=nope),
        grid=(tp // tm, n_heads),
        in_specs=[pl.BlockSpec((tm, rank), lambda i, h: (i, 0)),
                  pl.BlockSpec((tm, rope_dim), lambda i, h: (i, 0)),
                  pl.BlockSpec((None, rank, nope), lambda i, h: (h, 0, 0)),
                  pl.BlockSpec((None, dv, rank), lambda i, h: (h, 0, 0))],
        out_specs=[pl.BlockSpec((None, tm, nope + rope_dim), lambda i, h: (h, i, 0)),
                   pl.BlockSpec((None, dv, tm), lambda i, h: (h, 0, i))],
        out_shape=[jax.ShapeDtypeStruct((n_heads, tp, nope + rope_dim), BF16),
                   jax.ShapeDtypeStruct((n_heads, dv, tp), BF16)],
        compiler_params=_params("parallel", "arbitrary"),
        name="mla_expand_kv",
    )(ckv, kpe, w_uk_heads, w_uv_t_heads)


def _flash_kernel(qi_ref, kj_ref, q_ref, k_ref, vt_ref, o_ref, m_scr, l_scr, acc_scr, *, tile):
    step = pl.program_id(1)
    qi = qi_ref[step]
    kj = kj_ref[step]

    @pl.when(kj == 0)
    def _():
        m_scr[...] = jnp.full_like(m_scr, NEG_BIG)
        l_scr[...] = jnp.zeros_like(l_scr)
        acc_scr[...] = jnp.zeros_like(acc_scr)

    st = lax.dot_general(k_ref[...], q_ref[...], (((1,), (1,)), ((), ())), preferred_element_type=F32)

    def update(st):
        m_prev = m_scr[...]
        m_new = jnp.maximum(m_prev, jnp.max(st, axis=0, keepdims=True))
        alpha = jnp.exp2(m_prev - m_new)
        p = jnp.exp2(st - m_new)
        l_scr[...] = alpha * l_scr[...] + jnp.sum(p, axis=0, keepdims=True)
        acc_scr[...] = alpha * acc_scr[...] + jnp.dot(vt_ref[...], p.astype(BF16), preferred_element_type=F32)
        m_scr[...] = m_new

    @pl.when(kj < qi)
    def _():
        update(st)

    @pl.when(kj == qi)
    def _():
        key = lax.broadcasted_iota(jnp.int32, st.shape, 0)
        qry = lax.broadcasted_iota(jnp.int32, st.shape, 1)
        update(jnp.where(key <= qry, st, NEG_BIG))
        o_ref[...] = (acc_scr[...] / l_scr[...]).T.astype(o_ref.dtype)


def mla_prompt_attention(q_heads, k_heads, vt_heads, tp):
    n_heads, _, qk = q_heads.shape
    dv = vt_heads.shape[1]
    tile = _pick(tp, ATTN_TILE)
    n_blocks = tp // tile
    pairs = [(i, j) for i in range(n_blocks) for j in range(i + 1)]
    qi_list = jnp.asarray([p[0] for p in pairs], jnp.int32)
    kj_list = jnp.asarray([p[1] for p in pairs], jnp.int32)
    grid_spec = pltpu.PrefetchScalarGridSpec(
        num_scalar_prefetch=2,
        grid=(n_heads, len(pairs)),
        in_specs=[pl.BlockSpec((None, tile, qk), lambda h, s, qi, kj: (h, qi[s], 0)),
                  pl.BlockSpec((None, tile, qk), lambda h, s, qi, kj: (h, kj[s], 0)),
                  pl.BlockSpec((None, dv, tile), lambda h, s, qi, kj: (h, 0, kj[s]))],
        out_specs=pl.BlockSpec((tile, dv), lambda h, s, qi, kj: (qi[s], h)),
        scratch_shapes=[pltpu.VMEM((1, tile), F32), pltpu.VMEM((1, tile), F32), pltpu.VMEM((dv, tile), F32)],
    )
    return pl.pallas_call(
        functools.partial(_flash_kernel, tile=tile),
        grid_spec=grid_spec,
        out_shape=jax.ShapeDtypeStruct((tp, n_heads * dv), BF16),
        compiler_params=_params("parallel", "arbitrary"),
        name="mla_prompt_attention",
    )(qi_list, kj_list, q_heads, k_heads, vt_heads)


def _absorb_kernel(q_ref, wk_ref, o_ref, *, nope):
    q = q_ref[...]
    rank = wk_ref.shape[0]
    o_ref[:, :rank] = _dot_nt(q[:, :nope], wk_ref[...]).astype(o_ref.dtype)
    o_ref[:, rank:] = q[:, nope:]


def mla_absorb_query(q_heads, w_uk_heads, tp):
    n_heads, t, qk = q_heads.shape
    _, rank, nope = w_uk_heads.shape
    ts = t - tp
    tm = _pick(math.gcd(ts, tp), 1024)
    off = tp // tm
    return pl.pallas_call(
        functools.partial(_absorb_kernel, nope=nope),
        grid=(n_heads, ts // tm),
        in_specs=[pl.BlockSpec((None, tm, qk), lambda h, i: (h, off + i, 0)),
                  pl.BlockSpec((None, rank, nope), lambda h, i: (h, 0, 0))],
        out_specs=pl.BlockSpec((None, tm, rank + qk - nope), lambda h, i: (h, i, 0)),
        out_shape=jax.ShapeDtypeStruct((n_heads, ts, rank + qk - nope), BF16),
        compiler_params=_params("parallel", "parallel"),
        name="mla_absorb_query",
    )(q_heads, w_uk_heads)


def _paged_kernel(pt_ref, q_ref, cn_ref, kn_ref, ckv_hbm, kpe_hbm, o_ref, m_scr, l_scr, acc_scr, ckbuf, kpbuf, sem,
                  ck_scr, kp_scr, *, pages_per_step, rank, dec_seq, page, n_sub):
    s_idx = pl.program_id(0)
    j = pl.program_id(1)
    n_j = pl.num_programs(1)
    step = s_idx * n_j + j
    n_steps = pl.num_programs(0) * n_j
    slot = step % 2

    def page_copies(seq, blk, to_slot):
        copies = []
        for pg in range(pages_per_step):
            pid = pt_ref[seq, blk * pages_per_step + pg]
            copies.append(pltpu.make_async_copy(
                ckv_hbm.at[pid], ckbuf.at[to_slot, pl.ds(pg * page, page), :], sem.at[to_slot]))
            copies.append(pltpu.make_async_copy(
                kpe_hbm.at[pid], kpbuf.at[to_slot, :, pl.ds(pg * page, page)], sem.at[to_slot]))
        return copies

    @pl.when(step == 0)
    def _():
        for cp in page_copies(s_idx, j, slot):
            cp.start()

    for cp in page_copies(s_idx, j, slot):
        cp.wait()

    n_heads = q_ref.shape[0]
    q = q_ref[...].reshape(n_heads * dec_seq, q_ref.shape[2])
    q_lat, q_pe = q[:, :rank], q[:, rank:]

    @pl.when(j == 0)
    def _():
        pad = lambda a: jnp.concatenate([a, jnp.zeros((LANES - dec_seq, a.shape[1]), a.dtype)], axis=0)
        cn = pad(cn_ref[...]).astype(BF16)
        s = _dot_nt(q_lat, cn) + _dot_nt(q_pe, pad(kn_ref[...]))
        qtok = lax.broadcasted_iota(jnp.int32, s.shape, 0) & (dec_seq - 1)
        ktok = lax.broadcasted_iota(jnp.int32, s.shape, 1)
        s = jnp.where(ktok <= qtok, s, NEG_BIG)
        m0 = jnp.max(s, axis=-1, keepdims=True)
        p = jnp.exp(s - m0)
        m_scr[...] = m0
        l_scr[...] = jnp.sum(p, axis=-1, keepdims=True)
        acc_scr[...] = jnp.dot(p.astype(BF16), cn, preferred_element_type=F32)

    nxt = jnp.where(step + 1 < n_steps, step + 1, 0)
    for cp in page_copies(nxt // n_j, nxt % n_j, 1 - slot):
        cp.start()

    for pg in range(pages_per_step):
        ck_scr[pg * page:(pg + 1) * page, :] = ckbuf[slot, pg * page:(pg + 1) * page, :].astype(BF16)
    kp_scr[...] = kpbuf[slot].astype(BF16)

    sub = pages_per_step * page // n_sub
    cks = [ck_scr[i * sub:(i + 1) * sub, :] for i in range(n_sub)]
    ss = [lax.dot_general(q_lat, cks[i], (((1,), (1,)), ((), ())), preferred_element_type=F32)
          + jnp.dot(q_pe, kp_scr[:, i * sub:(i + 1) * sub], preferred_element_type=F32) for i in range(n_sub)]
    ms = [jnp.max(s, axis=-1, keepdims=True) for s in ss]
    ps = [jnp.exp(s - m) for s, m in zip(ss, ms)]
    ls = [jnp.sum(p, axis=-1, keepdims=True) for p in ps]
    accs = [jnp.dot(p.astype(BF16), ck, preferred_element_type=F32) for p, ck in zip(ps, cks)]
    m_prev = m_scr[...]
    m_new = functools.reduce(jnp.maximum, ms, m_prev)
    alpha = jnp.exp(m_prev - m_new)
    l_new = alpha * l_scr[...]
    acc_new = alpha * acc_scr[...]
    for m, l, acc in zip(ms, ls, accs):
        coef = jnp.exp(m - m_new)
        l_new = l_new + coef * l
        acc_new = acc_new + coef * acc
    m_scr[...] = m_new
    l_scr[...] = l_new
    acc_scr[...] = acc_new

    @pl.when(j == n_j - 1)
    def _():
        o_ref[...] = (acc_new / l_new).reshape(o_ref.shape).astype(o_ref.dtype)

    @pl.when(step == n_steps - 1)
    def _():
        for cp in page_copies(0, 0, 1 - slot):
            cp.wait()


def mla_paged_attention(q_abs, ckv_new, kpe_new, cache_ckv, cache_kpe_t, page_table, dec_seq):
    n_heads, ts, key_dim = q_abs.shape
    n_seq, n_pages = page_table.shape
    _, page, rank = cache_ckv.shape
    rope_dim = cache_kpe_t.shape[1]
    pps = _divisor(n_pages, PAGES_PER_STEP)
    rows = n_heads * dec_seq
    keys = pps * page
    grid_spec = pltpu.PrefetchScalarGridSpec(
        num_scalar_prefetch=1,
        grid=(n_seq, n_pages // pps),
        in_specs=[pl.BlockSpec((n_heads, dec_seq, key_dim), lambda s, j, pt: (0, s, 0)),
                  pl.BlockSpec((dec_seq, rank), lambda s, j, pt: (s, 0)),
                  pl.BlockSpec((dec_seq, rope_dim), lambda s, j, pt: (s, 0)),
                  pl.BlockSpec(memory_space=pl.ANY),
                  pl.BlockSpec(memory_space=pl.ANY)],
        out_specs=pl.BlockSpec((n_heads, dec_seq, rank), lambda s, j, pt: (0, s, 0)),
        scratch_shapes=[pltpu.VMEM((rows, 1), F32), pltpu.VMEM((rows, 1), F32), pltpu.VMEM((rows, rank), F32),
                        pltpu.VMEM((2, keys, rank), F32), pltpu.VMEM((2, rope_dim, keys), F32),
                        pltpu.SemaphoreType.DMA((2,)),
                        pltpu.VMEM((keys, rank), BF16), pltpu.VMEM((rope_dim, keys), BF16)],
    )
    return pl.pallas_call(
        functools.partial(_paged_kernel, pages_per_step=pps, rank=rank, dec_seq=dec_seq, page=page,
                          n_sub=_divisor(pps, PAGED_SUB_BLOCKS)),
        grid_spec=grid_spec,
        out_shape=jax.ShapeDtypeStruct((n_heads, ts, rank), BF16),
        compiler_params=_params("arbitrary", "arbitrary"),
        name="mla_paged_attention",
    )(page_table, q_abs, ckv_new, kpe_new, cache_ckv, cache_kpe_t)


def _unabsorb_kernel(o_ref, wv_ref, out_ref):
    out_ref[...] = _dot(o_ref[...], wv_ref[...]).astype(out_ref.dtype)


def mla_unabsorb(o_lat, w_uv_heads):
    n_heads, ts, rank = o_lat.shape
    dv = w_uv_heads.shape[2]
    tm = _pick(ts, 1024)
    return pl.pallas_call(
        _unabsorb_kernel,
        grid=(n_heads, ts // tm),
        in_specs=[pl.BlockSpec((None, tm, rank), lambda h, i: (h, i, 0)),
                  pl.BlockSpec((None, rank, dv), lambda h, i: (h, 0, 0))],
        out_specs=pl.BlockSpec((tm, dv), lambda h, i: (i, h)),
        out_shape=jax.ShapeDtypeStruct((ts, n_heads * dv), BF16),
        compiler_params=_params("parallel", "parallel"),
        name="mla_unabsorb",
    )(o_lat, w_uv_heads)


def kernel(x_prompt, x_sample, state_ssm, state_conv, cache_ckv, cache_kpe, page_table, norm_mix, norm_ffn,
           norm_final, gdn_w_in, gdn_conv_w, gdn_a_log, gdn_dt_bias, gdn_norm, gdn_w_out, kv_norm, w_dkv,
           kv_latent_norm, w_uk, w_uv, mla_w_q, mla_w_o, moe_w_group, moe_w_expert, moe_w1, moe_w3, moe_w2):
    batch, seq, d = x_prompt.shape
    n_seq, dec_seq, _ = x_sample.shape
    assert batch == 1 and dec_seq == SUBLANES
    tp, ts = seq, n_seq * dec_seq
    depth = norm_mix.shape[0]
    n_a = state_ssm.shape[0]
    n_vh, dk, dv = state_ssm.shape[2:]
    conv_dim = state_conv.shape[3]
    qk_dim = (conv_dim - n_vh * dv) // 2
    n_kh = qk_dim // dk
    n_taps = gdn_conv_w.shape[1]
    rank = kv_latent_norm.shape[0]
    rope_dim = w_dkv.shape[1] - rank
    n_heads, nope = w_uk.shape[1:]
    past_len = page_table.shape[1] * cache_ckv.shape[1]
    scale = float((nope + rope_dim) ** -0.5)

    h = jnp.concatenate([x_prompt.reshape(tp, d), x_sample.reshape(ts, d)], axis=0)
    positions = np.concatenate([np.arange(tp), past_len + np.arange(dec_seq)[None, :].repeat(n_seq, 0).reshape(-1)])
    cos2, sin2 = _rope_tables(positions, rope_dim // 2)

    ssm_p, conv_p, ssm_s, conv_s = [], [], [], []
    ckv = kpe = None
    for i in range(depth):
        if i < n_a:
            xn = rmsnorm(h, norm_mix[i], BF16)
            w_in = gdn_w_in[i]
            main_cols = conv_dim + n_vh * dv
            proj = matmul(xn, w_in, F32, n_cols=main_cols)
            gates = gdn_gates(xn, w_in[:, main_cols:], gdn_a_log[i], gdn_dt_bias[i])
            hist = jnp.pad(state_conv[i], ((0, 0), (SUBLANES - (n_taps - 1), 0), (0, 0)))
            qkv = gdn_conv(proj, hist, gdn_conv_w[i], tp, qk_dim, conv_dim, dk)
            gates_t = gates.T
            o_p, s_p = gdn_prompt(qkv, proj, gates_t, gdn_norm[i], tp, n_kh, n_vh, dk, dv, GDN_CHUNK)
            o_s, s_s = gdn_sample(qkv, proj, gates_t, gdn_norm[i], state_ssm[i], tp, n_kh, n_vh, dk, dv, dec_seq)
            h = matmul((o_p, o_s), gdn_w_out[i], F32, res=h, tm_pref=512, tn_pref=512)
            ssm_p.append(s_p[None])
            ssm_s.append(s_s)
            conv_p.append(proj[tp - (n_taps - 1):tp, :conv_dim][None])
            conv_s.append(proj[tp:, :conv_dim].reshape(n_seq, dec_seq, conv_dim)[:, dec_seq - (n_taps - 1):])
        else:
            if i == n_a:
                ckv, kpe = mla_shared_kv(rmsnorm(h, kv_norm, BF16), w_dkv, kv_latent_norm, cos2, sin2, rank)
                w_uk_heads = jnp.transpose(w_uk, (1, 0, 2))
                w_uv_heads = jnp.transpose(w_uv, (1, 0, 2))
                k_heads, vt_heads = mla_expand_kv(ckv, kpe, w_uk_heads, jnp.transpose(w_uv, (1, 2, 0)), tp)
                cache_kpe_t = jnp.swapaxes(cache_kpe, 1, 2)
            j = i - n_a
            xn = rmsnorm(h, norm_mix[i], BF16)
            q_heads = mla_query(xn, mla_w_q[j], cos2, sin2, n_heads, nope, scale, tp)
            o_p = mla_prompt_attention(q_heads, k_heads, vt_heads, tp)
            q_abs = mla_absorb_query(q_heads, w_uk_heads, tp)
            o_lat = mla_paged_attention(q_abs, ckv[tp:], kpe[tp:], cache_ckv, cache_kpe_t, page_table, dec_seq)
            o_s = mla_unabsorb(o_lat, w_uv_heads)
            h = matmul((o_p, o_s), mla_w_o[j], F32, res=h, tm_pref=512)
        h = hier_moe_layer(h, norm_ffn[i], moe_w_group[i], moe_w_expert[i], moe_w1, moe_w3, moe_w2, i)

    y_prompt = rmsnorm(h, norm_final, F32, 0, tp).reshape(batch, seq, d)
    y_sample = rmsnorm(h, norm_final, F32, tp, ts).reshape(n_seq, dec_seq, d)
    return (y_prompt, y_sample,
            jnp.stack(ssm_p), jnp.stack(conv_p),
            ckv[:tp].reshape(batch, seq, rank), kpe[:tp].reshape(batch, seq, rope_dim),
            jnp.stack(ssm_s), jnp.stack(conv_s),
            ckv[tp:].reshape(n_seq, dec_seq, rank), kpe[tp:].reshape(n_seq, dec_seq, rope_dim))
```

```python
import functools
import math

import numpy as np
import jax
import jax.numpy as jnp
from jax import lax
from jax.experimental import pallas as pl
from jax.experimental.pallas import tpu as pltpu

F32 = jnp.float32
BF16 = jnp.bfloat16

NORM_EPS = 1e-6
ROPE_THETA = 10000.0
GDN_CHUNK = 64
MOE_TOP_K = 2
SUBLANES = 8
LANES = 128
VMEM_LIMIT_BYTES = 56 * 1024 * 1024
GDN_ROWS = 128
GDN_PROMPT_HEADS = 16
GDN_SAMPLE_HEADS = 4
MOE_TILE = 256
ATTN_TILE = 1024
MLA_QUERY_HEADS = 4
PAGES_PER_STEP = 32
PAGED_SUB_BLOCKS = 4
NEG_BIG = -1e30
LOG2E = math.log2(math.e)


def _params(*sem):
    return pltpu.CompilerParams(dimension_semantics=sem, vmem_limit_bytes=VMEM_LIMIT_BYTES)


def _pick(n, pref):
    t = min(pref, n)
    while n % t:
        t -= SUBLANES
    return t


def _divisor(n, pref):
    t = min(pref, n)
    while n % t:
        t -= 1
    return t


def _dot(a, b):
    return jnp.dot(a.astype(BF16), b.astype(BF16), preferred_element_type=F32)


def _dot_nt(a, b):
    return lax.dot_general(a.astype(BF16), b.astype(BF16), (((1,), (1,)), ((), ())),
                           preferred_element_type=F32)


def _split(a):
    hi = a.astype(BF16)
    lo = (a - hi.astype(F32)).astype(BF16)
    return hi, lo


def _dot3(a, b):
    ah, al = _split(a)
    bh, bl = _split(b)
    d = lambda x, y: jnp.dot(x, y, preferred_element_type=F32)
    return d(ah, bh) + (d(ah, bl) + d(al, bh))


def _sigmoid(x):
    return 1.0 / (1.0 + jnp.exp(-x))


def _silu(x):
    return x * _sigmoid(x)


def _rmsnorm_kernel(x_ref, g_ref, o_ref):
    x = x_ref[...]
    y = x * lax.rsqrt(jnp.mean(x * x, axis=-1, keepdims=True) + NORM_EPS) * g_ref[...]
    o_ref[...] = y.astype(o_ref.dtype)


def rmsnorm(x, g, out_dtype, row_start=0, rows=None):
    t, d = x.shape
    rows = t if rows is None else rows
    tm = _pick(math.gcd(rows, row_start) if row_start else rows, 512)
    off = row_start // tm
    return pl.pallas_call(
        _rmsnorm_kernel,
        grid=(rows // tm,),
        in_specs=[pl.BlockSpec((tm, d), lambda i: (i + off, 0)),
                  pl.BlockSpec((1, d), lambda i: (0, 0))],
        out_specs=pl.BlockSpec((tm, d), lambda i: (i, 0)),
        out_shape=jax.ShapeDtypeStruct((rows, d), out_dtype),
        compiler_params=_params("parallel"),
        name="rmsnorm",
    )(x, g.reshape(1, d))


def _matmul_kernel(*refs, has_res, n_first):
    n_x = 1 if n_first is None else 2
    xa_ref, xb_ref, w_ref = refs[0], refs[n_x - 1], refs[n_x]
    if has_res:
        r_ref, o_ref, wb_ref = refs[n_x + 1:]
    else:
        o_ref, wb_ref = refs[n_x + 1:]
    i = pl.program_id(1)

    @pl.when(i == 0)
    def _():
        wb_ref[...] = w_ref[...].astype(BF16)

    def emit(x_ref):
        acc = jnp.dot(x_ref[...], wb_ref[...], preferred_element_type=F32)
        if has_res:
            acc = acc + r_ref[...]
        o_ref[...] = acc.astype(o_ref.dtype)

    if n_first is None:
        emit(xa_ref)
    else:
        @pl.when(i < n_first)
        def _():
            emit(xa_ref)

        @pl.when(i >= n_first)
        def _():
            emit(xb_ref)


def matmul(x, w, out_dtype, n_cols=None, res=None, tm_pref=1024, tn_pref=1024):
    xs = x if isinstance(x, tuple) else (x,)
    ma, k = xs[0].shape
    m = sum(a.shape[0] for a in xs)
    n = w.shape[1] if n_cols is None else n_cols
    tm = _pick(math.gcd(ma, m - ma) if m > ma else m, tm_pref)
    tn = _pick(n, tn_pref)
    na = ma // tm if len(xs) == 2 else None
    if na is None:
        in_specs = [pl.BlockSpec((tm, k), lambda j, i: (i, 0))]
    else:
        in_specs = [pl.BlockSpec((tm, k), lambda j, i: (jnp.minimum(i, na - 1), 0)),
                    pl.BlockSpec((tm, k), lambda j, i: (jnp.maximum(i - na, 0), 0))]
    in_specs.append(pl.BlockSpec((k, tn), lambda j, i: (0, j)))
    args = [*xs, w]
    if res is not None:
        in_specs.append(pl.BlockSpec((tm, tn), lambda j, i: (i, j)))
        args.append(res)
    return pl.pallas_call(
        functools.partial(_matmul_kernel, has_res=res is not None, n_first=na),
        grid=(n // tn, m // tm),
        in_specs=in_specs,
        out_specs=pl.BlockSpec((tm, tn), lambda j, i: (i, j)),
        out_shape=jax.ShapeDtypeStruct((m, n), out_dtype),
        scratch_shapes=[pltpu.VMEM((k, tn), BF16)],
        compiler_params=_params("parallel", "arbitrary"),
        name="matmul",
    )(*args)


def _gates_kernel(x_ref, w_ref, a_ref, dt_ref, o_ref, *, n_heads):
    ba = _dot(x_ref[...], w_ref[...])
    lane = lax.broadcasted_iota(jnp.int32, ba.shape, 1)
    beta = _sigmoid(ba)
    xs = ba + dt_ref[...]
    softplus = jnp.maximum(xs, 0.0) + jnp.log(1.0 + jnp.exp(-jnp.abs(xs)))
    g = -jnp.exp(a_ref[...]) * softplus
    o_ref[...] = jnp.where(lane < n_heads, beta, g)


def gdn_gates(xn, w_ba, a_log, dt_bias):
    t, d = xn.shape
    hv = a_log.shape[0]
    tm = _pick(t, 512)
    zeros = jnp.zeros((hv,), F32)
    a_pad = jnp.concatenate([zeros, a_log.astype(F32)]).reshape(1, 2 * hv)
    dt_pad = jnp.concatenate([zeros, dt_bias.astype(F32)]).reshape(1, 2 * hv)
    return pl.pallas_call(
        functools.partial(_gates_kernel, n_heads=hv),
        grid=(t // tm,),
        in_specs=[pl.BlockSpec((tm, d), lambda i: (i, 0)),
                  pl.BlockSpec((d, 2 * hv), lambda i: (0, 0)),
                  pl.BlockSpec((1, 2 * hv), lambda i: (0, 0)),
                  pl.BlockSpec((1, 2 * hv), lambda i: (0, 0))],
        out_specs=pl.BlockSpec((tm, 2 * hv), lambda i: (i, 0)),
        out_shape=jax.ShapeDtypeStruct((t, 2 * hv), F32),
        compiler_params=_params("parallel"),
        name="gdn_gates",
    )(xn, w_ba, a_pad, dt_pad)


def _conv_kernel(x_ref, prev_ref, st_ref, w_ref, o_ref, *, n_prompt_blocks, n_q_blocks, n_qk_blocks,
                 n_taps, head_dim):
    i = pl.program_id(0)
    j = pl.program_id(1)
    x3 = x_ref[...]
    prev = jnp.where(i == 0, 0.0, prev_ref[...])
    hist_prompt = jnp.concatenate([prev, x3[:-1]], axis=0)
    hist = jnp.where(i >= n_prompt_blocks, st_ref[...], hist_prompt)
    row = lax.broadcasted_iota(jnp.int32, x3.shape, 1)
    tap = lambda s: w_ref[n_taps - 1 - s:n_taps - s, :][None]
    acc = x3 * tap(0)
    for s in range(1, n_taps):
        shifted = jnp.where(row >= s, pltpu.roll(x3, s, 1), pltpu.roll(hist, s, 1))
        acc = acc + shifted * tap(s)
    act = _silu(acc).reshape(x3.shape[0] * SUBLANES, x3.shape[2])
    scale = jnp.where(j < n_q_blocks, head_dim ** -0.5, 1.0)
    is_qk = j < n_qk_blocks
    for c in range(act.shape[1] // head_dim):
        seg = act[:, c * head_dim:(c + 1) * head_dim]
        nrm = seg * (lax.rsqrt(jnp.sum(seg * seg, axis=-1, keepdims=True) + NORM_EPS) * scale)
        o_ref[:, c * head_dim:(c + 1) * head_dim] = jnp.where(is_qk, nrm, seg)


def gdn_conv(proj, state_hist, conv_w, tp, qk_dim, conv_dim, head_dim):
    t = proj.shape[0]
    n_taps = conv_w.shape[0]
    tm = _pick(math.gcd(tp, t - tp), 512)
    tc = _pick(qk_dim, 1024)
    tiles = tm // SUBLANES
    proj3 = proj.reshape(t // SUBLANES, SUBLANES, proj.shape[1])
    npb = tp // tm
    return pl.pallas_call(
        functools.partial(_conv_kernel, n_prompt_blocks=npb, n_q_blocks=qk_dim // tc,
                          n_qk_blocks=2 * qk_dim // tc, n_taps=n_taps, head_dim=head_dim),
        grid=(t // tm, conv_dim // tc),
        in_specs=[pl.BlockSpec((tiles, SUBLANES, tc), lambda i, j: (i, 0, j)),
                  pl.BlockSpec((1, SUBLANES, tc), lambda i, j: (jnp.maximum(i * tiles - 1, 0), 0, j)),
                  pl.BlockSpec((tiles, SUBLANES, tc), lambda i, j: (jnp.maximum(i - npb, 0), 0, j)),
                  pl.BlockSpec((n_taps, tc), lambda i, j: (0, j))],
        out_specs=pl.BlockSpec((tm, tc), lambda i, j: (i, j)),
        out_shape=jax.ShapeDtypeStruct((t, conv_dim), F32),
        compiler_params=_params("parallel", "parallel"),
        name="gdn_conv",
    )(proj3, proj3, state_hist, conv_w)


def _gdn_blocks(qs, ks, qk_of, vs, zs, g_rows, b_rows, norm_g, states, chunk):
    r = GDN_ROWS
    n_groups = r // chunk
    units = range(len(vs))
    ri = lax.broadcasted_iota(jnp.int32, (r, r), 0)
    ci = lax.broadcasted_iota(jnp.int32, (r, r), 1)
    shift = int(math.log2(chunk))
    rg = ri >> shift
    cg = ci >> shift
    same = rg == cg
    causal = same & (ri >= ci)
    strict = same & (ri > ci)
    eye = ri == ci
    last = ci == (rg << shift) + (chunk - 1)
    eye_f = jnp.where(eye, 1.0, 0.0)
    bcast = lambda a: jnp.broadcast_to(a, (r, r))

    gc_col = [jnp.sum(jnp.where(causal, bcast(g_rows[u]), 0.0), axis=1, keepdims=True) for u in units]
    gc_rb = [bcast(jnp.sum(jnp.where(eye, bcast(gc_col[u]), 0.0), axis=0, keepdims=True)) for u in units]
    b_col = [jnp.sum(jnp.where(eye, bcast(b_rows[u]), 0.0), axis=1, keepdims=True) for u in units]
    glast_col = [jnp.sum(jnp.where(last, gc_rb[u], 0.0), axis=1, keepdims=True) for u in units]
    decay = [jnp.where(causal, jnp.exp(jnp.where(causal, gc_col[u] - gc_rb[u], 0.0)), 0.0) for u in units]

    kk = [_dot_nt(k, k) for k in ks]
    qk = [_dot_nt(q, k) for q, k in zip(qs, ks)]
    lmat = [jnp.where(strict, b_col[u] * kk[qk_of[u]] * decay[u], 0.0) for u in units]
    tmat = [eye_f - lmat[u] for u in units]
    pw = lmat
    for _ in range(shift - 1):
        pw = [_dot(pw[u], pw[u]) for u in units]
        tmat = [tmat[u] + _dot(tmat[u], pw[u]) for u in units]

    eg_col = [jnp.exp(gc_col[u]) for u in units]
    uu = [_dot(tmat[u], vs[u] * b_col[u]) for u in units]
    ww = [_dot(tmat[u], ks[qk_of[u]] * (b_col[u] * eg_col[u])) for u in units]
    qg = [qs[qk_of[u]] * eg_col[u] for u in units]
    amat = [jnp.where(causal, qk[qk_of[u]] * decay[u], 0.0) for u in units]

    ws_qs = [[_dot(jnp.concatenate([ww[u][gi * chunk:(gi + 1) * chunk], qg[u][gi * chunk:(gi + 1) * chunk]], axis=0),
                   states[u][gi]) for gi in range(n_groups)] for u in units]
    vn = [uu[u] - jnp.concatenate([ws_qs[u][gi][:chunk] for gi in range(n_groups)], axis=0) for u in units]
    o = [jnp.concatenate([ws_qs[u][gi][chunk:] for gi in range(n_groups)], axis=0) + _dot(amat[u], vn[u])
         for u in units]

    kd_t = [(ks[qk_of[u]] * jnp.exp(glast_col[u] - gc_col[u])).T for u in units]
    new_states = [[states[u][gi] * jnp.exp(glast_col[u][gi * chunk:gi * chunk + 1, :])
                   + _dot(jnp.where(cg == gi, kd_t[u], 0.0), vn[u]) for gi in range(n_groups)] for u in units]

    outs = [o[u] * lax.rsqrt(jnp.mean(o[u] * o[u], axis=-1, keepdims=True) + NORM_EPS) * norm_g * _silu(zs[u])
            for u in units]
    return outs, new_states


def _gdn_prompt_kernel(q_ref, k_ref, v_ref, z_ref, g_ref, b_ref, ng_ref, o_ref, sfin_ref, s_scr, *,
                       heads_per_step, chunk, dv):
    c = pl.program_id(1)
    rep = GDN_ROWS // chunk

    @pl.when(c == 0)
    def _():
        s_scr[...] = jnp.zeros_like(s_scr)

    heads = range(heads_per_step)
    tile = lambda ref, hh: jnp.concatenate([ref[:, hh * LANES:(hh + 1) * LANES]] * rep, axis=0)
    stack = lambda ref, hh: jnp.concatenate(
        [ref[:, (hh * rep + e) * dv:(hh * rep + e + 1) * dv] for e in range(rep)], axis=0)
    outs, new_states = _gdn_blocks(
        [tile(q_ref, hh) for hh in heads], [tile(k_ref, hh) for hh in heads], list(heads),
        [stack(v_ref, hh) for hh in heads], [stack(z_ref, hh) for hh in heads],
        [g_ref[hh] for hh in heads], [b_ref[hh] for hh in heads], ng_ref[...],
        [[s_scr[hh * rep + e] for e in range(rep)] for hh in heads], chunk)
    for hh in heads:
        for e in range(rep):
            s_scr[hh * rep + e] = new_states[hh][e]
            o_ref[:, (hh * rep + e) * dv:(hh * rep + e + 1) * dv] = (
                outs[hh][e * chunk:(e + 1) * chunk].astype(o_ref.dtype))

    @pl.when(c == pl.num_programs(1) - 1)
    def _():
        sfin_ref[...] = s_scr[...]


def gdn_prompt(qkv, proj, gates_t, norm_g, tp, n_kh, n_vh, dk, dv, chunk):
    rep = n_vh // n_kh
    assert rep * chunk == GDN_ROWS and dk == LANES and dv == LANES
    hps = _divisor(n_kh, GDN_PROMPT_HEADS)
    n_chunks = tp // chunk
    qk_blocks = n_kh // hps
    conv_dim = 2 * n_kh * dk + n_vh * dv

    def arrange(gt):
        a = gt[:, :tp].reshape(qk_blocks, hps, rep, n_chunks, chunk)
        return a.transpose(0, 3, 1, 2, 4).reshape(qk_blocks, n_chunks, hps, 1, GDN_ROWS)

    beta_a = arrange(gates_t[:n_vh])
    g_a = arrange(gates_t[n_vh:])
    vw = hps * rep * dv
    v_off = 2 * n_kh * dk // vw
    z_off = conv_dim // vw
    gate_spec = pl.BlockSpec((None, None, hps, 1, GDN_ROWS), lambda h, c: (h, c, 0, 0, 0))
    return pl.pallas_call(
        functools.partial(_gdn_prompt_kernel, heads_per_step=hps, chunk=chunk, dv=dv),
        grid=(qk_blocks, n_chunks),
        in_specs=[pl.BlockSpec((chunk, hps * dk), lambda h, c: (c, h)),
                  pl.BlockSpec((chunk, hps * dk), lambda h, c: (c, qk_blocks + h)),
                  pl.BlockSpec((chunk, vw), lambda h, c: (c, v_off + h)),
                  pl.BlockSpec((chunk, vw), lambda h, c: (c, z_off + h)),
                  gate_spec, gate_spec,
                  pl.BlockSpec((1, dv), lambda h, c: (0, 0))],
        out_specs=[pl.BlockSpec((chunk, vw), lambda h, c: (c, h)),
                   pl.BlockSpec((hps * rep, dk, dv), lambda h, c: (h, 0, 0))],
        out_shape=[jax.ShapeDtypeStruct((tp, n_vh * dv), BF16),
                   jax.ShapeDtypeStruct((n_vh, dk, dv), F32)],
        scratch_shapes=[pltpu.VMEM((hps * rep, dk, dv), F32)],
        compiler_params=_params("parallel", "arbitrary"),
        name="gdn_prompt",
    )(qkv, qkv, qkv, proj, g_a, beta_a, norm_g.reshape(1, dv))


def _gdn_sample_kernel(q_ref, k_ref, v_ref, z_ref, g_ref, b_ref, ng_ref, s_ref, o_ref, snew_ref, *,
                       heads_per_step, rep, chunk, dv):
    n_groups = GDN_ROWS // chunk
    n_qk = heads_per_step // rep
    heads = range(heads_per_step)
    cols = lambda ref, n, w: [ref[:, i * w:(i + 1) * w] for i in range(n)]
    outs, new_states = _gdn_blocks(
        cols(q_ref, n_qk, LANES), cols(k_ref, n_qk, LANES), [hh // rep for hh in heads],
        cols(v_ref, heads_per_step, dv), cols(z_ref, heads_per_step, dv),
        [g_ref[hh] for hh in heads], [b_ref[hh] for hh in heads], ng_ref[...],
        [[s_ref[gi, hh] for gi in range(n_groups)] for hh in heads], chunk)
    for hh in heads:
        o_ref[:, hh * dv:(hh + 1) * dv] = outs[hh].astype(o_ref.dtype)
        for gi in range(n_groups):
            snew_ref[gi, hh] = new_states[hh][gi]


def gdn_sample(qkv, proj, gates_t, norm_g, state, tp, n_kh, n_vh, dk, dv, chunk):
    rep = n_vh // n_kh
    t = qkv.shape[0]
    ts = t - tp
    assert dk == LANES and dv == LANES and ts % GDN_ROWS == 0 and tp % GDN_ROWS == 0
    hps = rep * _divisor(n_kh, GDN_SAMPLE_HEADS // rep)
    n_blocks = ts // GDN_ROWS
    spb = GDN_ROWS // chunk
    row_off = tp // GDN_ROWS
    conv_dim = 2 * n_kh * dk + n_vh * dv
    n_qk = hps // rep

    def arrange(gt):
        return gt[:, tp:].reshape(n_vh // hps, hps, n_blocks, 1, GDN_ROWS).transpose(0, 2, 1, 3, 4)

    beta_a = arrange(gates_t[:n_vh])
    g_a = arrange(gates_t[n_vh:])
    gate_spec = pl.BlockSpec((None, None, hps, 1, GDN_ROWS), lambda b, h: (h, b, 0, 0, 0))
    state_spec = pl.BlockSpec((spb, hps, dk, dv), lambda b, h: (b, h, 0, 0))
    return pl.pallas_call(
        functools.partial(_gdn_sample_kernel, heads_per_step=hps, rep=rep, chunk=chunk, dv=dv),
        grid=(n_blocks, n_vh // hps),
        in_specs=[pl.BlockSpec((GDN_ROWS, n_qk * dk), lambda b, h: (row_off + b, h)),
                  pl.BlockSpec((GDN_ROWS, n_qk * dk), lambda b, h: (row_off + b, n_kh // n_qk + h)),
                  pl.BlockSpec((GDN_ROWS, hps * dv), lambda b, h: (row_off + b, 2 * n_kh * dk // (hps * dv) + h)),
                  pl.BlockSpec((GDN_ROWS, hps * dv), lambda b, h: (row_off + b, conv_dim // (hps * dv) + h)),
                  gate_spec, gate_spec,
                  pl.BlockSpec((1, dv), lambda b, h: (0, 0)),
                  state_spec],
        out_specs=[pl.BlockSpec((GDN_ROWS, hps * dv), lambda b, h: (b, h)),
                   state_spec],
        out_shape=[jax.ShapeDtypeStruct((ts, n_vh * dv), BF16),
                   jax.ShapeDtypeStruct(state.shape, F32)],
        compiler_params=_params("parallel", "parallel"),
        name="gdn_sample",
    )(qkv, qkv, qkv, proj, g_a, beta_a, norm_g.reshape(1, dv), state)


def _router_kernel(x_ref, g_ref, w_ref, xn_ref, r_ref, *, n_groups, per_group):
    x = x_ref[...]
    xn = x * lax.rsqrt(jnp.mean(x * x, axis=-1, keepdims=True) + NORM_EPS) * g_ref[...]
    xn_ref[...] = xn.astype(xn_ref.dtype)
    logits = _dot3(xn, w_ref[...])
    lane = lax.broadcasted_iota(jnp.int32, logits.shape, 1).astype(F32)
    is_grp = lane < n_groups
    m = jnp.max(jnp.where(is_grp, logits, NEG_BIG), axis=-1, keepdims=True)
    e = jnp.where(is_grp, jnp.exp(jnp.where(is_grp, logits - m, 0.0)), 0.0)
    pg = e / jnp.sum(e, axis=-1, keepdims=True)
    pg_top = jnp.max(pg, axis=-1, keepdims=True)
    g_idx = jnp.min(jnp.where(is_grp & (pg == pg_top), lane, LANES), axis=-1, keepdims=True)
    elane = lane - n_groups
    in_grp = (elane >= g_idx * per_group) & (elane < (g_idx + 1.0) * per_group)
    m2 = jnp.max(jnp.where(in_grp, logits, NEG_BIG), axis=-1, keepdims=True)
    e2 = jnp.where(in_grp, jnp.exp(jnp.where(in_grp, logits - m2, 0.0)), 0.0)
    pe = e2 / jnp.sum(e2, axis=-1, keepdims=True)
    p1 = jnp.max(jnp.where(in_grp, pe, -1.0), axis=-1, keepdims=True)
    i1 = jnp.min(jnp.where(in_grp & (pe == p1), elane, LANES), axis=-1, keepdims=True)
    rest = in_grp & (elane != i1)
    p2 = jnp.max(jnp.where(rest, pe, -1.0), axis=-1, keepdims=True)
    i2 = jnp.min(jnp.where(rest & (pe == p2), elane, LANES), axis=-1, keepdims=True)
    denom = p1 + p2
    w1 = pg_top * p1 / denom
    w2 = pg_top * p2 / denom
    r_ref[...] = jnp.where(lane == 0, i1,
                           jnp.where(lane == 1, i2, jnp.where(lane == 2, w1, jnp.where(lane == 3, w2, 0.0))))


def moe_router(h, g, w_group, w_expert):
    t, d = h.shape
    n_groups = w_group.shape[1]
    n_exp = w_expert.shape[1]
    tm = _pick(t, 256)
    w_r = jnp.concatenate([w_group.astype(F32), w_expert.astype(F32),
                           jnp.zeros((d, LANES - n_groups - n_exp), F32)], axis=1)
    return pl.pallas_call(
        functools.partial(_router_kernel, n_groups=n_groups, per_group=n_exp // n_groups),
        grid=(t // tm,),
        in_specs=[pl.BlockSpec((tm, d), lambda i: (i, 0)),
                  pl.BlockSpec((1, d), lambda i: (0, 0)),
                  pl.BlockSpec((d, LANES), lambda i: (0, 0))],
        out_specs=[pl.BlockSpec((tm, d), lambda i: (i, 0)),
                   pl.BlockSpec((tm, LANES), lambda i: (i, 0))],
        out_shape=[jax.ShapeDtypeStruct((t, d), F32),
                   jax.ShapeDtypeStruct((t, LANES), F32)],
        compiler_params=_params("parallel"),
        name="moe_router",
    )(h, g.reshape(1, d), w_r)


def _row_gather_start(idx_ref, base, n_rows, src_hbm, dst, sem, inline=False):
    def start(r, priority):
        pltpu.make_async_copy(src_hbm.at[pl.ds(idx_ref[base + r], 1), :], dst.at[pl.ds(r, 1), :],
                              sem).start(priority=priority)

    def body(pair, carry):
        start(2 * pair, 0)
        start(2 * pair + 1, 1)
        return carry
    if inline:
        for pair in range(n_rows // 2):
            body(pair, 0)
    else:
        lax.fori_loop(0, n_rows // 2, body, 0, unroll=4)


def _row_gather_wait(n_rows, src_hbm, dst, sem):
    pltpu.make_async_copy(src_hbm.at[pl.ds(0, n_rows), :], dst, sem).wait()


def _ffn_kernel(te_ref, nu_ref, rt_ref, x_hbm, w1_ref, w3_ref, w2_ref, o_ref, xbuf, sem, w1b, w3b, w2b):
    i = pl.program_id(0)
    n_used = nu_ref[0]

    @pl.when(i >= n_used)
    def _():
        o_ref[...] = jnp.zeros_like(o_ref)

    @pl.when(i < n_used)
    def _():
        slot = i % 2

        @pl.when(i == 0)
        def _():
            _row_gather_start(rt_ref, 0, MOE_TILE, x_hbm, xbuf.at[0], sem.at[0])

        prev = te_ref[jnp.maximum(i - 1, 0)]

        @pl.when((i == 0) | (te_ref[i] != prev))
        def _():
            w1b[...] = w1_ref[...].astype(BF16)
            w3b[...] = w3_ref[...].astype(BF16)
            w2b[...] = w2_ref[...].astype(BF16)

        _row_gather_wait(MOE_TILE, x_hbm, xbuf.at[slot], sem.at[slot])
        nxt = jnp.where(i + 1 < n_used, i + 1, 0)
        _row_gather_start(rt_ref, nxt * MOE_TILE, MOE_TILE, x_hbm, xbuf.at[1 - slot], sem.at[1 - slot], inline=True)
        x = xbuf[slot].astype(BF16)
        h1 = jnp.dot(x, w1b[...], preferred_element_type=F32)
        h3 = jnp.dot(x, w3b[...], preferred_element_type=F32)
        hdn = (_silu(h1) * h3).astype(BF16)
        o_ref[...] = jnp.dot(hdn, w2b[...], preferred_element_type=F32).astype(o_ref.dtype)

        @pl.when(i == n_used - 1)
        def _():
            _row_gather_wait(MOE_TILE, x_hbm, xbuf.at[1 - slot], sem.at[1 - slot])


def moe_ffn(xn, row_token, tile_expert, n_used, w1, w3, w2, layer):
    d = xn.shape[1]
    r = row_token.shape[0]
    ff = w1.shape[3]
    n_tiles = r // MOE_TILE
    grid_spec = pltpu.PrefetchScalarGridSpec(
        num_scalar_prefetch=3,
        grid=(n_tiles,),
        in_specs=[pl.BlockSpec(memory_space=pl.ANY),
                  pl.BlockSpec((None, None, d, ff), lambda i, te, nu, rt: (layer, te[i], 0, 0)),
                  pl.BlockSpec((None, None, d, ff), lambda i, te, nu, rt: (layer, te[i], 0, 0)),
                  pl.BlockSpec((None, None, ff, d), lambda i, te, nu, rt: (layer, te[i], 0, 0))],
        out_specs=pl.BlockSpec((MOE_TILE, d), lambda i, te, nu, rt: (i, 0)),
        scratch_shapes=[pltpu.VMEM((2, MOE_TILE, d), F32), pltpu.SemaphoreType.DMA((2,)),
                        pltpu.VMEM((d, ff), BF16), pltpu.VMEM((d, ff), BF16), pltpu.VMEM((ff, d), BF16)],
    )
    return pl.pallas_call(
        _ffn_kernel,
        grid_spec=grid_spec,
        out_shape=jax.ShapeDtypeStruct((r, d), F32),
        compiler_params=_params("arbitrary"),
        name="moe_ffn",
    )(tile_expert, n_used, row_token, xn, w1, w3, w2)


def _combine_kernel(dest_ref, h_ref, r_ref, y_hbm, o_ref, ybuf, sem, *, tm, n_tokens):
    i = pl.program_id(0)
    slot = i % 2

    def start(tile, to_slot):
        for k in range(MOE_TOP_K):
            _row_gather_start(dest_ref, k * n_tokens + tile * tm, tm, y_hbm, ybuf.at[to_slot, k], sem.at[to_slot])

    @pl.when(i == 0)
    def _():
        start(0, 0)

    @pl.when(i + 1 < pl.num_programs(0))
    def _():
        start(i + 1, 1 - slot)

    for k in range(MOE_TOP_K):
        _row_gather_wait(tm, y_hbm, ybuf.at[slot, k], sem.at[slot])
    r = r_ref[...]
    o_ref[...] = h_ref[...] + r[:, 2:3] * ybuf[slot, 0] + r[:, 3:4] * ybuf[slot, 1]


def moe_combine(h, y_sorted, dest, route):
    t, d = h.shape
    tm = _pick(t, 256)
    spec = pl.BlockSpec((tm, d), lambda i, dest: (i, 0))
    grid_spec = pltpu.PrefetchScalarGridSpec(
        num_scalar_prefetch=1,
        grid=(t // tm,),
        in_specs=[spec, pl.BlockSpec((tm, LANES), lambda i, dest: (i, 0)), pl.BlockSpec(memory_space=pl.ANY)],
        out_specs=spec,
        scratch_shapes=[pltpu.VMEM((2, MOE_TOP_K, tm, d), F32), pltpu.SemaphoreType.DMA((2,))],
    )
    return pl.pallas_call(
        functools.partial(_combine_kernel, tm=tm, n_tokens=t),
        grid_spec=grid_spec,
        out_shape=jax.ShapeDtypeStruct((t, d), F32),
        compiler_params=_params("arbitrary"),
        name="moe_combine",
    )(dest, h, route, y_sorted)


def hier_moe_layer(h, g, w_group, w_expert, w1, w3, w2, layer):
    t, d = h.shape
    n_exp = w1.shape[1]
    xn, route = moe_router(h, g, w_group, w_expert)
    e_flat = jnp.concatenate([route[:, 0], route[:, 1]]).astype(jnp.int32)
    onehot = (e_flat[:, None] == jnp.arange(n_exp, dtype=jnp.int32)[None, :]).astype(jnp.int32)
    counts = jnp.sum(onehot, axis=0)
    rank = jnp.sum((jnp.cumsum(onehot, axis=0) - onehot) * onehot, axis=1)
    padded = ((counts + MOE_TILE - 1) // MOE_TILE) * MOE_TILE
    ends = jnp.cumsum(padded)
    dest = jnp.sum(onehot * (ends - padded)[None, :], axis=1) + rank
    n_rows = MOE_TOP_K * t + n_exp * MOE_TILE
    n_rows = ((n_rows + MOE_TILE - 1) // MOE_TILE) * MOE_TILE
    token = jnp.concatenate([jnp.arange(t, dtype=jnp.int32)] * MOE_TOP_K)
    row_token = jnp.zeros((n_rows,), jnp.int32).at[dest].set(token)
    tile_start = jnp.arange(n_rows // MOE_TILE, dtype=jnp.int32) * MOE_TILE
    tile_expert = jnp.minimum(jnp.sum((ends[None, :] <= tile_start[:, None]).astype(jnp.int32), axis=1), n_exp - 1)
    n_used = (ends[-1] // MOE_TILE).astype(jnp.int32).reshape(1)
    y_sorted = moe_ffn(xn, row_token, tile_expert, n_used, w1, w3, w2, layer)
    return moe_combine(h, y_sorted, dest, route)


def _rope_tables(positions, half):
    inv = ROPE_THETA ** (-np.arange(half, dtype=np.float64) / half)
    ang = positions.astype(np.float64)[:, None] * inv[None, :]
    cos, sin = np.cos(ang), np.sin(ang)
    return (jnp.asarray(np.concatenate([cos, cos], axis=1), F32),
            jnp.asarray(np.concatenate([-sin, sin], axis=1), F32))


def _rope(x, cos2, sin2):
    half = x.shape[-1] // 2
    return x * cos2 + jnp.concatenate([x[:, half:], x[:, :half]], axis=-1) * sin2


def _kv_kernel(x_ref, w_ref, g_ref, cos_ref, sin_ref, ckv_ref, kpe_ref, wb_ref, *, rank):
    @pl.when(pl.program_id(0) == 0)
    def _():
        wb_ref[...] = w_ref[...].astype(BF16)

    c = jnp.dot(x_ref[...], wb_ref[...], preferred_element_type=F32)
    lat = c[:, :rank]
    ckv_ref[...] = lat * lax.rsqrt(jnp.mean(lat * lat, axis=-1, keepdims=True) + NORM_EPS) * g_ref[...]
    kpe_ref[...] = _rope(c[:, rank:], cos_ref[...], sin_ref[...])


def mla_shared_kv(xkv, w_dkv, latent_g, cos2, sin2, rank):
    t, d = xkv.shape
    rope_dim = w_dkv.shape[1] - rank
    tm = _pick(t, 512)
    return pl.pallas_call(
        functools.partial(_kv_kernel, rank=rank),
        grid=(t // tm,),
        in_specs=[pl.BlockSpec((tm, d), lambda i: (i, 0)),
                  pl.BlockSpec((d, rank + rope_dim), lambda i: (0, 0)),
                  pl.BlockSpec((1, rank), lambda i: (0, 0)),
                  pl.BlockSpec((tm, rope_dim), lambda i: (i, 0)),
                  pl.BlockSpec((tm, rope_dim), lambda i: (i, 0))],
        out_specs=[pl.BlockSpec((tm, rank), lambda i: (i, 0)),
                   pl.BlockSpec((tm, rope_dim), lambda i: (i, 0))],
        out_shape=[jax.ShapeDtypeStruct((t, rank), F32),
                   jax.ShapeDtypeStruct((t, rope_dim), F32)],
        scratch_shapes=[pltpu.VMEM((d, rank + rope_dim), BF16)],
        compiler_params=_params("arbitrary"),
        name="mla_shared_kv",
    )(xkv, w_dkv, latent_g.reshape(1, rank), cos2, sin2)


def _q_kernel(x_ref, w_ref, cos_ref, sin_ref, o_ref, *, nope, qk, scale, n_prompt_blocks):
    sc = jnp.where(pl.program_id(0) < n_prompt_blocks, scale * LOG2E, scale)
    q = _dot(x_ref[...], w_ref[...])
    cos2, sin2 = cos_ref[...], sin_ref[...]
    for hh in range(o_ref.shape[0]):
        qh = q[:, hh * qk:(hh + 1) * qk]
        o_ref[hh, :, :nope] = (qh[:, :nope] * sc).astype(o_ref.dtype)
        o_ref[hh, :, nope:] = (_rope(qh[:, nope:], cos2, sin2) * sc).astype(o_ref.dtype)


def mla_query(xn, w_q, cos2, sin2, n_heads, nope, scale, tp):
    t, d = xn.shape
    qk = w_q.shape[1] // n_heads
    rope_dim = qk - nope
    tm = _pick(math.gcd(tp, t - tp), 1024)
    hb = MLA_QUERY_HEADS if n_heads % MLA_QUERY_HEADS == 0 and (MLA_QUERY_HEADS * qk) % LANES == 0 else n_heads
    return pl.pallas_call(
        functools.partial(_q_kernel, nope=nope, qk=qk, scale=scale, n_prompt_blocks=tp // tm),
        grid=(t // tm, n_heads // hb),
        in_specs=[pl.BlockSpec((tm, d), lambda i, h: (i, 0)),
                  pl.BlockSpec((d, hb * qk), lambda i, h: (0, h)),
                  pl.BlockSpec((tm, rope_dim), lambda i, h: (i, 0)),
                  pl.BlockSpec((tm, rope_dim), lambda i, h: (i, 0))],
        out_specs=pl.BlockSpec((hb, tm, qk), lambda i, h: (h, i, 0)),
        out_shape=jax.ShapeDtypeStruct((n_heads, t, qk), BF16),
        compiler_params=_params("parallel", "arbitrary"),
        name="mla_query",
    )(xn, w_q, cos2, sin2)


def _kv_expand_kernel(ckv_ref, kpe_ref, wk_ref, wv_ref, k_ref, v_ref, *, nope):
    ckv = ckv_ref[...]
    k_ref[:, :nope] = _dot(ckv, wk_ref[...]).astype(k_ref.dtype)
    k_ref[:, nope:] = kpe_ref[...].astype(k_ref.dtype)
    v_ref[...] = _dot_nt(wv_ref[...], ckv).astype(v_ref.dtype)


def mla_expand_kv(ckv, kpe, w_uk_heads, w_uv_t_heads, tp):
    rank = ckv.shape[1]
    rope_dim = kpe.shape[1]
    n_heads, _, nope = w_uk_heads.shape
    dv = w_uv_t_heads.shape[1]
    tm = _pick(tp, 1024)
    return pl.pallas_call(
        functools.partial(_kv_expand_kernel, nope=nope),
        grid=(tp // tm, n_heads),
        in_specs=[pl.BlockSpec((tm, rank), lambda i, h: (i, 0)),
                  pl.BlockSpec((tm, rope_dim), lambda i, h: (i, 0)),
                  pl.BlockSpec((None, rank, nope), lambda i, h: (h, 0, 0)),
                  pl.BlockSpec((None, dv, rank), lambda i, h: (h, 0, 0))],
        out_specs=[pl.BlockSpec((None, tm, nope + rope_dim), lambda i, h: (h, i, 0)),
                   pl.BlockSpec((None, dv, tm), lambda i, h: (h, 0, i))],
        out_shape=[jax.ShapeDtypeStruct((n_heads, tp, nope + rope_dim), BF16),
                   jax.ShapeDtypeStruct((n_heads, dv, tp), BF16)],
        compiler_params=_params("parallel", "arbitrary"),
        name="mla_expand_kv",
    )(ckv, kpe, w_uk_heads, w_uv_t_heads)


def _flash_kernel(qi_ref, kj_ref, q_ref, k_ref, vt_ref, o_ref, m_scr, l_scr, acc_scr, *, tile):
    step = pl.program_id(1)
    qi = qi_ref[step]
    kj = kj_ref[step]

    @pl.when(kj == 0)
    def _():
        m_scr[...] = jnp.full_like(m_scr, NEG_BIG)
        l_scr[...] = jnp.zeros_like(l_scr)
        acc_scr[...] = jnp.zeros_like(acc_scr)

    st = lax.dot_general(k_ref[...], q_ref[...], (((1,), (1,)), ((), ())), preferred_element_type=F32)

    def update(st):
        m_prev = m_scr[...]
        m_new = jnp.maximum(m_prev, jnp.max(st, axis=0, keepdims=True))
        alpha = jnp.exp2(m_prev - m_new)
        p = jnp.exp2(st - m_new)
        l_scr[...] = alpha * l_scr[...] + jnp.sum(p, axis=0, keepdims=True)
        acc_scr[...] = alpha * acc_scr[...] + jnp.dot(vt_ref[...], p.astype(BF16), preferred_element_type=F32)
        m_scr[...] = m_new

    @pl.when(kj < qi)
    def _():
        update(st)

    @pl.when(kj == qi)
    def _():
        key = lax.broadcasted_iota(jnp.int32, st.shape, 0)
        qry = lax.broadcasted_iota(jnp.int32, st.shape, 1)
        update(jnp.where(key <= qry, st, NEG_BIG))
        o_ref[...] = (acc_scr[...] / l_scr[...]).T.astype(o_ref.dtype)


def mla_prompt_attention(q_heads, k_heads, vt_heads, tp):
    n_heads, _, qk = q_heads.shape
    dv = vt_heads.shape[1]
    tile = _pick(tp, ATTN_TILE)
    n_blocks = tp // tile
    pairs = [(i, j) for i in range(n_blocks) for j in range(i + 1)]
    qi_list = jnp.asarray([p[0] for p in pairs], jnp.int32)
    kj_list = jnp.asarray([p[1] for p in pairs], jnp.int32)
    grid_spec = pltpu.PrefetchScalarGridSpec(
        num_scalar_prefetch=2,
        grid=(n_heads, len(pairs)),
        in_specs=[pl.BlockSpec((None, tile, qk), lambda h, s, qi, kj: (h, qi[s], 0)),
                  pl.BlockSpec((None, tile, qk), lambda h, s, qi, kj: (h, kj[s], 0)),
                  pl.BlockSpec((None, dv, tile), lambda h, s, qi, kj: (h, 0, kj[s]))],
        out_specs=pl.BlockSpec((tile, dv), lambda h, s, qi, kj: (qi[s], h)),
        scratch_shapes=[pltpu.VMEM((1, tile), F32), pltpu.VMEM((1, tile), F32), pltpu.VMEM((dv, tile), F32)],
    )
    return pl.pallas_call(
        functools.partial(_flash_kernel, tile=tile),
        grid_spec=grid_spec,
        out_shape=jax.ShapeDtypeStruct((tp, n_heads * dv), BF16),
        compiler_params=_params("parallel", "arbitrary"),
        name="mla_prompt_attention",
    )(qi_list, kj_list, q_heads, k_heads, vt_heads)


def _absorb_kernel(q_ref, wk_ref, o_ref, *, nope):
    q = q_ref[...]
    rank = wk_ref.shape[0]
    o_ref[:, :rank] = _dot_nt(q[:, :nope], wk_ref[...]).astype(o_ref.dtype)
    o_ref[:, rank:] = q[:, nope:]


def mla_absorb_query(q_heads, w_uk_heads, tp):
    n_heads, t, qk = q_heads.shape
    _, rank, nope = w_uk_heads.shape
    ts = t - tp
    tm = _pick(math.gcd(ts, tp), 1024)
    off = tp // tm
    return pl.pallas_call(
        functools.partial(_absorb_kernel, nope=nope),
        grid=(n_heads, ts // tm),
        in_specs=[pl.BlockSpec((None, tm, qk), lambda h, i: (h, off + i, 0)),
                  pl.BlockSpec((None, rank, nope), lambda h, i: (h, 0, 0))],
        out_specs=pl.BlockSpec((None, tm, rank + qk - nope), lambda h, i: (h, i, 0)),
        out_shape=jax.ShapeDtypeStruct((n_heads, ts, rank + qk - nope), BF16),
        compiler_params=_params("parallel", "parallel"),
        name="mla_absorb_query",
    )(q_heads, w_uk_heads)


def _paged_kernel(pt_ref, q_ref, cn_ref, kn_ref, ckv_hbm, kpe_hbm, o_ref, m_scr, l_scr, acc_scr, ckbuf, kpbuf, sem,
                  ck_scr, kp_scr, *, pages_per_step, rank, dec_seq, page, n_sub):
    s_idx = pl.program_id(0)
    j = pl.program_id(1)
    n_j = pl.num_programs(1)
    step = s_idx * n_j + j
    n_steps = pl.num_programs(0) * n_j
    slot = step % 2

    def page_copies(seq, blk, to_slot):
        copies = []
        for pg in range(pages_per_step):
            pid = pt_ref[seq, blk * pages_per_step + pg]
            copies.append(pltpu.make_async_copy(
                ckv_hbm.at[pid], ckbuf.at[to_slot, pl.ds(pg * page, page), :], sem.at[to_slot]))
            copies.append(pltpu.make_async_copy(
                kpe_hbm.at[pid], kpbuf.at[to_slot, :, pl.ds(pg * page, page)], sem.at[to_slot]))
        return copies

    @pl.when(step == 0)
    def _():
        for cp in page_copies(s_idx, j, slot):
            cp.start()

    for cp in page_copies(s_idx, j, slot):
        cp.wait()

    n_heads = q_ref.shape[0]
    q = q_ref[...].reshape(n_heads * dec_seq, q_ref.shape[2])
    q_lat, q_pe = q[:, :rank], q[:, rank:]

    @pl.when(j == 0)
    def _():
        pad = lambda a: jnp.concatenate([a, jnp.zeros((LANES - dec_seq, a.shape[1]), a.dtype)], axis=0)
        cn = pad(cn_ref[...]).astype(BF16)
        s = _dot_nt(q_lat, cn) + _dot_nt(q_pe, pad(kn_ref[...]))
        qtok = lax.broadcasted_iota(jnp.int32, s.shape, 0) & (dec_seq - 1)
        ktok = lax.broadcasted_iota(jnp.int32, s.shape, 1)
        s = jnp.where(ktok <= qtok, s, NEG_BIG)
        m0 = jnp.max(s, axis=-1, keepdims=True)
        p = jnp.exp(s - m0)
        m_scr[...] = m0
        l_scr[...] = jnp.sum(p, axis=-1, keepdims=True)
        acc_scr[...] = jnp.dot(p.astype(BF16), cn, preferred_element_type=F32)

    nxt = jnp.where(step + 1 < n_steps, step + 1, 0)
    for cp in page_copies(nxt // n_j, nxt % n_j, 1 - slot):
        cp.start()

    for pg in range(pages_per_step):
        ck_scr[pg * page:(pg + 1) * page, :] = ckbuf[slot, pg * page:(pg + 1) * page, :].astype(BF16)
    kp_scr[...] = kpbuf[slot].astype(BF16)

    sub = pages_per_step * page // n_sub
    cks = [ck_scr[i * sub:(i + 1) * sub, :] for i in range(n_sub)]
    ss = [lax.dot_general(q_lat, cks[i], (((1,), (1,)), ((), ())), preferred_element_type=F32)
          + jnp.dot(q_pe, kp_scr[:, i * sub:(i + 1) * sub], preferred_element_type=F32) for i in range(n_sub)]
    ms = [jnp.max(s, axis=-1, keepdims=True) for s in ss]
    ps = [jnp.exp(s - m) for s, m in zip(ss, ms)]
    ls = [jnp.sum(p, axis=-1, keepdims=True) for p in ps]
    accs = [jnp.dot(p.astype(BF16), ck, preferred_element_type=F32) for p, ck in zip(ps, cks)]
    m_prev = m_scr[...]
    m_new = functools.reduce(jnp.maximum, ms, m_prev)
    alpha = jnp.exp(m_prev - m_new)
    l_new = alpha * l_scr[...]
    acc_new = alpha * acc_scr[...]
    for m, l, acc in zip(ms, ls, accs):
        coef = jnp.exp(m - m_new)
        l_new = l_new + coef * l
        acc_new = acc_new + coef * acc
    m_scr[...] = m_new
    l_scr[...] = l_new
    acc_scr[...] = acc_new

    @pl.when(j == n_j - 1)
    def _():
        o_ref[...] = (acc_new / l_new).reshape(o_ref.shape).astype(o_ref.dtype)

    @pl.when(step == n_steps - 1)
    def _():
        for cp in page_copies(0, 0, 1 - slot):
            cp.wait()


def mla_paged_attention(q_abs, ckv_new, kpe_new, cache_ckv, cache_kpe_t, page_table, dec_seq):
    n_heads, ts, key_dim = q_abs.shape
    n_seq, n_pages = page_table.shape
    _, page, rank = cache_ckv.shape
    rope_dim = cache_kpe_t.shape[1]
    pps = _divisor(n_pages, PAGES_PER_STEP)
    rows = n_heads * dec_seq
    keys = pps * page
    grid_spec = pltpu.PrefetchScalarGridSpec(
        num_scalar_prefetch=1,
        grid=(n_seq, n_pages // pps),
        in_specs=[pl.BlockSpec((n_heads, dec_seq, key_dim), lambda s, j, pt: (0, s, 0)),
                  pl.BlockSpec((dec_seq, rank), lambda s, j, pt: (s, 0)),
                  pl.BlockSpec((dec_seq, rope_dim), lambda s, j, pt: (s, 0)),
                  pl.BlockSpec(memory_space=pl.ANY),
                  pl.BlockSpec(memory_space=pl.ANY)],
        out_specs=pl.BlockSpec((n_heads, dec_seq, rank), lambda s, j, pt: (0, s, 0)),
        scratch_shapes=[pltpu.VMEM((rows, 1), F32), pltpu.VMEM((rows, 1), F32), pltpu.VMEM((rows, rank), F32),
                        pltpu.VMEM((2, keys, rank), F32), pltpu.VMEM((2, rope_dim, keys), F32),
                        pltpu.SemaphoreType.DMA((2,)),
                        pltpu.VMEM((keys, rank), BF16), pltpu.VMEM((rope_dim, keys), BF16)],
    )
    return pl.pallas_call(
        functools.partial(_paged_kernel, pages_per_step=pps, rank=rank, dec_seq=dec_seq, page=page,
                          n_sub=_divisor(pps, PAGED_SUB_BLOCKS)),
        grid_spec=grid_spec,
        out_shape=jax.ShapeDtypeStruct((n_heads, ts, rank), BF16),
        compiler_params=_params("arbitrary", "arbitrary"),
        name="mla_paged_attention",
    )(page_table, q_abs, ckv_new, kpe_new, cache_ckv, cache_kpe_t)


def _unabsorb_kernel(o_ref, wv_ref, out_ref):
    out_ref[...] = _dot(o_ref[...], wv_ref[...]).astype(out_ref.dtype)


def mla_unabsorb(o_lat, w_uv_heads):
    n_heads, ts, rank = o_lat.shape
    dv = w_uv_heads.shape[2]
    tm = _pick(ts, 1024)
    return pl.pallas_call(
        _unabsorb_kernel,
        grid=(n_heads, ts // tm),
        in_specs=[pl.BlockSpec((None, tm, rank), lambda h, i: (h, i, 0)),
                  pl.BlockSpec((None, rank, dv), lambda h, i: (h, 0, 0))],
        out_specs=pl.BlockSpec((tm, dv), lambda h, i: (i, h)),
        out_shape=jax.ShapeDtypeStruct((ts, n_heads * dv), BF16),
        compiler_params=_params("parallel", "parallel"),
        name="mla_unabsorb",
    )(o_lat, w_uv_heads)


def kernel(x_prompt, x_sample, state_ssm, state_conv, cache_ckv, cache_kpe, page_table, norm_mix, norm_ffn,
           norm_final, gdn_w_in, gdn_conv_w, gdn_a_log, gdn_dt_bias, gdn_norm, gdn_w_out, kv_norm, w_dkv,
           kv_latent_norm, w_uk, w_uv, mla_w_q, mla_w_o, moe_w_group, moe_w_expert, moe_w1, moe_w3, moe_w2):
    batch, seq, d = x_prompt.shape
    n_seq, dec_seq, _ = x_sample.shape
    assert batch == 1 and dec_seq == SUBLANES
    tp, ts = seq, n_seq * dec_seq
    depth = norm_mix.shape[0]
    n_a = state_ssm.shape[0]
    n_vh, dk, dv = state_ssm.shape[2:]
    conv_dim = state_conv.shape[3]
    qk_dim = (conv_dim - n_vh * dv) // 2
    n_kh = qk_dim // dk
    n_taps = gdn_conv_w.shape[1]
    rank = kv_latent_norm.shape[0]
    rope_dim = w_dkv.shape[1] - rank
    n_heads, nope = w_uk.shape[1:]
    past_len = page_table.shape[1] * cache_ckv.shape[1]
    scale = float((nope + rope_dim) ** -0.5)

    h = jnp.concatenate([x_prompt.reshape(tp, d), x_sample.reshape(ts, d)], axis=0)
    positions = np.concatenate([np.arange(tp), past_len + np.arange(dec_seq)[None, :].repeat(n_seq, 0).reshape(-1)])
    cos2, sin2 = _rope_tables(positions, rope_dim // 2)

    ssm_p, conv_p, ssm_s, conv_s = [], [], [], []
    ckv = kpe = None
    for i in range(depth):
        if i < n_a:
            xn = rmsnorm(h, norm_mix[i], BF16)
            w_in = gdn_w_in[i]
            main_cols = conv_dim + n_vh * dv
            proj = matmul(xn, w_in, F32, n_cols=main_cols)
            gates = gdn_gates(xn, w_in[:, main_cols:], gdn_a_log[i], gdn_dt_bias[i])
            hist = jnp.pad(state_conv[i], ((0, 0), (SUBLANES - (n_taps - 1), 0), (0, 0)))
            qkv = gdn_conv(proj, hist, gdn_conv_w[i], tp, qk_dim, conv_dim, dk)
            gates_t = gates.T
            o_p, s_p = gdn_prompt(qkv, proj, gates_t, gdn_norm[i], tp, n_kh, n_vh, dk, dv, GDN_CHUNK)
            o_s, s_s = gdn_sample(qkv, proj, gates_t, gdn_norm[i], state_ssm[i], tp, n_kh, n_vh, dk, dv, dec_seq)
            h = matmul((o_p, o_s), gdn_w_out[i], F32, res=h, tm_pref=512, tn_pref=512)
            ssm_p.append(s_p[None])
            ssm_s.append(s_s)
            conv_p.append(proj[tp - (n_taps - 1):tp, :conv_dim][None])
            conv_s.append(proj[tp:, :conv_dim].reshape(n_seq, dec_seq, conv_dim)[:, dec_seq - (n_taps - 1):])
        else:
            if i == n_a:
                ckv, kpe = mla_shared_kv(rmsnorm(h, kv_norm, BF16), w_dkv, kv_latent_norm, cos2, sin2, rank)
                w_uk_heads = jnp.transpose(w_uk, (1, 0, 2))
                w_uv_heads = jnp.transpose(w_uv, (1, 0, 2))
                k_heads, vt_heads = mla_expand_kv(ckv, kpe, w_uk_heads, jnp.transpose(w_uv, (1, 2, 0)), tp)
                cache_kpe_t = jnp.swapaxes(cache_kpe, 1, 2)
            j = i - n_a
            xn = rmsnorm(h, norm_mix[i], BF16)
            q_heads = mla_query(xn, mla_w_q[j], cos2, sin2, n_heads, nope, scale, tp)
            o_p = mla_prompt_attention(q_heads, k_heads, vt_heads, tp)
            q_abs = mla_absorb_query(q_heads, w_uk_heads, tp)
            o_lat = mla_paged_attention(q_abs, ckv[tp:], kpe[tp:], cache_ckv, cache_kpe_t, page_table, dec_seq)
            o_s = mla_unabsorb(o_lat, w_uv_heads)
            h = matmul((o_p, o_s), mla_w_o[j], F32, res=h, tm_pref=512)
        h = hier_moe_layer(h, norm_ffn[i], moe_w_group[i], moe_w_expert[i], moe_w1, moe_w3, moe_w2, i)

    y_prompt = rmsnorm(h, norm_final, F32, 0, tp).reshape(batch, seq, d)
    y_sample = rmsnorm(h, norm_final, F32, tp, ts).reshape(n_seq, dec_seq, d)
    return (y_prompt, y_sample,
            jnp.stack(ssm_p), jnp.stack(conv_p),
            ckv[:tp].reshape(batch, seq, rank), kpe[:tp].reshape(batch, seq, rope_dim),
            jnp.stack(ssm_s), jnp.stack(conv_s),
            ckv[tp:].reshape(n_seq, dec_seq, rank), kpe[tp:].reshape(n_seq, dec_seq, rope_dim))
```
